```python
import math
import jax, jax.numpy as jnp
from jax import lax
import numpy as np

D_MODEL = 1024
BATCH = 2
SEQ = 8192
DEPTH = 4

N_MIXERS = 3
RET_HEADS = 4
RET_DK = D_MODEL // RET_HEADS
RET_DV = 2 * RET_DK
RET_CHUNK = 128
GN_EPS = 1e-5
LRU_WIDTH = (5 * D_MODEL) // 4
LRU_BLOCKS = 10
LRU_BLOCK_DIM = LRU_WIDTH // LRU_BLOCKS
CONV_WIDTH = 4
LRU_C = 8.0
MLA_HEADS = 8
MLA_NOPE = 128
MLA_ROPE = 64
MLA_V = 128
MLA_Q_RANK = 384
MLA_KV_RANK = 256
ATTN_BLOCK = 128
D_FF = 4 * D_MODEL
ROPE_BASE = 10000.0
LN_EPS = 1e-5
RMS_EPS = 1e-6
ALPHA = (2.0 * DEPTH) ** 0.25
BETA = (8.0 * DEPTH) ** -0.25
N_RET = (DEPTH + 2) // 3
N_LRU = (DEPTH + 1) // 3
N_MLA = DEPTH // 3

kernel_name = 'hybrid_retention_rglru_mla_deepnorm'


def layer_norm(x, g, b):
    xf = x.astype(jnp.float32)
    mu = xf.mean(-1, keepdims=True)
    var = jnp.square(xf - mu).mean(-1, keepdims=True)
    return ((xf - mu) * lax.rsqrt(var + LN_EPS) * g + b).astype(x.dtype)


def rms_norm(x, g):
    xf = x.astype(jnp.float32)
    y = xf * lax.rsqrt(jnp.mean(jnp.square(xf), -1, keepdims=True) + RMS_EPS)
    return (y * g).astype(x.dtype)


def rope(x, positions):
    half = x.shape[-1] // 2
    inv = ROPE_BASE ** (-jnp.arange(half, dtype=jnp.float32) / half)
    ang = positions.astype(jnp.float32)[..., None] * inv
    cos = jnp.cos(ang)[:, :, None, :]
    sin = jnp.sin(ang)[:, :, None, :]
    xf = x.astype(jnp.float32)
    x1, x2 = xf[..., :half], xf[..., half:]
    return jnp.concatenate([x1 * cos - x2 * sin, x1 * sin + x2 * cos], -1).astype(x.dtype)


def retention_mixer(x, positions, w_in, gn_g, w_o):
    B, S, _ = x.shape
    H, dk, dv, C = RET_HEADS, RET_DK, RET_DV, RET_CHUNK
    n = S // C
    q, k, v, g = jnp.split(x @ w_in, [H * dk, 2 * H * dk, 2 * H * dk + H * dv], axis=-1)
    q = rope(q.reshape(B, S, H, dk), positions).astype(jnp.float32)
    k = rope(k.reshape(B, S, H, dk), positions).astype(jnp.float32) * (dk ** -0.5)
    v = v.reshape(B, S, H, dv).astype(jnp.float32)
    log_gamma = jnp.log1p(-jnp.exp2(-5.0 - jnp.arange(H, dtype=jnp.float32)))
    idx = jnp.arange(C, dtype=jnp.float32)
    diff = idx[:, None] - idx[None, :]
    intra_decay = jnp.where(diff >= 0, jnp.exp(log_gamma[:, None, None] * jnp.maximum(diff, 0.0)), 0.0)
    q_decay = jnp.exp(log_gamma[:, None] * (idx + 1.0))[None, :, :, None]
    k_decay = jnp.exp(log_gamma[:, None] * (C - 1.0 - idx))[None, :, :, None]
    chunk_decay = jnp.exp(log_gamma * C)[None, :, None, None]

    def to_chunks(t):
        return t.reshape(B, n, C, H, t.shape[-1]).transpose(1, 0, 3, 2, 4)

    def step(state, inp):
        qi, ki, vi = inp
        s = jnp.einsum('bhid,bhjd->bhij', qi, ki) * intra_decay
        inner = jnp.einsum('bhij,bhjv->bhiv', s, vi)
        cross = jnp.einsum('bhid,bhdv->bhiv', qi * q_decay, state)
        state = state * chunk_decay + jnp.einsum('bhjd,bhjv->bhdv', ki * k_decay, vi)
        return state, inner + cross

    state0 = jnp.zeros((B, H, dk, dv), jnp.float32)
    _, out = lax.scan(step, state0, (to_chunks(q), to_chunks(k), to_chunks(v)))
    out = out.transpose(1, 0, 3, 2, 4).reshape(B, S, H, dv)
    mu = out.mean(-1, keepdims=True)
    var = jnp.square(out - mu).mean(-1, keepdims=True)
    y = ((out - mu) * lax.rsqrt(var + GN_EPS)).reshape(B, S, H * dv) * gn_g
    y = jax.nn.silu(g.astype(jnp.float32)) * y
    return y.astype(x.dtype) @ w_o


def rglru_mixer(x, w_in, conv_w, conv_b, w_a, b_a, w_x, b_x, lam, w_o):
    B, S, _ = x.shape
    W = LRU_WIDTH
    gate_branch, rec = jnp.split(x @ w_in, 2, axis=-1)
    gate_branch = jax.nn.gelu(gate_branch.astype(jnp.float32))
    u = lax.conv_general_dilated(
        rec.astype(jnp.float32), conv_w.astype(jnp.float32)[:, None, :],
        window_strides=(1,), padding=[(CONV_WIDTH - 1, 0)],
        dimension_numbers=('NWC', 'WIO', 'NWC'), feature_group_count=W) + conv_b
    ub = u.reshape(B, S, LRU_BLOCKS, LRU_BLOCK_DIM)
    r = jax.nn.sigmoid(jnp.einsum('bsnd,nde->bsne', ub, w_a.astype(jnp.float32)).reshape(B, S, W) + b_a)
    i = jax.nn.sigmoid(jnp.einsum('bsnd,nde->bsne', ub, w_x.astype(jnp.float32)).reshape(B, S, W) + b_x)
    log_a = -LRU_C * r * jax.nn.softplus(-lam.astype(jnp.float32))
    a = jnp.exp(log_a)
    b_in = jnp.sqrt(-jnp.expm1(2.0 * log_a)) * (i * u)

    def combine(left, right):
        a1, b1 = left
        a2, b2 = right
        return a1 * a2, a2 * b1 + b2

    _, h = lax.associative_scan(combine, (a, b_in), axis=1)
    return (h * gate_branch).astype(x.dtype) @ w_o


def mla_mixer(x, positions, w_in, q_norm_g, kv_norm_g, w_uq, w_ukv, w_o):
    B, S, _ = x.shape
    H, dqk = MLA_HEADS, MLA_NOPE + MLA_ROPE
    c_q, c_kv, k_pe = jnp.split(x @ w_in, [MLA_Q_RANK, MLA_Q_RANK + MLA_KV_RANK], axis=-1)
    q = (rms_norm(c_q, q_norm_g) @ w_uq).reshape(B, S, H, dqk)
    q = jnp.concatenate([q[..., :MLA_NOPE], rope(q[..., MLA_NOPE:], positions)], -1) * (dqk ** -0.5)
    kv = (rms_norm(c_kv, kv_norm_g) @ w_ukv).reshape(B, S, H, MLA_NOPE + MLA_V)
    k_nope, v = kv[..., :MLA_NOPE], kv[..., MLA_NOPE:]
    k_pe = rope(k_pe[:, :, None, :], positions)
    k = jnp.concatenate([k_nope, jnp.broadcast_to(k_pe, (B, S, H, MLA_ROPE))], -1)
    nb = S // ATTN_BLOCK
    qb = q.reshape(B, nb, ATTN_BLOCK, H, dqk).transpose(1, 0, 2, 3, 4)
    key_pos = jnp.arange(S)

    def attend(args):
        qi, blk = args
        s = jnp.einsum('bqhd,bkhd->bhqk', qi, k, preferred_element_type=jnp.float32)
        qpos = blk * ATTN_BLOCK + jnp.arange(ATTN_BLOCK)
        s = jnp.where(key_pos[None, :] <= qpos[:, None], s, -jnp.inf)
        p = jax.nn.softmax(s, axis=-1).astype(v.dtype)
        return jnp.einsum('bhqk,bkhd->bqhd', p, v)

    o = lax.map(attend, (qb, jnp.arange(nb)))
    o = o.transpose(1, 0, 2, 3, 4).reshape(B, S, H * MLA_V)
    return o @ w_o


def sq_relu_mlp(x, w1, w2):
    return jnp.square(jax.nn.relu(x @ w1)) @ w2


def _normal(key, shape, std):
    return jax.random.normal(key, shape, jnp.float32) * std


def setup_inputs(seed: int = 0) -> dict:
    key = jax.random.key(seed)
    ks = list(jax.random.split(key, 32))
    D = D_MODEL
    x = jax.random.normal(ks[0], (BATCH, SEQ, D), jnp.float32)
    positions = jnp.broadcast_to(jnp.arange(SEQ, dtype=jnp.int32), (BATCH, SEQ))
    ret_qk = _normal(ks[1], (N_RET, D, 2 * RET_HEADS * RET_DK), D ** -0.5)
    ret_v = _normal(ks[2], (N_RET, D, RET_HEADS * RET_DV), BETA * D ** -0.5)
    ret_g = _normal(ks[3], (N_RET, D, RET_HEADS * RET_DV), D ** -0.5)
    ret_w_in = jnp.concatenate([ret_qk, ret_v, ret_g], -1)
    ret_gn_g = 1.0 + _normal(ks[4], (N_RET, RET_HEADS * RET_DV), 0.02)
    ret_w_o = _normal(ks[5], (N_RET, RET_HEADS * RET_DV, D), BETA * (RET_HEADS * RET_DV) ** -0.5)
    lru_w_in = _normal(ks[6], (N_LRU, D, 2 * LRU_WIDTH), D ** -0.5)
    lru_conv_w = _normal(ks[7], (N_LRU, CONV_WIDTH, LRU_WIDTH), CONV_WIDTH ** -0.5)
    lru_conv_b = _normal(ks[8], (N_LRU, LRU_WIDTH), 0.01)
    lru_w_a = _normal(ks[9], (N_LRU, LRU_BLOCKS, LRU_BLOCK_DIM, LRU_BLOCK_DIM), LRU_BLOCK_DIM ** -0.5)
    lru_b_a = _normal(ks[10], (N_LRU, LRU_WIDTH), 0.01)
    lru_w_x = _normal(ks[11], (N_LRU, LRU_BLOCKS, LRU_BLOCK_DIM, LRU_BLOCK_DIM), LRU_BLOCK_DIM ** -0.5)
    lru_b_x = _normal(ks[12], (N_LRU, LRU_WIDTH), 0.01)
    a_c = jax.random.uniform(ks[13], (N_LRU, LRU_WIDTH), jnp.float32, minval=0.9, maxval=0.999)
    s = a_c ** (1.0 / LRU_C)
    lru_lam = jnp.log(s) - jnp.log1p(-s)
    lru_w_o = _normal(ks[14], (N_LRU, LRU_WIDTH, D), BETA * LRU_WIDTH ** -0.5)
    mla_w_in = _normal(ks[15], (N_MLA, D, MLA_Q_RANK + MLA_KV_RANK + MLA_ROPE), D ** -0.5)
    mla_q_norm = 1.0 + _normal(ks[16], (N_MLA, MLA_Q_RANK), 0.02)
    mla_kv_norm = 1.0 + _normal(ks[17], (N_MLA, MLA_KV_RANK), 0.02)
    mla_w_uq = _normal(ks[18], (N_MLA, MLA_Q_RANK, MLA_HEADS * (MLA_NOPE + MLA_ROPE)), MLA_Q_RANK ** -0.5)
    uk = _normal(ks[19], (N_MLA, MLA_KV_RANK, MLA_HEADS, MLA_NOPE), MLA_KV_RANK ** -0.5)
    uv = _normal(ks[20], (N_MLA, MLA_KV_RANK, MLA_HEADS, MLA_V), BETA * MLA_KV_RANK ** -0.5)
    mla_w_ukv = jnp.concatenate([uk, uv], -1).reshape(N_MLA, MLA_KV_RANK, MLA_HEADS * (MLA_NOPE + MLA_V))
    mla_w_o = _normal(ks[21], (N_MLA, MLA_HEADS * MLA_V, D), BETA * (MLA_HEADS * MLA_V) ** -0.5)
    ln_g = 1.0 + _normal(ks[22], (DEPTH, 2, D), 0.02)
    ln_b = _normal(ks[23], (DEPTH, 2, D), 0.02)
    mlp_w1 = _normal(ks[24], (DEPTH, D, D_FF), BETA * D ** -0.5)
    mlp_w2 = _normal(ks[25], (DEPTH, D_FF, D), BETA * D_FF ** -0.5)
    return {'x': x, 'positions': positions,
            'ret_w_in': ret_w_in, 'ret_gn_g': ret_gn_g, 'ret_w_o': ret_w_o,
            'lru_w_in': lru_w_in, 'lru_conv_w': lru_conv_w, 'lru_conv_b': lru_conv_b,
            'lru_w_a': lru_w_a, 'lru_b_a': lru_b_a, 'lru_w_x': lru_w_x, 'lru_b_x': lru_b_x,
            'lru_lam': lru_lam, 'lru_w_o': lru_w_o,
            'mla_w_in': mla_w_in, 'mla_q_norm': mla_q_norm, 'mla_kv_norm': mla_kv_norm,
            'mla_w_uq': mla_w_uq, 'mla_w_ukv': mla_w_ukv, 'mla_w_o': mla_w_o,
            'ln_g': ln_g, 'ln_b': ln_b, 'mlp_w1': mlp_w1, 'mlp_w2': mlp_w2}


def reference(x, positions, ret_w_in, ret_gn_g, ret_w_o,
              lru_w_in, lru_conv_w, lru_conv_b, lru_w_a, lru_b_a, lru_w_x, lru_b_x, lru_lam, lru_w_o,
              mla_w_in, mla_q_norm, mla_kv_norm, mla_w_uq, mla_w_ukv, mla_w_o,
              ln_g, ln_b, mlp_w1, mlp_w2):
    h = x
    for i in range(DEPTH):
        kind, j = i % N_MIXERS, i // N_MIXERS
        if kind == 0:
            mix = retention_mixer(h, positions, ret_w_in[j], ret_gn_g[j], ret_w_o[j])
        elif kind == 1:
            mix = rglru_mixer(h, lru_w_in[j], lru_conv_w[j], lru_conv_b[j], lru_w_a[j], lru_b_a[j],
                              lru_w_x[j], lru_b_x[j], lru_lam[j], lru_w_o[j])
        else:
            mix = mla_mixer(h, positions, mla_w_in[j], mla_q_norm[j], mla_kv_norm[j],
                            mla_w_uq[j], mla_w_ukv[j], mla_w_o[j])
        h = layer_norm(ALPHA * h + mix, ln_g[i, 0], ln_b[i, 0])
        h = layer_norm(ALPHA * h + sq_relu_mlp(h, mlp_w1[i], mlp_w2[i]), ln_g[i, 1], ln_b[i, 1])
    return h
```

```python
import functools
import math

import jax
import jax.numpy as jnp
from jax import lax
from jax.experimental import pallas as pl
from jax.experimental.pallas import tpu as pltpu

F32 = jnp.float32
BF16 = jnp.bfloat16

DEPTH = 4
N_MIXERS = 3
RET_HEADS = 4
GN_EPS = 1e-5
LRU_BLOCKS = 10
CONV_WIDTH = 4
LRU_C = 8.0
MLA_HEADS = 8
MLA_NOPE = 128
MLA_ROPE = 64
MLA_V = 128
MLA_Q_RANK = 384
MLA_KV_RANK = 256
ROPE_BASE = 10000.0
LN_EPS = 1e-5
RMS_EPS = 1e-6
ALPHA = (2.0 * DEPTH) ** 0.25

RET_TILE = 256
LRU_TILE = 256
MLP_TILE = 512
MLA_PROJ_TILE = 256
ATTN_TQ = 512
ATTN_TK = 512
OUT_TILE = 512
FF_CHUNK = 1024

V7X_VMEM_LIMIT = 56 * 1024 * 1024


def _resident(shape):
    nd = len(shape)
    return pl.BlockSpec(shape, lambda *_: (0,) * nd, pipeline_mode=pl.Buffered(1))


def _dot(a, b):
    return jnp.dot(a, b, preferred_element_type=F32)


def _dot_nt(a, b):
    return lax.dot_general(a, b, (((1,), (1,)), ((), ())), preferred_element_type=F32)


def _layer_norm(z, g, b):
    mu = jnp.mean(z, axis=-1, keepdims=True)
    zc = z - mu
    var = jnp.mean(zc * zc, axis=-1, keepdims=True)
    return zc * lax.rsqrt(var + LN_EPS) * g + b


def _rms_norm(z, g):
    return z * lax.rsqrt(jnp.mean(z * z, axis=-1, keepdims=True) + RMS_EPS) * g


def _sigmoid(z):
    return jax.nn.sigmoid(z)


def _mlp_kernel(h_ref, w1_ref, w2_ref, g_ref, b_ref, o_ref):
    x = h_ref[...]
    xb = x.astype(BF16)
    acc = ALPHA * x
    d_ff = w1_ref.shape[1]
    for c in range(d_ff // FF_CHUNK):
        a = _dot(xb, w1_ref[:, c * FF_CHUNK:(c + 1) * FF_CHUNK])
        a = jnp.square(jnp.maximum(a, 0.0)).astype(BF16)
        acc = acc + _dot(a, w2_ref[c * FF_CHUNK:(c + 1) * FF_CHUNK, :])
    o_ref[...] = _layer_norm(acc, g_ref[...], b_ref[...])


def _mlp_layer(h, w1, w2, g, b):
    m, d = h.shape
    d_ff = w1.shape[1]
    return pl.pallas_call(
        _mlp_kernel,
        out_shape=jax.ShapeDtypeStruct((m, d), F32),
        grid=(m // MLP_TILE,),
        in_specs=[
            pl.BlockSpec((MLP_TILE, d), lambda i: (i, 0)),
            _resident((d, d_ff)),
            _resident((d_ff, d)),
            _resident((1, d)),
            _resident((1, d)),
        ],
        out_specs=pl.BlockSpec((MLP_TILE, d), lambda i: (i, 0)),
        compiler_params=pltpu.CompilerParams(
            dimension_semantics=("parallel",), vmem_limit_bytes=V7X_VMEM_LIMIT),
        name="mlp_ln",
    )(h, w1.astype(BF16), w2.astype(BF16), g.reshape(1, d), b.reshape(1, d))


def _ret_kernel(lg_ref, x_ref, pos_ref, inv_ref, w_in_ref, gn_ref, w_o_ref, g_ref, b_ref,
                o_ref, state_ref):
    c_idx = pl.program_id(1)
    heads, dk, dv = state_ref.shape
    half = dk // 2
    tile = x_ref.shape[0]

    @pl.when(c_idx == 0)
    def _():
        state_ref[...] = jnp.zeros_like(state_ref)

    x = x_ref[...]
    xb = x.astype(BF16)
    ang = pos_ref[...] * inv_ref[...]
    cos = jnp.cos(ang)
    sin = jnp.sin(ang)

    def rope(t):
        t1, t2 = t[:, :half], t[:, half:]
        return jnp.concatenate([t1 * cos - t2 * sin, t1 * sin + t2 * cos], axis=-1)

    row = lax.broadcasted_iota(jnp.int32, (tile, tile), 0)
    col = lax.broadcasted_iota(jnp.int32, (tile, tile), 1)
    diff = (row - col).astype(F32)
    causal = row >= col
    idx = lax.broadcasted_iota(jnp.int32, (tile, 1), 0).astype(F32)

    k_off = heads * dk
    v_off = 2 * heads * dk
    g_off = v_off + heads * dv
    mix = ALPHA * x
    for h in range(heads):
        lg = lg_ref[h]
        q = rope(_dot(xb, w_in_ref[:, h * dk:(h + 1) * dk]))
        k = rope(_dot(xb, w_in_ref[:, k_off + h * dk:k_off + (h + 1) * dk])) * (dk ** -0.5)
        v = _dot(xb, w_in_ref[:, v_off + h * dv:v_off + (h + 1) * dv])
        gate = _dot(xb, w_in_ref[:, g_off + h * dv:g_off + (h + 1) * dv])
        vb = v.astype(BF16)

        decay = jnp.where(causal, jnp.exp(lg * jnp.maximum(diff, 0.0)), 0.0)
        s = _dot_nt(q.astype(BF16), k.astype(BF16)) * decay
        inner = _dot(s.astype(BF16), vb)

        st = state_ref[h]
        q_dec = q * jnp.exp(lg * (idx + 1.0))
        cross = _dot(q_dec.astype(BF16), st.astype(BF16))
        k_dec = k * jnp.exp(lg * (tile - 1.0 - idx))
        state_ref[h] = st * jnp.exp(lg * tile) + _dot(k_dec.T.astype(BF16), vb)

        o = inner + cross
        mu = jnp.mean(o, axis=-1, keepdims=True)
        oc = o - mu
        var = jnp.mean(oc * oc, axis=-1, keepdims=True)
        y = oc * lax.rsqrt(var + GN_EPS) * gn_ref[:, h * dv:(h + 1) * dv]
        y = gate * _sigmoid(gate) * y
        mix = mix + _dot(y.astype(BF16), w_o_ref[h * dv:(h + 1) * dv, :])

    o_ref[...] = _layer_norm(mix, g_ref[...], b_ref[...])


def _retention_layer(h, pos, w_in, gn_g, w_o, g, b):
    bsz, seq, d = h.shape
    heads = RET_HEADS
    dk = d // heads
    dv = 2 * dk
    n_in = w_in.shape[1]
    log_gamma = jnp.log1p(-jnp.exp2(-5.0 - jnp.arange(heads, dtype=F32)))
    half = dk // 2
    inv = (ROPE_BASE ** (-jnp.arange(half, dtype=F32) / half)).reshape(1, half)
    tile = RET_TILE
    grid_spec = pltpu.PrefetchScalarGridSpec(
        num_scalar_prefetch=1,
        grid=(bsz, seq // tile),
        in_specs=[
            pl.BlockSpec((None, tile, d), lambda bi, ci, lg: (bi, ci, 0)),
            pl.BlockSpec((None, tile, 1), lambda bi, ci, lg: (bi, ci, 0)),
            _resident((1, half)),
            _resident((d, n_in)),
            _resident((1, heads * dv)),
            _resident((heads * dv, d)),
            _resident((1, d)),
            _resident((1, d)),
        ],
        out_specs=pl.BlockSpec((None, tile, d), lambda bi, ci, lg: (bi, ci, 0)),
        scratch_shapes=[pltpu.VMEM((heads, dk, dv), F32)],
    )
    return pl.pallas_call(
        _ret_kernel,
        out_shape=jax.ShapeDtypeStruct((bsz, seq, d), F32),
        grid_spec=grid_spec,
        compiler_params=pltpu.CompilerParams(
            dimension_semantics=("parallel", "arbitrary"), vmem_limit_bytes=V7X_VMEM_LIMIT),
        name="retention_layer",
    )(log_gamma, h, pos, inv, w_in.astype(BF16), gn_g.reshape(1, -1), w_o.astype(BF16),
      g.reshape(1, d), b.reshape(1, d))


def _gelu_tanh(z):
    c = math.sqrt(2.0 / math.pi)
    return 0.5 * z * (1.0 + jnp.tanh(c * (z + 0.044715 * (z * z * z))))


def _softplus(z):
    return jnp.maximum(z, 0.0) + jnp.log1p(jnp.exp(-jnp.abs(z)))


def _lru_kernel(x_ref, w_in_ref, conv_w_ref, conv_b_ref, w_a_ref, b_a_ref, w_x_ref, b_x_ref,
                lam_ref, w_o_ref, g_ref, b_ref, o_ref, tail_ref, h_ref):
    t_idx = pl.program_id(1)
    tile = x_ref.shape[0]
    width = conv_b_ref.shape[1]
    nblk, bdim, _ = w_a_ref.shape

    @pl.when(t_idx == 0)
    def _():
        tail_ref[...] = jnp.zeros_like(tail_ref)
        h_ref[...] = jnp.zeros_like(h_ref)

    x = x_ref[...]
    xb = x.astype(BF16)
    gate = _gelu_tanh(_dot(xb, w_in_ref[:, :width]))
    rec = _dot(xb, w_in_ref[:, width:])

    tail = tail_ref[...]
    row8 = lax.broadcasted_iota(jnp.int32, (8, width), 0)
    u = rec * conv_w_ref[CONV_WIDTH - 1:CONV_WIDTH, :] + conv_b_ref[...]
    for s in range(1, CONV_WIDTH):
        shifted = pltpu.roll(rec, s, 0)
        head = jnp.where(row8 < s, pltpu.roll(tail, s, 0), shifted[:8])
        shifted = jnp.concatenate([head, shifted[8:]], axis=0)
        u = u + shifted * conv_w_ref[CONV_WIDTH - 1 - s:CONV_WIDTH - s, :]
    tail_ref[...] = rec[tile - 8:, :]

    pre_a = []
    pre_x = []
    for n in range(nblk):
        ub = u[:, n * bdim:(n + 1) * bdim].astype(BF16)
        pre_a.append(_dot(ub, w_a_ref[n]))
        pre_x.append(_dot(ub, w_x_ref[n]))
    r = _sigmoid(jnp.concatenate(pre_a, axis=-1) + b_a_ref[...])
    i_gate = _sigmoid(jnp.concatenate(pre_x, axis=-1) + b_x_ref[...])
    log_a = (-LRU_C) * r * _softplus(-lam_ref[...])
    a = jnp.exp(log_a)
    b_in = jnp.sqrt(1.0 - jnp.exp(2.0 * log_a)) * (i_gate * u)

    rows = lax.broadcasted_iota(jnp.int32, (tile, width), 0)
    step = 1
    while step < tile:
        a_sh = pltpu.roll(a, step, 0)
        b_sh = pltpu.roll(b_in, step, 0)
        live = rows >= step
        b_in = jnp.where(live, a * b_sh + b_in, b_in)
        a = jnp.where(live, a * a_sh, a)
        step *= 2
    hs = a * h_ref[...] + b_in
    h_ref[...] = hs[tile - 1:tile, :]

    y = (hs * gate).astype(BF16)
    mix = ALPHA * x + _dot(y, w_o_ref[...])
    o_ref[...] = _layer_norm(mix, g_ref[...], b_ref[...])


def _rglru_layer(h, w_in, conv_w, conv_b, w_a, b_a, w_x, b_x, lam, w_o, g, b):
    bsz, seq, d = h.shape
    width = conv_b.shape[0]
    tile = LRU_TILE
    row = lambda t: t.reshape(1, -1)
    return pl.pallas_call(
        _lru_kernel,
        out_shape=jax.ShapeDtypeStruct((bsz, seq, d), F32),
        grid=(bsz, seq // tile),
        in_specs=[
            pl.BlockSpec((None, tile, d), lambda bi, ti: (bi, ti, 0)),
            _resident((d, 2 * width)),
            _resident((CONV_WIDTH, width)),
            _resident((1, width)),
            _resident(w_a.shape),
            _resident((1, width)),
            _resident(w_x.shape),
            _resident((1, width)),
            _resident((1, width)),
            _resident((width, d)),
            _resident((1, d)),
            _resident((1, d)),
        ],
        out_specs=pl.BlockSpec((None, tile, d), lambda bi, ti: (bi, ti, 0)),
        scratch_shapes=[pltpu.VMEM((8, width), F32), pltpu.VMEM((1, width), F32)],
        compiler_params=pltpu.CompilerParams(
            dimension_semantics=("parallel", "arbitrary"), vmem_limit_bytes=V7X_VMEM_LIMIT),
        name="rglru_layer",
    )(h, w_in.astype(BF16), conv_w, row(conv_b), w_a.astype(BF16), row(b_a),
      w_x.astype(BF16), row(b_x), row(lam), w_o.astype(BF16), row(g), row(b))


def _mla_proj_kernel(x_ref, pos_ref, inv_ref, w_q_ref, w_kv_ref, w_pe_ref, w_pe_rot_ref,
                     qn_g_ref, kvn_g_ref, w_qn_ref, w_qp_ref, w_qr_ref, w_kn_ref, w_v_ref,
                     q_ref, k_ref, v_ref):
    heads = w_qp_ref.shape[0]
    nope = MLA_NOPE
    scale = (MLA_NOPE + MLA_ROPE) ** -0.5
    xb = x_ref[...].astype(BF16)
    ang = pos_ref[...] * inv_ref[...]
    cos = jnp.cos(ang)
    sin = jnp.sin(ang)

    c_q = _rms_norm(_dot(xb, w_q_ref[...]), qn_g_ref[...]).astype(BF16)
    c_kv = _rms_norm(_dot(xb, w_kv_ref[...]), kvn_g_ref[...]).astype(BF16)
    k_pe = (_dot(xb, w_pe_ref[...]) * cos + _dot(xb, w_pe_rot_ref[...]) * sin).astype(BF16)

    q_nope = _dot(c_q, w_qn_ref[...]) * scale
    k_nope = _dot(c_kv, w_kn_ref[...])
    v_all = _dot(c_kv, w_v_ref[...])
    for h in range(heads):
        q_pe = (_dot(c_q, w_qp_ref[h]) * cos + _dot(c_q, w_qr_ref[h]) * sin) * scale
        q_ref[h, :, :nope] = q_nope[:, h * nope:(h + 1) * nope].astype(BF16)
        q_ref[h, :, nope:] = q_pe.astype(BF16)
        k_ref[h, :, :nope] = k_nope[:, h * nope:(h + 1) * nope].astype(BF16)
        k_ref[h, :, nope:] = k_pe
        v_ref[h] = v_all[:, h * MLA_V:(h + 1) * MLA_V].astype(BF16)


def _attn_kernel(q_ref, k_ref, v_ref, o_ref):
    i = pl.program_id(2)
    tq = q_ref.shape[0]
    tk = ATTN_TK
    dv = v_ref.shape[1]
    q = q_ref[...]

    def block(j, carry, masked):
        m_prev, l_prev, acc = carry
        start = pl.multiple_of(j * tk, tk)
        s = _dot_nt(q, k_ref[pl.ds(start, tk), :])
        if masked:
            row = lax.broadcasted_iota(jnp.int32, (tq, tk), 0)
            col = lax.broadcasted_iota(jnp.int32, (tq, tk), 1)
            s = jnp.where(col <= row, s, -jnp.inf)
        m_new = jnp.maximum(m_prev, jnp.max(s, axis=-1, keepdims=True))
        p = jnp.exp(s - m_new)
        alpha = jnp.exp(m_prev - m_new)
        l_new = alpha * l_prev + jnp.sum(p, axis=-1, keepdims=True)
        acc = alpha * acc + _dot(p.astype(BF16), v_ref[pl.ds(start, tk), :])
        return m_new, l_new, acc

    init = (jnp.full((tq, 1), -jnp.inf, F32), jnp.zeros((tq, 1), F32), jnp.zeros((tq, dv), F32))
    carry = lax.fori_loop(0, i, lambda j, c: block(j, c, False), init)
    _, l_fin, acc = block(i, carry, True)
    o_ref[...] = (acc / l_fin).astype(o_ref.dtype)


def _out_ln_kernel(h_ref, y_ref, w_o_ref, g_ref, b_ref, o_ref):
    mix = ALPHA * h_ref[...] + _dot(y_ref[...], w_o_ref[...])
    o_ref[...] = _layer_norm(mix, g_ref[...], b_ref[...])


def _rot_half_cols(w):
    half = w.shape[-1] // 2
    return jnp.concatenate([-w[..., half:], w[..., :half]], axis=-1)


def _mla_layer(h, pos, w_in, q_norm_g, kv_norm_g, w_uq, w_ukv, w_o, g, b):
    bsz, seq, d = h.shape
    heads, nope, rope_d, vd = MLA_HEADS, MLA_NOPE, MLA_ROPE, MLA_V
    dqk = nope + rope_d
    half = rope_d // 2
    inv = ROPE_BASE ** (-jnp.arange(half, dtype=F32) / half)
    inv = jnp.concatenate([inv, inv]).reshape(1, rope_d)

    w_q = w_in[:, :MLA_Q_RANK]
    w_kv = w_in[:, MLA_Q_RANK:MLA_Q_RANK + MLA_KV_RANK]
    w_pe = w_in[:, MLA_Q_RANK + MLA_KV_RANK:]
    w_uq3 = w_uq.reshape(MLA_Q_RANK, heads, dqk)
    w_qn = w_uq3[:, :, :nope].reshape(MLA_Q_RANK, heads * nope)
    w_qp = w_uq3[:, :, nope:].transpose(1, 0, 2)
    w_ukv3 = w_ukv.reshape(MLA_KV_RANK, heads, nope + vd)
    w_kn = w_ukv3[:, :, :nope].reshape(MLA_KV_RANK, heads * nope)
    w_v = w_ukv3[:, :, nope:].reshape(MLA_KV_RANK, heads * vd)

    tile = MLA_PROJ_TILE
    bf = lambda t: t.astype(BF16)
    q, k, v = pl.pallas_call(
        _mla_proj_kernel,
        out_shape=(jax.ShapeDtypeStruct((bsz, heads, seq, dqk), BF16),
                   jax.ShapeDtypeStruct((bsz, heads, seq, dqk), BF16),
                   jax.ShapeDtypeStruct((bsz, heads, seq, vd), BF16)),
        grid=(bsz, seq // tile),
        in_specs=[
            pl.BlockSpec((None, tile, d), lambda bi, ti: (bi, ti, 0)),
            pl.BlockSpec((None, tile, 1), lambda bi, ti: (bi, ti, 0)),
            _resident((1, rope_d)),
            _resident(w_q.shape), _resident(w_kv.shape), _resident(w_pe.shape),
            _resident(w_pe.shape),
            _resident((1, MLA_Q_RANK)), _resident((1, MLA_KV_RANK)),
            _resident(w_qn.shape), _resident(w_qp.shape), _resident(w_qp.shape),
            _resident(w_kn.shape), _resident(w_v.shape),
        ],
        out_specs=(pl.BlockSpec((None, heads, tile, dqk), lambda bi, ti: (bi, 0, ti, 0)),
                   pl.BlockSpec((None, heads, tile, dqk), lambda bi, ti: (bi, 0, ti, 0)),
                   pl.BlockSpec((None, heads, tile, vd), lambda bi, ti: (bi, 0, ti, 0))),
        compiler_params=pltpu.CompilerParams(
            dimension_semantics=("parallel", "parallel"), vmem_limit_bytes=V7X_VMEM_LIMIT),
        name="mla_proj",
    )(h, pos, inv, bf(w_q), bf(w_kv), bf(w_pe), bf(_rot_half_cols(w_pe)),
      q_norm_g.reshape(1, -1), kv_norm_g.reshape(1, -1),
      bf(w_qn), bf(w_qp), bf(_rot_half_cols(w_qp)), bf(w_kn), bf(w_v))

    attn = pl.pallas_call(
        _attn_kernel,
        out_shape=jax.ShapeDtypeStruct((bsz, seq, heads * vd), BF16),
        grid=(bsz, heads, seq // ATTN_TQ),
        in_specs=[
            pl.BlockSpec((None, None, ATTN_TQ, dqk), lambda bi, hi, qi: (bi, hi, qi, 0)),
            pl.BlockSpec((None, None, seq, dqk), lambda bi, hi, qi: (bi, hi, 0, 0)),
            pl.BlockSpec((None, None, seq, vd), lambda bi, hi, qi: (bi, hi, 0, 0)),
        ],
        out_specs=pl.BlockSpec((None, ATTN_TQ, vd), lambda bi, hi, qi: (bi, qi, hi)),
        compiler_params=pltpu.CompilerParams(
            dimension_semantics=("parallel", "parallel", "arbitrary"),
            vmem_limit_bytes=V7X_VMEM_LIMIT),
        name="mla_attention",
    )(q, k, v)

    m = bsz * seq
    out = pl.pallas_call(
        _out_ln_kernel,
        out_shape=jax.ShapeDtypeStruct((m, d), F32),
        grid=(m // OUT_TILE,),
        in_specs=[
            pl.BlockSpec((OUT_TILE, d), lambda i: (i, 0)),
            pl.BlockSpec((OUT_TILE, heads * vd), lambda i: (i, 0)),
            _resident((heads * vd, d)),
            _resident((1, d)),
            _resident((1, d)),
        ],
        out_specs=pl.BlockSpec((OUT_TILE, d), lambda i: (i, 0)),
        compiler_params=pltpu.CompilerParams(
            dimension_semantics=("parallel",), vmem_limit_bytes=V7X_VMEM_LIMIT),
        name="mla_out_ln",
    )(h.reshape(m, d), attn.reshape(m, heads * vd), bf(w_o), g.reshape(1, d), b.reshape(1, d))
    return out.reshape(bsz, seq, d)


def kernel(x, positions, ret_w_in, ret_gn_g, ret_w_o, lru_w_in, lru_conv_w, lru_conv_b, lru_w_a,
           lru_b_a, lru_w_x, lru_b_x, lru_lam, lru_w_o, mla_w_in, mla_q_norm, mla_kv_norm,
           mla_w_uq, mla_w_ukv, mla_w_o, ln_g, ln_b, mlp_w1, mlp_w2):
    bsz, seq, d = x.shape
    pos = positions.astype(F32).reshape(bsz, seq, 1)
    h = x
    for i in range(DEPTH):
        kind, j = i % N_MIXERS, i // N_MIXERS
        if kind == 0:
            h = _retention_layer(h, pos, ret_w_in[j], ret_gn_g[j], ret_w_o[j],
                                 ln_g[i, 0], ln_b[i, 0])
        elif kind == 1:
            h = _rglru_layer(h, lru_w_in[j], lru_conv_w[j], lru_conv_b[j], lru_w_a[j],
                             lru_b_a[j], lru_w_x[j], lru_b_x[j], lru_lam[j], lru_w_o[j],
                             ln_g[i, 0], ln_b[i, 0])
        else:
            h = _mla_layer(h, pos, mla_w_in[j], mla_q_norm[j], mla_kv_norm[j], mla_w_uq[j],
                           mla_w_ukv[j], mla_w_o[j], ln_g[i, 0], ln_b[i, 0])
        h = _mlp_layer(h.reshape(bsz * seq, d), mlp_w1[i], mlp_w2[i],
                       ln_g[i, 1], ln_b[i, 1]).reshape(bsz, seq, d)
    return h
```

```python
import functools
import math

import jax
import jax.numpy as jnp
from jax import lax
from jax.experimental import pallas as pl
from jax.experimental.pallas import tpu as pltpu

F32 = jnp.float32
BF16 = jnp.bfloat16

DEPTH = 4
N_MIXERS = 3
RET_HEADS = 4
GN_EPS = 1e-5
LRU_BLOCKS = 10
CONV_WIDTH = 4
LRU_C = 8.0
MLA_HEADS = 8
MLA_NOPE = 128
MLA_ROPE = 64
MLA_V = 128
MLA_Q_RANK = 384
MLA_KV_RANK = 256
ROPE_BASE = 10000.0
LN_EPS = 1e-5
RMS_EPS = 1e-6
ALPHA = (2.0 * DEPTH) ** 0.25
LOG2_E = math.log2(math.e)

RET_TILE = 256
LRU_TILE = 256
MLP_TILE = 512
MLA_PROJ_TILE = 256
ATTN_TQ = 1024
ATTN_TK = 256
ATTN_SLOTS = 2
ATTN_VT_PAD = 16
OUT_TILE = 512
FF_CHUNK = 1024

V7X_VMEM_LIMIT = 56 * 1024 * 1024


def _resident(shape):
    nd = len(shape)
    return pl.BlockSpec(shape, lambda *_: (0,) * nd, pipeline_mode=pl.Buffered(1))


def _dot(a, b):
    return jnp.dot(a, b, preferred_element_type=F32)


def _dot_nt(a, b):
    return lax.dot_general(a, b, (((1,), (1,)), ((), ())), preferred_element_type=F32)


def _layer_norm(z, g, b):
    mu = jnp.mean(z, axis=-1, keepdims=True)
    zc = z - mu
    var = jnp.mean(zc * zc, axis=-1, keepdims=True)
    return zc * lax.rsqrt(var + LN_EPS) * g + b


def _rms_norm(z, g):
    return z * lax.rsqrt(jnp.mean(z * z, axis=-1, keepdims=True) + RMS_EPS) * g


def _sigmoid(z):
    return jax.nn.sigmoid(z)


def _mlp_kernel(h_ref, w1_ref, w2_ref, g_ref, b_ref, o_ref):
    x = h_ref[...]
    xb = x.astype(BF16)
    acc = ALPHA * x
    d_ff = w1_ref.shape[1]
    for c in range(d_ff // FF_CHUNK):
        a = _dot(xb, w1_ref[:, c * FF_CHUNK:(c + 1) * FF_CHUNK])
        a = jnp.square(jnp.maximum(a, 0.0)).astype(BF16)
        acc = acc + _dot(a, w2_ref[c * FF_CHUNK:(c + 1) * FF_CHUNK, :])
    o_ref[...] = _layer_norm(acc, g_ref[...], b_ref[...])


def _mlp_layer(h, w1, w2, g, b):
    m, d = h.shape
    d_ff = w1.shape[1]
    return pl.pallas_call(
        _mlp_kernel,
        out_shape=jax.ShapeDtypeStruct((m, d), F32),
        grid=(m // MLP_TILE,),
        in_specs=[
            pl.BlockSpec((MLP_TILE, d), lambda i: (i, 0)),
            _resident((d, d_ff)),
            _resident((d_ff, d)),
            _resident((1, d)),
            _resident((1, d)),
        ],
        out_specs=pl.BlockSpec((MLP_TILE, d), lambda i: (i, 0)),
        compiler_params=pltpu.CompilerParams(
            dimension_semantics=("parallel",), vmem_limit_bytes=V7X_VMEM_LIMIT),
        name="mlp_ln",
    )(h, w1.astype(BF16), w2.astype(BF16), g.reshape(1, d), b.reshape(1, d))


def _ret_kernel(lg_ref, x_ref, pos_ref, inv_ref, w_in_ref, gn_ref, w_o_ref, g_ref, b_ref,
                o_ref, state_ref):
    c_idx = pl.program_id(1)
    heads, dk, dv = state_ref.shape
    half = dk // 2
    tile = x_ref.shape[0]

    @pl.when(c_idx == 0)
    def _():
        state_ref[...] = jnp.zeros_like(state_ref)

    x = x_ref[...]
    xb = x.astype(BF16)
    ang = pos_ref[...] * inv_ref[...]
    cos = jnp.cos(ang)
    sin = jnp.sin(ang)

    def rope(t):
        t1, t2 = t[:, :half], t[:, half:]
        return jnp.concatenate([t1 * cos - t2 * sin, t1 * sin + t2 * cos], axis=-1)

    row = lax.broadcasted_iota(jnp.int32, (tile, tile), 0)
    col = lax.broadcasted_iota(jnp.int32, (tile, tile), 1)
    diff = (row - col).astype(F32)
    causal = row >= col
    idx = lax.broadcasted_iota(jnp.int32, (tile, 1), 0).astype(F32)

    k_off = heads * dk
    v_off = 2 * heads * dk
    g_off = v_off + heads * dv
    mix = ALPHA * x
    for h in range(heads):
        lg = lg_ref[h]
        q = rope(_dot(xb, w_in_ref[:, h * dk:(h + 1) * dk]))
        k = rope(_dot(xb, w_in_ref[:, k_off + h * dk:k_off + (h + 1) * dk])) * (dk ** -0.5)
        v = _dot(xb, w_in_ref[:, v_off + h * dv:v_off + (h + 1) * dv])
        gate = _dot(xb, w_in_ref[:, g_off + h * dv:g_off + (h + 1) * dv])
        vb = v.astype(BF16)

        decay = jnp.where(causal, jnp.exp(lg * jnp.maximum(diff, 0.0)), 0.0)
        s = _dot_nt(q.astype(BF16), k.astype(BF16)) * decay
        inner = _dot(s.astype(BF16), vb)

        st = state_ref[h]
        q_dec = q * jnp.exp(lg * (idx + 1.0))
        cross = _dot(q_dec.astype(BF16), st.astype(BF16))
        k_dec = k * jnp.exp(lg * (tile - 1.0 - idx))
        state_ref[h] = st * jnp.exp(lg * tile) + _dot(k_dec.T.astype(BF16), vb)

        o = inner + cross
        mu = jnp.mean(o, axis=-1, keepdims=True)
        oc = o - mu
        var = jnp.mean(oc * oc, axis=-1, keepdims=True)
        y = oc * lax.rsqrt(var + GN_EPS) * gn_ref[:, h * dv:(h + 1) * dv]
        y = gate * _sigmoid(gate) * y
        mix = mix + _dot(y.astype(BF16), w_o_ref[h * dv:(h + 1) * dv, :])

    o_ref[...] = _layer_norm(mix, g_ref[...], b_ref[...])


def _retention_layer(h, pos, w_in, gn_g, w_o, g, b):
    bsz, seq, d = h.shape
    heads = RET_HEADS
    dk = d // heads
    dv = 2 * dk
    n_in = w_in.shape[1]
    log_gamma = jnp.log1p(-jnp.exp2(-5.0 - jnp.arange(heads, dtype=F32)))
    half = dk // 2
    inv = (ROPE_BASE ** (-jnp.arange(half, dtype=F32) / half)).reshape(1, half)
    tile = RET_TILE
    grid_spec = pltpu.PrefetchScalarGridSpec(
        num_scalar_prefetch=1,
        grid=(bsz, seq // tile),
        in_specs=[
            pl.BlockSpec((None, tile, d), lambda bi, ci, lg: (bi, ci, 0)),
            pl.BlockSpec((None, tile, 1), lambda bi, ci, lg: (bi, ci, 0)),
            _resident((1, half)),
            _resident((d, n_in)),
            _resident((1, heads * dv)),
            _resident((heads * dv, d)),
            _resident((1, d)),
            _resident((1, d)),
        ],
        out_specs=pl.BlockSpec((None, tile, d), lambda bi, ci, lg: (bi, ci, 0)),
        scratch_shapes=[pltpu.VMEM((heads, dk, dv), F32)],
    )
    return pl.pallas_call(
        _ret_kernel,
        out_shape=jax.ShapeDtypeStruct((bsz, seq, d), F32),
        grid_spec=grid_spec,
        compiler_params=pltpu.CompilerParams(
            dimension_semantics=("parallel", "arbitrary"), vmem_limit_bytes=V7X_VMEM_LIMIT),
        name="retention_layer",
    )(log_gamma, h, pos, inv, w_in.astype(BF16), gn_g.reshape(1, -1), w_o.astype(BF16),
      g.reshape(1, d), b.reshape(1, d))


def _gelu_tanh(z):
    c = math.sqrt(2.0 / math.pi)
    return 0.5 * z * (1.0 + jnp.tanh(c * (z + 0.044715 * (z * z * z))))


def _softplus(z):
    return jnp.maximum(z, 0.0) + jnp.log1p(jnp.exp(-jnp.abs(z)))


def _lru_kernel(x_ref, w_in_ref, conv_w_ref, conv_b_ref, w_a_ref, b_a_ref, w_x_ref, b_x_ref,
                lam_ref, w_o_ref, g_ref, b_ref, o_ref, tail_ref, h_ref):
    t_idx = pl.program_id(1)
    tile = x_ref.shape[0]
    width = conv_b_ref.shape[1]
    nblk, bdim, _ = w_a_ref.shape

    @pl.when(t_idx == 0)
    def _():
        tail_ref[...] = jnp.zeros_like(tail_ref)
        h_ref[...] = jnp.zeros_like(h_ref)

    x = x_ref[...]
    xb = x.astype(BF16)
    gate = _gelu_tanh(_dot(xb, w_in_ref[:, :width]))
    rec = _dot(xb, w_in_ref[:, width:])

    tail = tail_ref[...]
    row8 = lax.broadcasted_iota(jnp.int32, (8, width), 0)
    u = rec * conv_w_ref[CONV_WIDTH - 1:CONV_WIDTH, :] + conv_b_ref[...]
    for s in range(1, CONV_WIDTH):
        shifted = pltpu.roll(rec, s, 0)
        head = jnp.where(row8 < s, pltpu.roll(tail, s, 0), shifted[:8])
        shifted = jnp.concatenate([head, shifted[8:]], axis=0)
        u = u + shifted * conv_w_ref[CONV_WIDTH - 1 - s:CONV_WIDTH - s, :]
    tail_ref[...] = rec[tile - 8:, :]

    pre_a = []
    pre_x = []
    for n in range(nblk):
        ub = u[:, n * bdim:(n + 1) * bdim].astype(BF16)
        pre_a.append(_dot(ub, w_a_ref[n]))
        pre_x.append(_dot(ub, w_x_ref[n]))
    r = _sigmoid(jnp.concatenate(pre_a, axis=-1) + b_a_ref[...])
    i_gate = _sigmoid(jnp.concatenate(pre_x, axis=-1) + b_x_ref[...])
    log_a = (-LRU_C) * r * _softplus(-lam_ref[...])
    a = jnp.exp(log_a)
    b_in = jnp.sqrt(1.0 - jnp.exp(2.0 * log_a)) * (i_gate * u)

    rows = lax.broadcasted_iota(jnp.int32, (tile, width), 0)
    step = 1
    while step < tile:
        a_sh = pltpu.roll(a, step, 0)
        b_sh = pltpu.roll(b_in, step, 0)
        live = rows >= step
        b_in = jnp.where(live, a * b_sh + b_in, b_in)
        a = jnp.where(live, a * a_sh, a)
        step *= 2
    hs = a * h_ref[...] + b_in
    h_ref[...] = hs[tile - 1:tile, :]

    y = (hs * gate).astype(BF16)
    mix = ALPHA * x + _dot(y, w_o_ref[...])
    o_ref[...] = _layer_norm(mix, g_ref[...], b_ref[...])


def _rglru_layer(h, w_in, conv_w, conv_b, w_a, b_a, w_x, b_x, lam, w_o, g, b):
    bsz, seq, d = h.shape
    width = conv_b.shape[0]
    tile = LRU_TILE
    row = lambda t: t.reshape(1, -1)
    return pl.pallas_call(
        _lru_kernel,
        out_shape=jax.ShapeDtypeStruct((bsz, seq, d), F32),
        grid=(bsz, seq // tile),
        in_specs=[
            pl.BlockSpec((None, tile, d), lambda bi, ti: (bi, ti, 0)),
            _resident((d, 2 * width)),
            _resident((CONV_WIDTH, width)),
            _resident((1, width)),
            _resident(w_a.shape),
            _resident((1, width)),
            _resident(w_x.shape),
            _resident((1, width)),
            _resident((1, width)),
            _resident((width, d)),
            _resident((1, d)),
            _resident((1, d)),
        ],
        out_specs=pl.BlockSpec((None, tile, d), lambda bi, ti: (bi, ti, 0)),
        scratch_shapes=[pltpu.VMEM((8, width), F32), pltpu.VMEM((1, width), F32)],
        compiler_params=pltpu.CompilerParams(
            dimension_semantics=("parallel", "arbitrary"), vmem_limit_bytes=V7X_VMEM_LIMIT),
        name="rglru_layer",
    )(h, w_in.astype(BF16), conv_w, row(conv_b), w_a.astype(BF16), row(b_a),
      w_x.astype(BF16), row(b_x), row(lam), w_o.astype(BF16), row(g), row(b))


def _mla_proj_kernel(x_ref, pos_ref, inv_ref, w_q_ref, w_kv_ref, w_pe_ref, w_pe_rot_ref,
                     qn_g_ref, kvn_g_ref, w_qn_ref, w_qp_ref, w_qr_ref, w_kn_ref, w_v_ref,
                     q_ref, k_ref, v_ref):
    heads = w_qp_ref.shape[0]
    nope = MLA_NOPE
    scale = (MLA_NOPE + MLA_ROPE) ** -0.5 * LOG2_E
    xb = x_ref[...].astype(BF16)
    ang = pos_ref[...] * inv_ref[...]
    cos = jnp.cos(ang)
    sin = jnp.sin(ang)

    c_q = _rms_norm(_dot(xb, w_q_ref[...]), qn_g_ref[...]).astype(BF16)
    c_kv = _rms_norm(_dot(xb, w_kv_ref[...]), kvn_g_ref[...]).astype(BF16)
    k_pe = (_dot(xb, w_pe_ref[...]) * cos + _dot(xb, w_pe_rot_ref[...]) * sin).astype(BF16)

    q_nope = _dot(c_q, w_qn_ref[...]) * scale
    k_nope = _dot(c_kv, w_kn_ref[...])
    v_all = _dot(c_kv, w_v_ref[...])
    pad_rows = v_ref.shape[1] - MLA_V
    ones_row = jnp.where(lax.broadcasted_iota(jnp.int32, (pad_rows, v_ref.shape[2]), 0) == 0,
                         1.0, 0.0).astype(BF16)
    for h in range(heads):
        q_pe = (_dot(c_q, w_qp_ref[h]) * cos + _dot(c_q, w_qr_ref[h]) * sin) * scale
        q_ref[h, :, :nope] = q_nope[:, h * nope:(h + 1) * nope].astype(BF16)
        q_ref[h, :, nope:] = q_pe.astype(BF16)
        k_ref[h, :, :nope] = k_nope[:, h * nope:(h + 1) * nope].astype(BF16)
        k_ref[h, :, nope:] = k_pe
        v_ref[h, :MLA_V] = v_all[:, h * MLA_V:(h + 1) * MLA_V].T.astype(BF16)
        v_ref[h, MLA_V:] = ones_row


def _attn_kernel(q_ref, k_ref, vt_ref, o_ref, s_ref):
    i = pl.program_id(2)
    tq, dv = o_ref.shape
    tk = vt_ref.shape[2]
    sub = tk
    nsub = tq // sub
    kpq = tq // tk
    qs = [q_ref[s * sub:(s + 1) * sub, :] for s in range(nsub)]

    def scores(j, slot):
        kb = k_ref[pl.ds(pl.multiple_of(j * tk, tk), tk), :]
        maxima = []
        for s in range(nsub):
            st = _dot_nt(kb, qs[s])
            s_ref[slot, s] = st
            maxima.append(jnp.max(st, axis=0, keepdims=True))
        return tuple(maxima)

    def absorb(j, slot, stats, maxima):
        vt = vt_ref[j]
        ps, scaled = [], []
        for s, ((m_prev, acc), m_blk) in enumerate(zip(stats, maxima)):
            m_new = jnp.maximum(m_prev, m_blk)
            ps.append(jnp.exp2(s_ref[slot, s] - m_new).astype(BF16))
            scaled.append((m_new, jnp.exp2(m_prev - m_new) * acc))
        return tuple((m, a + _dot(vt, p)) for (m, a), p in zip(scaled, ps))

    stats = []
    for s in range(nsub):
        nkeys = (s + 1) * sub
        kd = k_ref[pl.ds(pl.multiple_of(i * tq, tq), nkeys), :]
        key = lax.broadcasted_iota(jnp.int32, (nkeys, sub), 0)
        qry = lax.broadcasted_iota(jnp.int32, (nkeys, sub), 1) + s * sub
        st = jnp.where(key <= qry, _dot_nt(kd, qs[s]), -jnp.inf)
        m0 = jnp.max(st, axis=0, keepdims=True)
        p = jnp.exp2(st - m0).astype(BF16)
        acc = _dot(vt_ref[i * kpq], p[:tk])
        for c in range(1, s + 1):
            acc = acc + _dot(vt_ref[i * kpq + c], p[c * tk:(c + 1) * tk])
        stats.append((m0, acc))

    nslot = s_ref.shape[0]

    def body(t, carry):
        stats, maxima = carry
        for c in range(nslot):
            nxt = jnp.minimum(t * nslot + c + 1, i * kpq - 1)
            maxima_next = scores(nxt, (c + 1) % nslot)
            stats = absorb(t * nslot + c, c, stats, maxima)
            maxima = maxima_next
        return stats, maxima

    stats, _ = lax.fori_loop(0, i * (kpq // nslot), body, (tuple(stats), scores(0, 0)))
    for s in range(nsub):
        _, acc = stats[s]
        o_ref[s * sub:(s + 1) * sub, :] = (acc[:dv] / acc[dv:dv + 1]).T.astype(o_ref.dtype)


def _out_ln_kernel(h_ref, y_ref, w_o_ref, g_ref, b_ref, o_ref):
    mix = ALPHA * h_ref[...] + _dot(y_ref[...], w_o_ref[...])
    o_ref[...] = _layer_norm(mix, g_ref[...], b_ref[...])


def _rot_half_cols(w):
    half = w.shape[-1] // 2
    return jnp.concatenate([-w[..., half:], w[..., :half]], axis=-1)


def _mla_layer(h, pos, w_in, q_norm_g, kv_norm_g, w_uq, w_ukv, w_o, g, b):
    bsz, seq, d = h.shape
    heads, nope, rope_d, vd = MLA_HEADS, MLA_NOPE, MLA_ROPE, MLA_V
    dqk = nope + rope_d
    half = rope_d // 2
    inv = ROPE_BASE ** (-jnp.arange(half, dtype=F32) / half)
    inv = jnp.concatenate([inv, inv]).reshape(1, rope_d)

    w_q = w_in[:, :MLA_Q_RANK]
    w_kv = w_in[:, MLA_Q_RANK:MLA_Q_RANK + MLA_KV_RANK]
    w_pe = w_in[:, MLA_Q_RANK + MLA_KV_RANK:]
    w_uq3 = w_uq.reshape(MLA_Q_RANK, heads, dqk)
    w_qn = w_uq3[:, :, :nope].reshape(MLA_Q_RANK, heads * nope)
    w_qp = w_uq3[:, :, nope:].transpose(1, 0, 2)
    w_ukv3 = w_ukv.reshape(MLA_KV_RANK, heads, nope + vd)
    w_kn = w_ukv3[:, :, :nope].reshape(MLA_KV_RANK, heads * nope)
    w_v = w_ukv3[:, :, nope:].reshape(MLA_KV_RANK, heads * vd)

    tile = MLA_PROJ_TILE
    per_tk = ATTN_TK // tile
    bf = lambda t: t.astype(BF16)
    q, k, v = pl.pallas_call(
        _mla_proj_kernel,
        out_shape=(jax.ShapeDtypeStruct((bsz, heads, seq, dqk), BF16),
                   jax.ShapeDtypeStruct((bsz, heads, seq, dqk), BF16),
                   jax.ShapeDtypeStruct((bsz, heads, seq // ATTN_TK, vd + ATTN_VT_PAD, ATTN_TK), BF16)),
        grid=(bsz, seq // tile),
        in_specs=[
            pl.BlockSpec((None, tile, d), lambda bi, ti: (bi, ti, 0)),
            pl.BlockSpec((None, tile, 1), lambda bi, ti: (bi, ti, 0)),
            _resident((1, rope_d)),
            _resident(w_q.shape), _resident(w_kv.shape), _resident(w_pe.shape),
            _resident(w_pe.shape),
            _resident((1, MLA_Q_RANK)), _resident((1, MLA_KV_RANK)),
            _resident(w_qn.shape), _resident(w_qp.shape), _resident(w_qp.shape),
            _resident(w_kn.shape), _resident(w_v.shape),
        ],
        out_specs=(pl.BlockSpec((None, heads, tile, dqk), lambda bi, ti: (bi, 0, ti, 0)),
                   pl.BlockSpec((None, heads, tile, dqk), lambda bi, ti: (bi, 0, ti, 0)),
                   pl.BlockSpec((None, heads, None, vd + ATTN_VT_PAD, tile),
                                lambda bi, ti: (bi, 0, ti // per_tk, 0, ti % per_tk))),
        compiler_params=pltpu.CompilerParams(
            dimension_semantics=("parallel", "parallel"), vmem_limit_bytes=V7X_VMEM_LIMIT),
        name="mla_proj",
    )(h, pos, inv, bf(w_q), bf(w_kv), bf(w_pe), bf(_rot_half_cols(w_pe)),
      q_norm_g.reshape(1, -1), kv_norm_g.reshape(1, -1),
      bf(w_qn), bf(w_qp), bf(_rot_half_cols(w_qp)), bf(w_kn), bf(w_v))

    attn = pl.pallas_call(
        _attn_kernel,
        out_shape=jax.ShapeDtypeStruct((bsz, seq, heads * vd), BF16),
        grid=(bsz, heads, seq // ATTN_TQ),
        in_specs=[
            pl.BlockSpec((None, None, ATTN_TQ, dqk), lambda bi, hi, qi: (bi, hi, qi, 0)),
            pl.BlockSpec((None, None, seq, dqk), lambda bi, hi, qi: (bi, hi, 0, 0)),
            pl.BlockSpec((None, None, seq // ATTN_TK, vd + ATTN_VT_PAD, ATTN_TK),
                         lambda bi, hi, qi: (bi, hi, 0, 0, 0)),
        ],
        out_specs=pl.BlockSpec((None, ATTN_TQ, vd), lambda bi, hi, qi: (bi, qi, hi)),
        scratch_shapes=[pltpu.VMEM((ATTN_SLOTS, ATTN_TQ // ATTN_TK, ATTN_TK, ATTN_TK),
                                   F32)],
        compiler_params=pltpu.CompilerParams(
            dimension_semantics=("parallel", "parallel", "arbitrary"),
            vmem_limit_bytes=V7X_VMEM_LIMIT),
        name="mla_attention",
    )(q, k, v)

    m = bsz * seq
    out = pl.pallas_call(
        _out_ln_kernel,
        out_shape=jax.ShapeDtypeStruct((m, d), F32),
        grid=(m // OUT_TILE,),
        in_specs=[
            pl.BlockSpec((OUT_TILE, d), lambda i: (i, 0)),
            pl.BlockSpec((OUT_TILE, heads * vd), lambda i: (i, 0)),
            _resident((heads * vd, d)),
            _resident((1, d)),
            _resident((1, d)),
        ],
        out_specs=pl.BlockSpec((OUT_TILE, d), lambda i: (i, 0)),
        compiler_params=pltpu.CompilerParams(
            dimension_semantics=("parallel",), vmem_limit_bytes=V7X_VMEM_LIMIT),
        name="mla_out_ln",
    )(h.reshape(m, d), attn.reshape(m, heads * vd), bf(w_o), g.reshape(1, d), b.reshape(1, d))
    return out.reshape(bsz, seq, d)


def kernel(x, positions, ret_w_in, ret_gn_g, ret_w_o, lru_w_in, lru_conv_w, lru_conv_b, lru_w_a,
           lru_b_a, lru_w_x, lru_b_x, lru_lam, lru_w_o, mla_w_in, mla_q_norm, mla_kv_norm,
           mla_w_uq, mla_w_ukv, mla_w_o, ln_g, ln_b, mlp_w1, mlp_w2):
    bsz, seq, d = x.shape
    pos = positions.astype(F32).reshape(bsz, seq, 1)
    h = x
    for i in range(DEPTH):
        kind, j = i % N_MIXERS, i // N_MIXERS
        if kind == 0:
            h = _retention_layer(h, pos, ret_w_in[j], ret_gn_g[j], ret_w_o[j],
                                 ln_g[i, 0], ln_b[i, 0])
        elif kind == 1:
            h = _rglru_layer(h, lru_w_in[j], lru_conv_w[j], lru_conv_b[j], lru_w_a[j],
                             lru_b_a[j], lru_w_x[j], lru_b_x[j], lru_lam[j], lru_w_o[j],
                             ln_g[i, 0], ln_b[i, 0])
        else:
            h = _mla_layer(h, pos, mla_w_in[j], mla_q_norm[j], mla_kv_norm[j], mla_w_uq[j],
                           mla_w_ukv[j], mla_w_o[j], ln_g[i, 0], ln_b[i, 0])
        h = _mlp_layer(h.reshape(bsz * seq, d), mlp_w1[i], mlp_w2[i],
                       ln_g[i, 1], ln_b[i, 1]).reshape(bsz, seq, d)
    return h
```

```python
import functools
import math

import jax
import jax.numpy as jnp
from jax import lax
from jax.experimental import pallas as pl
from jax.experimental.pallas import tpu as pltpu

F32 = jnp.float32
BF16 = jnp.bfloat16

DEPTH = 4
N_MIXERS = 3
RET_HEADS = 4
GN_EPS = 1e-5
LRU_BLOCKS = 10
CONV_WIDTH = 4
LRU_C = 8.0
MLA_HEADS = 8
MLA_NOPE = 128
MLA_ROPE = 64
MLA_V = 128
MLA_Q_RANK = 384
MLA_KV_RANK = 256
ROPE_BASE = 10000.0
LN_EPS = 1e-5
RMS_EPS = 1e-6
ALPHA = (2.0 * DEPTH) ** 0.25
LOG2_E = math.log2(math.e)

RET_TILE = 256
LRU_TILE = 256
MLP_TILE = 512
MLA_PROJ_TILE = 256
ATTN_TQ = 1024
ATTN_TK = 256
ATTN_SLOTS = 4
ATTN_VT_PAD = 16
OUT_TILE = 512
FF_CHUNK = 1024

V7X_VMEM_LIMIT = 56 * 1024 * 1024


def _resident(shape):
    nd = len(shape)
    return pl.BlockSpec(shape, lambda *_: (0,) * nd, pipeline_mode=pl.Buffered(1))


def _dot(a, b):
    return jnp.dot(a, b, preferred_element_type=F32)


def _dot_nt(a, b):
    return lax.dot_general(a, b, (((1,), (1,)), ((), ())), preferred_element_type=F32)


def _layer_norm(z, g, b):
    mu = jnp.mean(z, axis=-1, keepdims=True)
    zc = z - mu
    var = jnp.mean(zc * zc, axis=-1, keepdims=True)
    return zc * lax.rsqrt(var + LN_EPS) * g + b


def _rms_norm(z, g):
    return z * lax.rsqrt(jnp.mean(z * z, axis=-1, keepdims=True) + RMS_EPS) * g


def _sigmoid(z):
    return jax.nn.sigmoid(z)


def _mlp_kernel(h_ref, w1_ref, w2_ref, g_ref, b_ref, o_ref):
    x = h_ref[...]
    xb = x.astype(BF16)
    acc = ALPHA * x
    d_ff = w1_ref.shape[1]
    for c in range(d_ff // FF_CHUNK):
        a = _dot(xb, w1_ref[:, c * FF_CHUNK:(c + 1) * FF_CHUNK])
        a = jnp.square(jnp.maximum(a, 0.0)).astype(BF16)
        acc = acc + _dot(a, w2_ref[c * FF_CHUNK:(c + 1) * FF_CHUNK, :])
    o_ref[...] = _layer_norm(acc, g_ref[...], b_ref[...])


def _mlp_layer(h, w1, w2, g, b):
    m, d = h.shape
    d_ff = w1.shape[1]
    return pl.pallas_call(
        _mlp_kernel,
        out_shape=jax.ShapeDtypeStruct((m, d), F32),
        grid=(m // MLP_TILE,),
        in_specs=[
            pl.BlockSpec((MLP_TILE, d), lambda i: (i, 0)),
            _resident((d, d_ff)),
            _resident((d_ff, d)),
            _resident((1, d)),
            _resident((1, d)),
        ],
        out_specs=pl.BlockSpec((MLP_TILE, d), lambda i: (i, 0)),
        compiler_params=pltpu.CompilerParams(
            dimension_semantics=("parallel",), vmem_limit_bytes=V7X_VMEM_LIMIT),
        name="mlp_ln",
    )(h, w1.astype(BF16), w2.astype(BF16), g.reshape(1, d), b.reshape(1, d))


def _ret_kernel(lg_ref, x_ref, pos_ref, inv_ref, w_in_ref, gn_ref, w_o_ref, g_ref, b_ref,
                o_ref, state_ref):
    c_idx = pl.program_id(1)
    heads, dk, dv = state_ref.shape
    half = dk // 2
    tile = x_ref.shape[0]

    @pl.when(c_idx == 0)
    def _():
        state_ref[...] = jnp.zeros_like(state_ref)

    x = x_ref[...]
    xb = x.astype(BF16)
    ang = pos_ref[...] * inv_ref[...]
    cos = jnp.cos(ang)
    sin = jnp.sin(ang)

    def rope(t):
        t1, t2 = t[:, :half], t[:, half:]
        return jnp.concatenate([t1 * cos - t2 * sin, t1 * sin + t2 * cos], axis=-1)

    row = lax.broadcasted_iota(jnp.int32, (tile, tile), 0)
    col = lax.broadcasted_iota(jnp.int32, (tile, tile), 1)
    diff = (row - col).astype(F32)
    causal = row >= col
    idx = lax.broadcasted_iota(jnp.int32, (tile, 1), 0).astype(F32)

    k_off = heads * dk
    v_off = 2 * heads * dk
    g_off = v_off + heads * dv
    hs = range(heads)
    lgs = [lg_ref[h] for h in hs]
    q_raw = [_dot(xb, w_in_ref[:, h * dk:(h + 1) * dk]) for h in hs]
    k_raw = [_dot(xb, w_in_ref[:, k_off + h * dk:k_off + (h + 1) * dk]) for h in hs]
    vb = [_dot(xb, w_in_ref[:, v_off + h * dv:v_off + (h + 1) * dv]).astype(BF16) for h in hs]
    gates = [_dot(xb, w_in_ref[:, g_off + h * dv:g_off + (h + 1) * dv]) for h in hs]

    q = [rope(t) for t in q_raw]
    k = [rope(t) * (dk ** -0.5) for t in k_raw]
    scores = [_dot_nt(q[h].astype(BF16), k[h].astype(BF16)) for h in hs]

    old = [state_ref[h] for h in hs]
    cross = [_dot((q[h] * jnp.exp(lgs[h] * (idx + 1.0))).astype(BF16), old[h].astype(BF16))
             for h in hs]
    for h in hs:
        k_dec = k[h] * jnp.exp(lgs[h] * (tile - 1.0 - idx))
        state_ref[h] = old[h] * jnp.exp(lgs[h] * tile) + _dot(k_dec.T.astype(BF16), vb[h])

    inner = []
    for h in hs:
        decay = jnp.where(causal, jnp.exp(lgs[h] * jnp.maximum(diff, 0.0)), 0.0)
        inner.append(_dot((scores[h] * decay).astype(BF16), vb[h]))

    mix = ALPHA * x
    for h in hs:
        o = inner[h] + cross[h]
        mu = jnp.mean(o, axis=-1, keepdims=True)
        oc = o - mu
        var = jnp.mean(oc * oc, axis=-1, keepdims=True)
        y = oc * lax.rsqrt(var + GN_EPS) * gn_ref[:, h * dv:(h + 1) * dv]
        y = gates[h] * _sigmoid(gates[h]) * y
        mix = mix + _dot(y.astype(BF16), w_o_ref[h * dv:(h + 1) * dv, :])

    o_ref[...] = _layer_norm(mix, g_ref[...], b_ref[...])


def _retention_layer(h, pos, w_in, gn_g, w_o, g, b):
    bsz, seq, d = h.shape
    heads = RET_HEADS
    dk = d // heads
    dv = 2 * dk
    n_in = w_in.shape[1]
    log_gamma = jnp.log1p(-jnp.exp2(-5.0 - jnp.arange(heads, dtype=F32)))
    half = dk // 2
    inv = (ROPE_BASE ** (-jnp.arange(half, dtype=F32) / half)).reshape(1, half)
    tile = RET_TILE
    grid_spec = pltpu.PrefetchScalarGridSpec(
        num_scalar_prefetch=1,
        grid=(bsz, seq // tile),
        in_specs=[
            pl.BlockSpec((None, tile, d), lambda bi, ci, lg: (bi, ci, 0)),
            pl.BlockSpec((None, tile, 1), lambda bi, ci, lg: (bi, ci, 0)),
            _resident((1, half)),
            _resident((d, n_in)),
            _resident((1, heads * dv)),
            _resident((heads * dv, d)),
            _resident((1, d)),
            _resident((1, d)),
        ],
        out_specs=pl.BlockSpec((None, tile, d), lambda bi, ci, lg: (bi, ci, 0)),
        scratch_shapes=[pltpu.VMEM((heads, dk, dv), F32)],
    )
    return pl.pallas_call(
        _ret_kernel,
        out_shape=jax.ShapeDtypeStruct((bsz, seq, d), F32),
        grid_spec=grid_spec,
        compiler_params=pltpu.CompilerParams(
            dimension_semantics=("parallel", "arbitrary"), vmem_limit_bytes=V7X_VMEM_LIMIT),
        name="retention_layer",
    )(log_gamma, h, pos, inv, w_in.astype(BF16), gn_g.reshape(1, -1), w_o.astype(BF16),
      g.reshape(1, d), b.reshape(1, d))


def _gelu_tanh(z):
    c = math.sqrt(2.0 / math.pi)
    return z * _sigmoid((2.0 * c) * (z + 0.044715 * (z * z * z)))


def _softplus(z):
    return jnp.maximum(z, 0.0) + jnp.log1p(jnp.exp(-jnp.abs(z)))


def _lru_kernel(x_ref, w_in_ref, conv_w_ref, conv_b_ref, w_a_ref, b_a_ref, w_x_ref, b_x_ref,
                lam_ref, w_o_ref, g_ref, b_ref, o_ref, tail_ref, h_ref):
    t_idx = pl.program_id(1)
    tile = x_ref.shape[0]
    width = conv_b_ref.shape[1]
    nblk, bdim, _ = w_a_ref.shape

    @pl.when(t_idx == 0)
    def _():
        tail_ref[...] = jnp.zeros_like(tail_ref)
        h_ref[...] = jnp.zeros_like(h_ref)

    x = x_ref[...]
    xb = x.astype(BF16)
    gate = _gelu_tanh(_dot(xb, w_in_ref[:, :width]))
    rec = _dot(xb, w_in_ref[:, width:])

    tail = tail_ref[...]
    row8 = lax.broadcasted_iota(jnp.int32, (8, width), 0)
    u = rec * conv_w_ref[CONV_WIDTH - 1:CONV_WIDTH, :] + conv_b_ref[...]
    for s in range(1, CONV_WIDTH):
        shifted = pltpu.roll(rec, s, 0)
        head = jnp.where(row8 < s, pltpu.roll(tail, s, 0), shifted[:8])
        shifted = jnp.concatenate([head, shifted[8:]], axis=0)
        u = u + shifted * conv_w_ref[CONV_WIDTH - 1 - s:CONV_WIDTH - s, :]
    tail_ref[...] = rec[tile - 8:, :]

    pre_a = []
    pre_x = []
    for n in range(nblk):
        ub = u[:, n * bdim:(n + 1) * bdim].astype(BF16)
        pre_a.append(_dot(ub, w_a_ref[n]))
        pre_x.append(_dot(ub, w_x_ref[n]))
    r = _sigmoid(jnp.concatenate(pre_a, axis=-1) + b_a_ref[...])
    i_gate = _sigmoid(jnp.concatenate(pre_x, axis=-1) + b_x_ref[...])
    log_a = (-LRU_C) * r * _softplus(-lam_ref[...])
    a = jnp.exp(log_a)
    b_in = jnp.sqrt(1.0 - jnp.exp(2.0 * log_a)) * (i_gate * u)

    slab = 8
    a3 = a.reshape(tile // slab, slab, width)
    b3 = b_in.reshape(tile // slab, slab, width)
    rows = lax.broadcasted_iota(jnp.int32, a3.shape, 1)
    step = 1
    while step < slab:
        a_sh = pltpu.roll(a3, step, 1)
        b_sh = pltpu.roll(b3, step, 1)
        live = rows >= step
        b3 = jnp.where(live, a3 * b_sh + b3, b3)
        a3 = jnp.where(live, a3 * a_sh, a3)
        step *= 2
    carry = h_ref[...]
    slabs = []
    for i in range(tile // slab):
        hs_i = a3[i] * carry + b3[i]
        carry = hs_i[slab - 1:slab, :]
        slabs.append(hs_i)
    hs = jnp.concatenate(slabs, axis=0)
    h_ref[...] = carry

    y = (hs * gate).astype(BF16)
    mix = ALPHA * x + _dot(y, w_o_ref[...])
    o_ref[...] = _layer_norm(mix, g_ref[...], b_ref[...])


def _rglru_layer(h, w_in, conv_w, conv_b, w_a, b_a, w_x, b_x, lam, w_o, g, b):
    bsz, seq, d = h.shape
    width = conv_b.shape[0]
    tile = LRU_TILE
    row = lambda t: t.reshape(1, -1)
    return pl.pallas_call(
        _lru_kernel,
        out_shape=jax.ShapeDtypeStruct((bsz, seq, d), F32),
        grid=(bsz, seq // tile),
        in_specs=[
            pl.BlockSpec((None, tile, d), lambda bi, ti: (bi, ti, 0)),
            _resident((d, 2 * width)),
            _resident((CONV_WIDTH, width)),
            _resident((1, width)),
            _resident(w_a.shape),
            _resident((1, width)),
            _resident(w_x.shape),
            _resident((1, width)),
            _resident((1, width)),
            _resident((width, d)),
            _resident((1, d)),
            _resident((1, d)),
        ],
        out_specs=pl.BlockSpec((None, tile, d), lambda bi, ti: (bi, ti, 0)),
        scratch_shapes=[pltpu.VMEM((8, width), F32), pltpu.VMEM((1, width), F32)],
        compiler_params=pltpu.CompilerParams(
            dimension_semantics=("parallel", "arbitrary"), vmem_limit_bytes=V7X_VMEM_LIMIT),
        name="rglru_layer",
    )(h, w_in.astype(BF16), conv_w, row(conv_b), w_a.astype(BF16), row(b_a),
      w_x.astype(BF16), row(b_x), row(lam), w_o.astype(BF16), row(g), row(b))


def _mla_proj_kernel(x_ref, pos_ref, inv_ref, w_q_ref, w_kv_ref, w_pe_ref, w_pe_rot_ref,
                     qn_g_ref, kvn_g_ref, w_qn_ref, w_qp_ref, w_qr_ref, w_kn_ref, w_v_ref,
                     q_ref, k_ref, v_ref):
    heads = w_qp_ref.shape[0]
    nope = MLA_NOPE
    scale = (MLA_NOPE + MLA_ROPE) ** -0.5 * LOG2_E
    xb = x_ref[...].astype(BF16)
    ang = pos_ref[...] * inv_ref[...]
    cos = jnp.cos(ang)
    sin = jnp.sin(ang)

    c_q = _rms_norm(_dot(xb, w_q_ref[...]), qn_g_ref[...]).astype(BF16)
    c_kv = _rms_norm(_dot(xb, w_kv_ref[...]), kvn_g_ref[...]).astype(BF16)
    k_pe = (_dot(xb, w_pe_ref[...]) * cos + _dot(xb, w_pe_rot_ref[...]) * sin).astype(BF16)

    q_nope = _dot(c_q, w_qn_ref[...]) * scale
    k_nope = _dot(c_kv, w_kn_ref[...])
    v_all = _dot(c_kv, w_v_ref[...])
    pad_rows = v_ref.shape[1] - MLA_V
    ones_row = jnp.where(lax.broadcasted_iota(jnp.int32, (pad_rows, v_ref.shape[2]), 0) == 0,
                         1.0, 0.0).astype(BF16)
    for h in range(heads):
        q_pe = (_dot(c_q, w_qp_ref[h]) * cos + _dot(c_q, w_qr_ref[h]) * sin) * scale
        q_ref[h, :, :nope] = q_nope[:, h * nope:(h + 1) * nope].astype(BF16)
        q_ref[h, :, nope:] = q_pe.astype(BF16)
        k_ref[h, :, :nope] = k_nope[:, h * nope:(h + 1) * nope].astype(BF16)
        k_ref[h, :, nope:] = k_pe
        v_ref[h, :MLA_V] = v_all[:, h * MLA_V:(h + 1) * MLA_V].T.astype(BF16)
        v_ref[h, MLA_V:] = ones_row


def _attn_kernel(q_ref, k_ref, vt_ref, o_ref, s_ref):
    i = pl.program_id(2)
    tq, dv = o_ref.shape
    tk = vt_ref.shape[2]
    sub = tk
    nsub = tq // sub
    kpq = tq // tk
    qs = [q_ref[s * sub:(s + 1) * sub, :] for s in range(nsub)]

    def scores(j, slot):
        kb = k_ref[pl.ds(pl.multiple_of(j * tk, tk), tk), :]
        maxima = []
        for s in range(nsub):
            st = _dot_nt(kb, qs[s])
            s_ref[slot, s] = st
            maxima.append(jnp.max(st, axis=0, keepdims=True))
        return tuple(maxima)

    def absorb(j, slot, stats, maxima):
        vt = vt_ref[j]
        ps, scaled = [], []
        for s, ((m_prev, acc), m_blk) in enumerate(zip(stats, maxima)):
            m_new = jnp.maximum(m_prev, m_blk)
            ps.append(jnp.exp2(s_ref[slot, s] - m_new).astype(BF16))
            scaled.append((m_new, jnp.exp2(m_prev - m_new) * acc))
        return tuple((m, a + _dot(vt, p)) for (m, a), p in zip(scaled, ps))

    tri = (lax.broadcasted_iota(jnp.int32, (tk, sub), 0)
           <= lax.broadcasted_iota(jnp.int32, (tk, sub), 1))
    diag_scores = {}
    for c in range(kpq):
        kb = k_ref[pl.ds(pl.multiple_of(i * tq + c * tk, tk), tk), :]
        for s in range(c, nsub):
            st = _dot_nt(kb, qs[s])
            diag_scores[c, s] = jnp.where(tri, st, -jnp.inf) if c == s else st
    diag_p = {}
    diag_m = []
    for s in range(nsub):
        m0 = jnp.max(diag_scores[0, s], axis=0, keepdims=True)
        for c in range(1, s + 1):
            m0 = jnp.maximum(m0, jnp.max(diag_scores[c, s], axis=0, keepdims=True))
        for c in range(s + 1):
            diag_p[c, s] = jnp.exp2(diag_scores[c, s] - m0).astype(BF16)
        diag_m.append(m0)
    stats = []
    for s in range(nsub):
        acc = _dot(vt_ref[i * kpq], diag_p[0, s])
        for c in range(1, s + 1):
            acc = acc + _dot(vt_ref[i * kpq + c], diag_p[c, s])
        stats.append((diag_m[s], acc))

    nslot = s_ref.shape[0]

    def body(t, carry):
        stats, maxima = carry
        for c in range(nslot):
            nxt = jnp.minimum(t * nslot + c + 1, i * kpq - 1)
            maxima_next = scores(nxt, (c + 1) % nslot)
            stats = absorb(t * nslot + c, c, stats, maxima)
            maxima = maxima_next
        return stats, maxima

    stats, _ = lax.fori_loop(0, i * (kpq // nslot), body, (tuple(stats), scores(0, 0)))
    for s in range(nsub):
        _, acc = stats[s]
        o_ref[s * sub:(s + 1) * sub, :] = (acc[:dv] / acc[dv:dv + 1]).T.astype(o_ref.dtype)


def _out_ln_kernel(h_ref, y_ref, w_o_ref, g_ref, b_ref, o_ref):
    mix = ALPHA * h_ref[...] + _dot(y_ref[...], w_o_ref[...])
    o_ref[...] = _layer_norm(mix, g_ref[...], b_ref[...])


def _rot_half_cols(w):
    half = w.shape[-1] // 2
    return jnp.concatenate([-w[..., half:], w[..., :half]], axis=-1)


def _mla_layer(h, pos, w_in, q_norm_g, kv_norm_g, w_uq, w_ukv, w_o, g, b):
    bsz, seq, d = h.shape
    heads, nope, rope_d, vd = MLA_HEADS, MLA_NOPE, MLA_ROPE, MLA_V
    dqk = nope + rope_d
    half = rope_d // 2
    inv = ROPE_BASE ** (-jnp.arange(half, dtype=F32) / half)
    inv = jnp.concatenate([inv, inv]).reshape(1, rope_d)

    w_q = w_in[:, :MLA_Q_RANK]
    w_kv = w_in[:, MLA_Q_RANK:MLA_Q_RANK + MLA_KV_RANK]
    w_pe = w_in[:, MLA_Q_RANK + MLA_KV_RANK:]
    w_uq3 = w_uq.reshape(MLA_Q_RANK, heads, dqk)
    w_qn = w_uq3[:, :, :nope].reshape(MLA_Q_RANK, heads * nope)
    w_qp = w_uq3[:, :, nope:].transpose(1, 0, 2)
    w_ukv3 = w_ukv.reshape(MLA_KV_RANK, heads, nope + vd)
    w_kn = w_ukv3[:, :, :nope].reshape(MLA_KV_RANK, heads * nope)
    w_v = w_ukv3[:, :, nope:].reshape(MLA_KV_RANK, heads * vd)

    tile = MLA_PROJ_TILE
    per_tk = ATTN_TK // tile
    bf = lambda t: t.astype(BF16)
    q, k, v = pl.pallas_call(
        _mla_proj_kernel,
        out_shape=(jax.ShapeDtypeStruct((bsz, heads, seq, dqk), BF16),
                   jax.ShapeDtypeStruct((bsz, heads, seq, dqk), BF16),
                   jax.ShapeDtypeStruct((bsz, heads, seq // ATTN_TK, vd + ATTN_VT_PAD, ATTN_TK), BF16)),
        grid=(bsz, seq // tile),
        in_specs=[
            pl.BlockSpec((None, tile, d), lambda bi, ti: (bi, ti, 0)),
            pl.BlockSpec((None, tile, 1), lambda bi, ti: (bi, ti, 0)),
            _resident((1, rope_d)),
            _resident(w_q.shape), _resident(w_kv.shape), _resident(w_pe.shape),
            _resident(w_pe.shape),
            _resident((1, MLA_Q_RANK)), _resident((1, MLA_KV_RANK)),
            _resident(w_qn.shape), _resident(w_qp.shape), _resident(w_qp.shape),
            _resident(w_kn.shape), _resident(w_v.shape),
        ],
        out_specs=(pl.BlockSpec((None, heads, tile, dqk), lambda bi, ti: (bi, 0, ti, 0)),
                   pl.BlockSpec((None, heads, tile, dqk), lambda bi, ti: (bi, 0, ti, 0)),
                   pl.BlockSpec((None, heads, None, vd + ATTN_VT_PAD, tile),
                                lambda bi, ti: (bi, 0, ti // per_tk, 0, ti % per_tk))),
        compiler_params=pltpu.CompilerParams(
            dimension_semantics=("parallel", "parallel"), vmem_limit_bytes=V7X_VMEM_LIMIT),
        name="mla_proj",
    )(h, pos, inv, bf(w_q), bf(w_kv), bf(w_pe), bf(_rot_half_cols(w_pe)),
      q_norm_g.reshape(1, -1), kv_norm_g.reshape(1, -1),
      bf(w_qn), bf(w_qp), bf(_rot_half_cols(w_qp)), bf(w_kn), bf(w_v))

    attn = pl.pallas_call(
        _attn_kernel,
        out_shape=jax.ShapeDtypeStruct((bsz, seq, heads * vd), BF16),
        grid=(bsz, heads, seq // ATTN_TQ),
        in_specs=[
            pl.BlockSpec((None, None, ATTN_TQ, dqk), lambda bi, hi, qi: (bi, hi, qi, 0)),
            pl.BlockSpec((None, None, seq, dqk), lambda bi, hi, qi: (bi, hi, 0, 0)),
            pl.BlockSpec((None, None, seq // ATTN_TK, vd + ATTN_VT_PAD, ATTN_TK),
                         lambda bi, hi, qi: (bi, hi, 0, 0, 0)),
        ],
        out_specs=pl.BlockSpec((None, ATTN_TQ, vd), lambda bi, hi, qi: (bi, qi, hi)),
        scratch_shapes=[pltpu.VMEM((ATTN_SLOTS, ATTN_TQ // ATTN_TK, ATTN_TK, ATTN_TK),
                                   F32)],
        compiler_params=pltpu.CompilerParams(
            dimension_semantics=("parallel", "parallel", "arbitrary"),
            vmem_limit_bytes=V7X_VMEM_LIMIT),
        name="mla_attention",
    )(q, k, v)

    m = bsz * seq
    out = pl.pallas_call(
        _out_ln_kernel,
        out_shape=jax.ShapeDtypeStruct((m, d), F32),
        grid=(m // OUT_TILE,),
        in_specs=[
            pl.BlockSpec((OUT_TILE, d), lambda i: (i, 0)),
            pl.BlockSpec((OUT_TILE, heads * vd), lambda i: (i, 0)),
            _resident((heads * vd, d)),
            _resident((1, d)),
            _resident((1, d)),
        ],
        out_specs=pl.BlockSpec((OUT_TILE, d), lambda i: (i, 0)),
        compiler_params=pltpu.CompilerParams(
            dimension_semantics=("parallel",), vmem_limit_bytes=V7X_VMEM_LIMIT),
        name="mla_out_ln",
    )(h.reshape(m, d), attn.reshape(m, heads * vd), bf(w_o), g.reshape(1, d), b.reshape(1, d))
    return out.reshape(bsz, seq, d)


def kernel(x, positions, ret_w_in, ret_gn_g, ret_w_o, lru_w_in, lru_conv_w, lru_conv_b, lru_w_a,
           lru_b_a, lru_w_x, lru_b_x, lru_lam, lru_w_o, mla_w_in, mla_q_norm, mla_kv_norm,
           mla_w_uq, mla_w_ukv, mla_w_o, ln_g, ln_b, mlp_w1, mlp_w2):
    bsz, seq, d = x.shape
    pos = positions.astype(F32).reshape(bsz, seq, 1)
    h = x
    for i in range(DEPTH):
        kind, j = i % N_MIXERS, i // N_MIXERS
        if kind == 0:
            h = _retention_layer(h, pos, ret_w_in[j], ret_gn_g[j], ret_w_o[j],
                                 ln_g[i, 0], ln_b[i, 0])
        elif kind == 1:
            h = _rglru_layer(h, lru_w_in[j], lru_conv_w[j], lru_conv_b[j], lru_w_a[j],
                             lru_b_a[j], lru_w_x[j], lru_b_x[j], lru_lam[j], lru_w_o[j],
                             ln_g[i, 0], ln_b[i, 0])
        else:
            h = _mla_layer(h, pos, mla_w_in[j], mla_q_norm[j], mla_kv_norm[j], mla_w_uq[j],
                           mla_w_ukv[j], mla_w_o[j], ln_g[i, 0], ln_b[i, 0])
        h = _mlp_layer(h.reshape(bsz * seq, d), mlp_w1[i], mlp_w2[i],
                       ln_g[i, 1], ln_b[i, 1]).reshape(bsz, seq, d)
    return h
```

```python
import functools
import math

import jax
import jax.numpy as jnp
from jax import lax
from jax.experimental import pallas as pl
from jax.experimental.pallas import tpu as pltpu

F32 = jnp.float32
BF16 = jnp.bfloat16

DEPTH = 4
N_MIXERS = 3
RET_HEADS = 4
GN_EPS = 1e-5
LRU_BLOCKS = 10
CONV_WIDTH = 4
LRU_C = 8.0
MLA_HEADS = 8
MLA_NOPE = 128
MLA_ROPE = 64
MLA_V = 128
MLA_Q_RANK = 384
MLA_KV_RANK = 256
ROPE_BASE = 10000.0
LN_EPS = 1e-5
RMS_EPS = 1e-6
ALPHA = (2.0 * DEPTH) ** 0.25
LOG2_E = math.log2(math.e)

RET_CHUNK = 256
RET_TILE = 512
LRU_TILE = 256
MLP_TILE = 1024
MLP_SUB = 512
MLA_PROJ_TILE = 256
ATTN_TQ = 1024
ATTN_TK = 256
ATTN_SLOTS = 4
ATTN_VT_PAD = 16
OUT_TILE = 512
FF_CHUNK = 1024

V7X_VMEM_LIMIT = 56 * 1024 * 1024


def _resident(shape):
    nd = len(shape)
    return pl.BlockSpec(shape, lambda *_: (0,) * nd, pipeline_mode=pl.Buffered(1))


def _dot(a, b):
    return jnp.dot(a, b, preferred_element_type=F32)


def _dot_nt(a, b):
    return lax.dot_general(a, b, (((1,), (1,)), ((), ())), preferred_element_type=F32)


def _layer_norm(z, g, b):
    mu = jnp.mean(z, axis=-1, keepdims=True)
    zc = z - mu
    var = jnp.mean(zc * zc, axis=-1, keepdims=True)
    return zc * lax.rsqrt(var + LN_EPS) * g + b


def _rms_norm(z, g):
    return z * lax.rsqrt(jnp.mean(z * z, axis=-1, keepdims=True) + RMS_EPS) * g


def _sigmoid(z):
    return jax.nn.sigmoid(z)


def _mlp_kernel(h_ref, w1_ref, w2_ref, g_ref, b_ref, o_ref):
    d_ff = w1_ref.shape[1]
    parts = range(h_ref.shape[0] // MLP_SUB)
    accs = []
    for p in parts:
        x = h_ref[p * MLP_SUB:(p + 1) * MLP_SUB, :]
        xb = x.astype(BF16)
        acc = ALPHA * x
        for c in range(d_ff // FF_CHUNK):
            a = _dot(xb, w1_ref[:, c * FF_CHUNK:(c + 1) * FF_CHUNK])
            a = jnp.square(jnp.maximum(a, 0.0)).astype(BF16)
            acc = acc + _dot(a, w2_ref[c * FF_CHUNK:(c + 1) * FF_CHUNK, :])
        accs.append(acc)
    for p in parts:
        o_ref[p * MLP_SUB:(p + 1) * MLP_SUB, :] = _layer_norm(accs[p], g_ref[...], b_ref[...])


def _mlp_layer(h, w1, w2, g, b):
    m, d = h.shape
    d_ff = w1.shape[1]
    return pl.pallas_call(
        _mlp_kernel,
        out_shape=jax.ShapeDtypeStruct((m, d), F32),
        grid=(m // MLP_TILE,),
        in_specs=[
            pl.BlockSpec((MLP_TILE, d), lambda i: (i, 0)),
            _resident((d, d_ff)),
            _resident((d_ff, d)),
            _resident((1, d)),
            _resident((1, d)),
        ],
        out_specs=pl.BlockSpec((MLP_TILE, d), lambda i: (i, 0)),
        compiler_params=pltpu.CompilerParams(
            dimension_semantics=("parallel",), vmem_limit_bytes=V7X_VMEM_LIMIT),
        name="mlp_ln",
    )(h, w1.astype(BF16), w2.astype(BF16), g.reshape(1, d), b.reshape(1, d))


def _ret_kernel(lg_ref, x_ref, pos_ref, inv_ref, w_in_ref, gn_ref, w_o_ref, g_ref, b_ref,
                o_ref, state_ref):
    heads, dk, dv = state_ref.shape
    half = dk // 2
    chunk = RET_CHUNK
    cs = range(x_ref.shape[0] // chunk)
    hs = range(heads)

    @pl.when(pl.program_id(1) == 0)
    def _():
        state_ref[...] = jnp.zeros_like(state_ref)

    def rows(ref, c):
        return ref[c * chunk:(c + 1) * chunk, :]

    k_off = heads * dk
    v_off = 2 * heads * dk
    g_off = v_off + heads * dv
    xb = [rows(x_ref, c).astype(BF16) for c in cs]
    q_raw = [[_dot(xb[c], w_in_ref[:, h * dk:(h + 1) * dk]) for h in hs] for c in cs]
    k_raw = [[_dot(xb[c], w_in_ref[:, k_off + h * dk:k_off + (h + 1) * dk]) for h in hs]
             for c in cs]
    vb = [[_dot(xb[c], w_in_ref[:, v_off + h * dv:v_off + (h + 1) * dv]).astype(BF16)
           for h in hs] for c in cs]
    gates = [[_dot(xb[c], w_in_ref[:, g_off + h * dv:g_off + (h + 1) * dv]) for h in hs]
             for c in cs]

    lgs = [lg_ref[h] for h in hs]
    row = lax.broadcasted_iota(jnp.int32, (chunk, chunk), 0)
    col = lax.broadcasted_iota(jnp.int32, (chunk, chunk), 1)
    diff = jnp.maximum(row - col, 0).astype(F32)
    idx = lax.broadcasted_iota(jnp.int32, (chunk, 1), 0).astype(F32)
    intra = [jnp.where(row >= col, jnp.exp(lgs[h] * diff), 0.0) for h in hs]
    q_dec = [jnp.exp(lgs[h] * (idx + 1.0)) for h in hs]
    k_dec = [jnp.exp(lgs[h] * (chunk - 1.0 - idx)) for h in hs]
    chunk_dec = [jnp.exp(lgs[h] * chunk) for h in hs]

    state = [state_ref[h] for h in hs]
    outs = []
    for c in cs:
        ang = rows(pos_ref, c) * inv_ref[...]
        cos = jnp.cos(ang)
        sin = jnp.sin(ang)

        def rope(t):
            t1, t2 = t[:, :half], t[:, half:]
            return jnp.concatenate([t1 * cos - t2 * sin, t1 * sin + t2 * cos], axis=-1)

        q = [rope(t) for t in q_raw[c]]
        k = [rope(t) * (dk ** -0.5) for t in k_raw[c]]
        scores = [_dot_nt(q[h].astype(BF16), k[h].astype(BF16)) for h in hs]
        old = [state[h].astype(BF16) for h in hs]
        state = [state[h] * chunk_dec[h] + _dot((k[h] * k_dec[h]).T.astype(BF16), vb[c][h])
                 for h in hs]
        outs.append([
            _dot(jnp.concatenate([(scores[h] * intra[h]).astype(BF16),
                                  (q[h] * q_dec[h]).astype(BF16)], axis=1),
                 jnp.concatenate([vb[c][h], old[h]], axis=0))
            for h in hs])
    for h in hs:
        state_ref[h] = state[h]

    for c in cs:
        mix = ALPHA * rows(x_ref, c)
        for h in hs:
            o = outs[c][h]
            mu = jnp.mean(o, axis=-1, keepdims=True)
            oc = o - mu
            var = jnp.mean(oc * oc, axis=-1, keepdims=True)
            y = oc * lax.rsqrt(var + GN_EPS) * gn_ref[:, h * dv:(h + 1) * dv]
            y = gates[c][h] * _sigmoid(gates[c][h]) * y
            mix = mix + _dot(y.astype(BF16), w_o_ref[h * dv:(h + 1) * dv, :])
        o_ref[c * chunk:(c + 1) * chunk, :] = _layer_norm(mix, g_ref[...], b_ref[...])


def _retention_layer(h, pos, w_in, gn_g, w_o, g, b):
    bsz, seq, d = h.shape
    heads = RET_HEADS
    dk = d // heads
    dv = 2 * dk
    n_in = w_in.shape[1]
    log_gamma = jnp.log1p(-jnp.exp2(-5.0 - jnp.arange(heads, dtype=F32)))
    half = dk // 2
    inv = (ROPE_BASE ** (-jnp.arange(half, dtype=F32) / half)).reshape(1, half)
    tile = RET_TILE
    grid_spec = pltpu.PrefetchScalarGridSpec(
        num_scalar_prefetch=1,
        grid=(bsz, seq // tile),
        in_specs=[
            pl.BlockSpec((None, tile, d), lambda bi, ci, lg: (bi, ci, 0)),
            pl.BlockSpec((None, tile, 1), lambda bi, ci, lg: (bi, ci, 0)),
            _resident((1, half)),
            _resident((d, n_in)),
            _resident((1, heads * dv)),
            _resident((heads * dv, d)),
            _resident((1, d)),
            _resident((1, d)),
        ],
        out_specs=pl.BlockSpec((None, tile, d), lambda bi, ci, lg: (bi, ci, 0)),
        scratch_shapes=[pltpu.VMEM((heads, dk, dv), F32)],
    )
    return pl.pallas_call(
        _ret_kernel,
        out_shape=jax.ShapeDtypeStruct((bsz, seq, d), F32),
        grid_spec=grid_spec,
        compiler_params=pltpu.CompilerParams(
            dimension_semantics=("parallel", "arbitrary"), vmem_limit_bytes=V7X_VMEM_LIMIT),
        name="retention_layer",
    )(log_gamma, h, pos, inv, w_in.astype(BF16), gn_g.reshape(1, -1), w_o.astype(BF16),
      g.reshape(1, d), b.reshape(1, d))


def _gelu_tanh(z):
    c = math.sqrt(2.0 / math.pi)
    return z * _sigmoid((2.0 * c) * (z + 0.044715 * (z * z * z)))


def _softplus(z):
    return jnp.maximum(z, 0.0) + jnp.log1p(jnp.exp(-jnp.abs(z)))


def _lru_kernel(x_ref, w_in_ref, conv_w_ref, conv_b_ref, w_a_ref, b_a_ref, w_x_ref, b_x_ref,
                lam_ref, w_o_ref, g_ref, b_ref, o_ref, tail_ref, h_ref):
    t_idx = pl.program_id(1)
    tile = x_ref.shape[0]
    width = conv_b_ref.shape[1]
    nblk, bdim, _ = w_a_ref.shape

    @pl.when(t_idx == 0)
    def _():
        tail_ref[...] = jnp.zeros_like(tail_ref)
        h_ref[...] = jnp.zeros_like(h_ref)

    x = x_ref[...]
    xb = x.astype(BF16)
    gate = _gelu_tanh(_dot(xb, w_in_ref[:, :width]))
    rec = _dot(xb, w_in_ref[:, width:])

    tail = tail_ref[...]
    row8 = lax.broadcasted_iota(jnp.int32, (8, width), 0)
    u = rec * conv_w_ref[CONV_WIDTH - 1:CONV_WIDTH, :] + conv_b_ref[...]
    for s in range(1, CONV_WIDTH):
        shifted = pltpu.roll(rec, s, 0)
        head = jnp.where(row8 < s, pltpu.roll(tail, s, 0), shifted[:8])
        shifted = jnp.concatenate([head, shifted[8:]], axis=0)
        u = u + shifted * conv_w_ref[CONV_WIDTH - 1 - s:CONV_WIDTH - s, :]
    tail_ref[...] = rec[tile - 8:, :]

    pre_a = []
    pre_x = []
    for n in range(nblk):
        ub = u[:, n * bdim:(n + 1) * bdim].astype(BF16)
        pre_a.append(_dot(ub, w_a_ref[n]))
        pre_x.append(_dot(ub, w_x_ref[n]))
    r = _sigmoid(jnp.concatenate(pre_a, axis=-1) + b_a_ref[...])
    i_gate = _sigmoid(jnp.concatenate(pre_x, axis=-1) + b_x_ref[...])
    log_a = (-LRU_C) * r * _softplus(-lam_ref[...])
    a = jnp.exp(log_a)
    b_in = jnp.sqrt(1.0 - jnp.exp(2.0 * log_a)) * (i_gate * u)

    slab = 8
    a3 = a.reshape(tile // slab, slab, width)
    b3 = b_in.reshape(tile // slab, slab, width)
    rows = lax.broadcasted_iota(jnp.int32, a3.shape, 1)
    step = 1
    while step < slab:
        a_sh = pltpu.roll(a3, step, 1)
        b_sh = pltpu.roll(b3, step, 1)
        live = rows >= step
        b3 = jnp.where(live, a3 * b_sh + b3, b3)
        a3 = jnp.where(live, a3 * a_sh, a3)
        step *= 2
    carry = h_ref[...]
    slabs = []
    for i in range(tile // slab):
        hs_i = a3[i] * carry + b3[i]
        carry = hs_i[slab - 1:slab, :]
        slabs.append(hs_i)
    hs = jnp.concatenate(slabs, axis=0)
    h_ref[...] = carry

    y = (hs * gate).astype(BF16)
    mix = ALPHA * x + _dot(y, w_o_ref[...])
    o_ref[...] = _layer_norm(mix, g_ref[...], b_ref[...])


def _rglru_layer(h, w_in, conv_w, conv_b, w_a, b_a, w_x, b_x, lam, w_o, g, b):
    bsz, seq, d = h.shape
    width = conv_b.shape[0]
    tile = LRU_TILE
    row = lambda t: t.reshape(1, -1)
    return pl.pallas_call(
        _lru_kernel,
        out_shape=jax.ShapeDtypeStruct((bsz, seq, d), F32),
        grid=(bsz, seq // tile),
        in_specs=[
            pl.BlockSpec((None, tile, d), lambda bi, ti: (bi, ti, 0)),
            _resident((d, 2 * width)),
            _resident((CONV_WIDTH, width)),
            _resident((1, width)),
            _resident(w_a.shape),
            _resident((1, width)),
            _resident(w_x.shape),
            _resident((1, width)),
            _resident((1, width)),
            _resident((width, d)),
            _resident((1, d)),
            _resident((1, d)),
        ],
        out_specs=pl.BlockSpec((None, tile, d), lambda bi, ti: (bi, ti, 0)),
        scratch_shapes=[pltpu.VMEM((8, width), F32), pltpu.VMEM((1, width), F32)],
        compiler_params=pltpu.CompilerParams(
            dimension_semantics=("parallel", "arbitrary"), vmem_limit_bytes=V7X_VMEM_LIMIT),
        name="rglru_layer",
    )(h, w_in.astype(BF16), conv_w, row(conv_b), w_a.astype(BF16), row(b_a),
      w_x.astype(BF16), row(b_x), row(lam), w_o.astype(BF16), row(g), row(b))


def _mla_proj_kernel(x_ref, pos_ref, inv_ref, w_down_ref, w_kpe_ref, qn_g_ref, kvn_g_ref,
                     w_uq_ref, w_ukv_ref, q_ref, k_ref, v_ref):
    heads = q_ref.shape[0]
    nope, lanes = MLA_NOPE, 2 * MLA_ROPE
    q_rank = qn_g_ref.shape[1]
    scale = (MLA_NOPE + MLA_ROPE) ** -0.5 * LOG2_E
    xb = x_ref[...].astype(BF16)
    ang = pos_ref[...] * inv_ref[...]
    cos = jnp.cos(ang)
    sin = jnp.sin(ang)

    down = _dot(xb, w_down_ref[...])
    kpe = _dot(xb, w_kpe_ref[...])
    c_q = _rms_norm(down[:, :q_rank], qn_g_ref[...]).astype(BF16)
    c_kv = _rms_norm(down[:, q_rank:], kvn_g_ref[...]).astype(BF16)
    uq = _dot(c_q, w_uq_ref[...])
    ukv = _dot(c_kv, w_ukv_ref[...])

    k_pe = [(kpe[:, par * lanes:(par + 1) * lanes] * cos
             + kpe[:, (2 + par) * lanes:(3 + par) * lanes] * sin).astype(BF16) for par in (0, 1)]
    pairs = heads // 2
    pe_off = heads * nope
    rot_off = pe_off + pairs * lanes
    q_pe = [((uq[:, pe_off + g * lanes:pe_off + (g + 1) * lanes] * cos
              + uq[:, rot_off + g * lanes:rot_off + (g + 1) * lanes] * sin) * scale).astype(BF16)
            for g in range(pairs)]
    pad_rows = v_ref.shape[1] - MLA_V
    ones_row = jnp.where(lax.broadcasted_iota(jnp.int32, (pad_rows, v_ref.shape[2]), 0) == 0,
                         1.0, 0.0).astype(BF16)
    v_off = heads * nope
    for h in range(heads):
        q_ref[h, :, :nope] = (uq[:, h * nope:(h + 1) * nope] * scale).astype(BF16)
        q_ref[h, :, nope:] = q_pe[h // 2]
        k_ref[h, :, :nope] = ukv[:, h * nope:(h + 1) * nope].astype(BF16)
        k_ref[h, :, nope:] = k_pe[h % 2]
        v_ref[h, :MLA_V] = ukv[:, v_off + h * MLA_V:v_off + (h + 1) * MLA_V].T.astype(BF16)
        v_ref[h, MLA_V:] = ones_row


def _attn_kernel(q_ref, k_ref, vt_ref, o_ref, s_ref):
    i = pl.program_id(2)
    tq, dv = o_ref.shape
    tk = vt_ref.shape[2]
    sub = tk
    nsub = tq // sub
    kpq = tq // tk
    qs = [q_ref[s * sub:(s + 1) * sub, :] for s in range(nsub)]

    def scores(j, slot):
        kb = k_ref[pl.ds(pl.multiple_of(j * tk, tk), tk), :]
        maxima = []
        for s in range(nsub):
            st = _dot_nt(kb, qs[s])
            s_ref[slot, s] = st
            maxima.append(jnp.max(st, axis=0, keepdims=True))
        return tuple(maxima)

    def absorb(j, slot, stats, maxima):
        vt = vt_ref[j]
        ps, scaled = [], []
        for s, ((m_prev, acc), m_blk) in enumerate(zip(stats, maxima)):
            m_new = jnp.maximum(m_prev, m_blk)
            ps.append(jnp.exp2(s_ref[slot, s] - m_new).astype(BF16))
            scaled.append((m_new, jnp.exp2(m_prev - m_new) * acc))
        return tuple((m, a + _dot(vt, p)) for (m, a), p in zip(scaled, ps))

    tri = (lax.broadcasted_iota(jnp.int32, (tk, sub), 0)
           <= lax.broadcasted_iota(jnp.int32, (tk, sub), 1))
    diag_scores = {}
    for c in range(kpq):
        kb = k_ref[pl.ds(pl.multiple_of(i * tq + c * tk, tk), tk), :]
        for s in range(c, nsub):
            st = _dot_nt(kb, qs[s])
            diag_scores[c, s] = jnp.where(tri, st, -jnp.inf) if c == s else st
    diag_p = {}
    diag_m = []
    for s in range(nsub):
        m0 = jnp.max(diag_scores[0, s], axis=0, keepdims=True)
        for c in range(1, s + 1):
            m0 = jnp.maximum(m0, jnp.max(diag_scores[c, s], axis=0, keepdims=True))
        for c in range(s + 1):
            diag_p[c, s] = jnp.exp2(diag_scores[c, s] - m0).astype(BF16)
        diag_m.append(m0)
    stats = []
    for s in range(nsub):
        acc = _dot(vt_ref[i * kpq], diag_p[0, s])
        for c in range(1, s + 1):
            acc = acc + _dot(vt_ref[i * kpq + c], diag_p[c, s])
        stats.append((diag_m[s], acc))

    nslot = s_ref.shape[0]

    def body(t, carry):
        stats, maxima = carry
        for c in range(nslot):
            nxt = jnp.minimum(t * nslot + c + 1, i * kpq - 1)
            maxima_next = scores(nxt, (c + 1) % nslot)
            stats = absorb(t * nslot + c, c, stats, maxima)
            maxima = maxima_next
        return stats, maxima

    stats, _ = lax.fori_loop(0, i * (kpq // nslot), body, (tuple(stats), scores(0, 0)))
    for s in range(nsub):
        _, acc = stats[s]
        o_ref[s * sub:(s + 1) * sub, :] = (acc[:dv] / acc[dv:dv + 1]).T.astype(o_ref.dtype)


def _out_ln_kernel(h_ref, y_ref, w_o_ref, g_ref, b_ref, o_ref):
    mix = ALPHA * h_ref[...] + _dot(y_ref[...], w_o_ref[...])
    o_ref[...] = _layer_norm(mix, g_ref[...], b_ref[...])


def _rot_half_cols(w):
    half = w.shape[-1] // 2
    return jnp.concatenate([-w[..., half:], w[..., :half]], axis=-1)


def _mla_layer(h, pos, w_in, q_norm_g, kv_norm_g, w_uq, w_ukv, w_o, g, b):
    bsz, seq, d = h.shape
    heads, nope, rope_d, vd = MLA_HEADS, MLA_NOPE, MLA_ROPE, MLA_V
    dqk = nope + 2 * rope_d
    half = rope_d // 2
    inv = ROPE_BASE ** (-jnp.arange(half, dtype=F32) / half)
    inv = jnp.tile(inv, 4).reshape(1, 2 * rope_d)

    w_down = w_in[:, :MLA_Q_RANK + MLA_KV_RANK]
    w_pe = w_in[:, MLA_Q_RANK + MLA_KV_RANK:]
    zero = jnp.zeros_like(w_pe)
    w_pe_rot = _rot_half_cols(w_pe)
    w_kpe = jnp.concatenate([w_pe, zero, zero, w_pe, w_pe_rot, zero, zero, w_pe_rot], axis=1)
    w_uq3 = w_uq.reshape(MLA_Q_RANK, heads, nope + rope_d)
    w_qp = w_uq3[:, :, nope:]
    w_uq_all = jnp.concatenate(
        [w_uq3[:, :, :nope].reshape(MLA_Q_RANK, heads * nope),
         w_qp.reshape(MLA_Q_RANK, heads * rope_d),
         _rot_half_cols(w_qp).reshape(MLA_Q_RANK, heads * rope_d)], axis=1)
    w_ukv3 = w_ukv.reshape(MLA_KV_RANK, heads, nope + vd)
    w_ukv_all = jnp.concatenate(
        [w_ukv3[:, :, :nope].reshape(MLA_KV_RANK, heads * nope),
         w_ukv3[:, :, nope:].reshape(MLA_KV_RANK, heads * vd)], axis=1)

    tile = MLA_PROJ_TILE
    per_tk = ATTN_TK // tile
    bf = lambda t: t.astype(BF16)
    q, k, v = pl.pallas_call(
        _mla_proj_kernel,
        out_shape=(jax.ShapeDtypeStruct((bsz, heads, seq, dqk), BF16),
                   jax.ShapeDtypeStruct((bsz, heads, seq, dqk), BF16),
                   jax.ShapeDtypeStruct((bsz, heads, seq // ATTN_TK, vd + ATTN_VT_PAD, ATTN_TK), BF16)),
        grid=(bsz, seq // tile),
        in_specs=[
            pl.BlockSpec((None, tile, d), lambda bi, ti: (bi, ti, 0)),
            pl.BlockSpec((None, tile, 1), lambda bi, ti: (bi, ti, 0)),
            _resident((1, 2 * rope_d)),
            _resident(w_down.shape), _resident(w_kpe.shape),
            _resident((1, MLA_Q_RANK)), _resident((1, MLA_KV_RANK)),
            _resident(w_uq_all.shape), _resident(w_ukv_all.shape),
        ],
        out_specs=(pl.BlockSpec((None, heads, tile, dqk), lambda bi, ti: (bi, 0, ti, 0)),
                   pl.BlockSpec((None, heads, tile, dqk), lambda bi, ti: (bi, 0, ti, 0)),
                   pl.BlockSpec((None, heads, None, vd + ATTN_VT_PAD, tile),
                                lambda bi, ti: (bi, 0, ti // per_tk, 0, ti % per_tk))),
        compiler_params=pltpu.CompilerParams(
            dimension_semantics=("parallel", "parallel"), vmem_limit_bytes=V7X_VMEM_LIMIT),
        name="mla_proj",
    )(h, pos, inv, bf(w_down), bf(w_kpe), q_norm_g.reshape(1, -1), kv_norm_g.reshape(1, -1),
      bf(w_uq_all), bf(w_ukv_all))

    attn = pl.pallas_call(
        _attn_kernel,
        out_shape=jax.ShapeDtypeStruct((bsz, seq, heads * vd), BF16),
        grid=(bsz, heads, seq // ATTN_TQ),
        in_specs=[
            pl.BlockSpec((None, None, ATTN_TQ, dqk), lambda bi, hi, qi: (bi, hi, qi, 0)),
            pl.BlockSpec((None, None, seq, dqk), lambda bi, hi, qi: (bi, hi, 0, 0)),
            pl.BlockSpec((None, None, seq // ATTN_TK, vd + ATTN_VT_PAD, ATTN_TK),
                         lambda bi, hi, qi: (bi, hi, 0, 0, 0)),
        ],
        out_specs=pl.BlockSpec((None, ATTN_TQ, vd), lambda bi, hi, qi: (bi, qi, hi)),
        scratch_shapes=[pltpu.VMEM((ATTN_SLOTS, ATTN_TQ // ATTN_TK, ATTN_TK, ATTN_TK),
                                   F32)],
        compiler_params=pltpu.CompilerParams(
            dimension_semantics=("parallel", "parallel", "arbitrary"),
            vmem_limit_bytes=V7X_VMEM_LIMIT),
        name="mla_attention",
    )(q, k, v)

    m = bsz * seq
    out = pl.pallas_call(
        _out_ln_kernel,
        out_shape=jax.ShapeDtypeStruct((m, d), F32),
        grid=(m // OUT_TILE,),
        in_specs=[
            pl.BlockSpec((OUT_TILE, d), lambda i: (i, 0)),
            pl.BlockSpec((OUT_TILE, heads * vd), lambda i: (i, 0)),
            _resident((heads * vd, d)),
            _resident((1, d)),
            _resident((1, d)),
        ],
        out_specs=pl.BlockSpec((OUT_TILE, d), lambda i: (i, 0)),
        compiler_params=pltpu.CompilerParams(
            dimension_semantics=("parallel",), vmem_limit_bytes=V7X_VMEM_LIMIT),
        name="mla_out_ln",
    )(h.reshape(m, d), attn.reshape(m, heads * vd), bf(w_o), g.reshape(1, d), b.reshape(1, d))
    return out.reshape(bsz, seq, d)


def kernel(x, positions, ret_w_in, ret_gn_g, ret_w_o, lru_w_in, lru_conv_w, lru_conv_b, lru_w_a,
           lru_b_a, lru_w_x, lru_b_x, lru_lam, lru_w_o, mla_w_in, mla_q_norm, mla_kv_norm,
           mla_w_uq, mla_w_ukv, mla_w_o, ln_g, ln_b, mlp_w1, mlp_w2):
    bsz, seq, d = x.shape
    pos = positions.astype(F32).reshape(bsz, seq, 1)
    h = x
    for i in range(DEPTH):
        kind, j = i % N_MIXERS, i // N_MIXERS
        if kind == 0:
            h = _retention_layer(h, pos, ret_w_in[j], ret_gn_g[j], ret_w_o[j],
                                 ln_g[i, 0], ln_b[i, 0])
        elif kind == 1:
            h = _rglru_layer(h, lru_w_in[j], lru_conv_w[j], lru_conv_b[j], lru_w_a[j],
                             lru_b_a[j], lru_w_x[j], lru_b_x[j], lru_lam[j], lru_w_o[j],
                             ln_g[i, 0], ln_b[i, 0])
        else:
            h = _mla_layer(h, pos, mla_w_in[j], mla_q_norm[j], mla_kv_norm[j], mla_w_uq[j],
                           mla_w_ukv[j], mla_w_o[j], ln_g[i, 0], ln_b[i, 0])
        h = _mlp_layer(h.reshape(bsz * seq, d), mlp_w1[i], mlp_w2[i],
                       ln_g[i, 1], ln_b[i, 1]).reshape(bsz, seq, d)
    return h
```

```python
import functools
import math

import jax
import jax.numpy as jnp
from jax import lax
from jax.experimental import pallas as pl
from jax.experimental.pallas import tpu as pltpu

F32 = jnp.float32
BF16 = jnp.bfloat16

DEPTH = 4
N_MIXERS = 3
RET_HEADS = 4
GN_EPS = 1e-5
LRU_BLOCKS = 10
CONV_WIDTH = 4
LRU_C = 8.0
MLA_HEADS = 8
MLA_NOPE = 128
MLA_ROPE = 64
MLA_V = 128
MLA_Q_RANK = 384
MLA_KV_RANK = 256
ROPE_BASE = 10000.0
LN_EPS = 1e-5
RMS_EPS = 1e-6
ALPHA = (2.0 * DEPTH) ** 0.25
LOG2_E = math.log2(math.e)

RET_CHUNK = 256
RET_TILE = 512
LRU_TILE = 512
LRU_SUB = 256
MLP_TILE = 1024
MLP_SUB = 512
MLA_PROJ_TILE = 256
ATTN_TQ = 1024
ATTN_TK = 256
ATTN_SLOTS = 4
ATTN_VT_PAD = 16
FF_CHUNK = 1024

V7X_VMEM_LIMIT = 56 * 1024 * 1024


def _resident(shape, layer=None):
    nd = len(shape)
    if layer is None:
        return pl.BlockSpec(shape, lambda *_: (0,) * nd, pipeline_mode=pl.Buffered(1))
    return pl.BlockSpec((None,) + tuple(shape), lambda *_: (layer,) + (0,) * nd,
                        pipeline_mode=pl.Buffered(1))


def _dot(a, b):
    return jnp.dot(a, b, preferred_element_type=F32)


def _dot_nt(a, b):
    return lax.dot_general(a, b, (((1,), (1,)), ((), ())), preferred_element_type=F32)


def _layer_norm(z, g, b):
    mu = jnp.mean(z, axis=-1, keepdims=True)
    zc = z - mu
    var = jnp.mean(zc * zc, axis=-1, keepdims=True)
    return zc * lax.rsqrt(var + LN_EPS) * g + b


def _rms_norm(z, g):
    return z * lax.rsqrt(jnp.mean(z * z, axis=-1, keepdims=True) + RMS_EPS) * g


def _sigmoid(z):
    return jax.nn.sigmoid(z)


def _mlp_kernel(*refs, pre_proj):
    if pre_proj:
        h_ref, y_ref, w_o_ref, g0_ref, b0_ref, w1_ref, w2_ref, g_ref, b_ref, o_ref = refs
    else:
        h_ref, w1_ref, w2_ref, g_ref, b_ref, o_ref = refs
    d_ff = w1_ref.shape[1]
    parts = range(h_ref.shape[0] // MLP_SUB)
    accs = []
    for p in parts:
        x = h_ref[p * MLP_SUB:(p + 1) * MLP_SUB, :]
        if pre_proj:
            mix = ALPHA * x + _dot(y_ref[p * MLP_SUB:(p + 1) * MLP_SUB, :], w_o_ref[...])
            x = _layer_norm(mix, g0_ref[...], b0_ref[...])
        xb = x.astype(BF16)
        acc = ALPHA * x
        for c in range(d_ff // FF_CHUNK):
            a = _dot(xb, w1_ref[:, c * FF_CHUNK:(c + 1) * FF_CHUNK])
            a = jnp.square(jnp.maximum(a, 0.0)).astype(BF16)
            acc = acc + _dot(a, w2_ref[c * FF_CHUNK:(c + 1) * FF_CHUNK, :])
        accs.append(acc)
    for p in parts:
        o_ref[p * MLP_SUB:(p + 1) * MLP_SUB, :] = _layer_norm(accs[p], g_ref[...], b_ref[...])


def _mlp_layer(h, w1_stack, w2_stack, layer, g, b, pre=None):
    m, d = h.shape
    d_ff = w1_stack.shape[2]
    tile = MLP_TILE if pre is None else MLP_SUB
    rows = lambda width: pl.BlockSpec((tile, width), lambda i: (i, 0))
    operands = [h]
    in_specs = [rows(d)]
    if pre is not None:
        y, w_o, g0, b0 = pre
        operands += [y, w_o, g0.reshape(1, d), b0.reshape(1, d)]
        in_specs += [rows(y.shape[1]), _resident(w_o.shape), _resident((1, d)), _resident((1, d))]
    operands += [w1_stack, w2_stack, g.reshape(1, d), b.reshape(1, d)]
    in_specs += [_resident((d, d_ff), layer), _resident((d_ff, d), layer),
                 _resident((1, d)), _resident((1, d))]
    return pl.pallas_call(
        functools.partial(_mlp_kernel, pre_proj=pre is not None),
        out_shape=jax.ShapeDtypeStruct((m, d), F32),
        grid=(m // tile,),
        in_specs=in_specs,
        out_specs=rows(d),
        compiler_params=pltpu.CompilerParams(
            dimension_semantics=("parallel",), vmem_limit_bytes=V7X_VMEM_LIMIT),
        name="mlp_ln",
    )(*operands)


def _ret_kernel(lg_ref, x_ref, pos_ref, inv_ref, w_in_ref, gn_ref, w_o_ref, g_ref, b_ref,
                o_ref, state_ref):
    heads, dk, dv = state_ref.shape
    half = dk // 2
    chunk = RET_CHUNK
    cs = range(x_ref.shape[0] // chunk)
    hs = range(heads)

    @pl.when(pl.program_id(1) == 0)
    def _():
        state_ref[...] = jnp.zeros_like(state_ref)

    def rows(ref, c):
        return ref[c * chunk:(c + 1) * chunk, :]

    k_off = heads * dk
    v_off = 2 * heads * dk
    g_off = v_off + heads * dv
    xb = [rows(x_ref, c).astype(BF16) for c in cs]
    q_raw = [[_dot(xb[c], w_in_ref[:, h * dk:(h + 1) * dk]) for h in hs] for c in cs]
    k_raw = [[_dot(xb[c], w_in_ref[:, k_off + h * dk:k_off + (h + 1) * dk]) for h in hs]
             for c in cs]
    vb = [[_dot(xb[c], w_in_ref[:, v_off + h * dv:v_off + (h + 1) * dv]).astype(BF16)
           for h in hs] for c in cs]
    gates = [[_dot(xb[c], w_in_ref[:, g_off + h * dv:g_off + (h + 1) * dv]) for h in hs]
             for c in cs]

    lgs = [lg_ref[h] for h in hs]
    row = lax.broadcasted_iota(jnp.int32, (chunk, chunk), 0)
    col = lax.broadcasted_iota(jnp.int32, (chunk, chunk), 1)
    diff = jnp.maximum(row - col, 0).astype(F32)
    idx = lax.broadcasted_iota(jnp.int32, (chunk, 1), 0).astype(F32)
    intra = [jnp.where(row >= col, jnp.exp(lgs[h] * diff), 0.0) for h in hs]
    q_dec = [jnp.exp(lgs[h] * (idx + 1.0)) for h in hs]
    k_dec = [jnp.exp(lgs[h] * (chunk - 1.0 - idx)) for h in hs]
    chunk_dec = [jnp.exp(lgs[h] * chunk) for h in hs]

    state = [state_ref[h] for h in hs]
    outs = []
    for c in cs:
        ang = rows(pos_ref, c) * inv_ref[...]
        cos = jnp.cos(ang)
        sin = jnp.sin(ang)

        def rope(t):
            t1, t2 = t[:, :half], t[:, half:]
            return jnp.concatenate([t1 * cos - t2 * sin, t1 * sin + t2 * cos], axis=-1)

        q = [rope(t) for t in q_raw[c]]
        k = [rope(t) * (dk ** -0.5) for t in k_raw[c]]
        scores = [_dot_nt(q[h].astype(BF16), k[h].astype(BF16)) for h in hs]
        old = [state[h].astype(BF16) for h in hs]
        state = [state[h] * chunk_dec[h] + _dot((k[h] * k_dec[h]).T.astype(BF16), vb[c][h])
                 for h in hs]
        outs.append([
            _dot(jnp.concatenate([(scores[h] * intra[h]).astype(BF16),
                                  (q[h] * q_dec[h]).astype(BF16)], axis=1),
                 jnp.concatenate([vb[c][h], old[h]], axis=0))
            for h in hs])
    for h in hs:
        state_ref[h] = state[h]

    for c in cs:
        mix = ALPHA * rows(x_ref, c)
        for h in hs:
            o = outs[c][h]
            mu = jnp.mean(o, axis=-1, keepdims=True)
            oc = o - mu
            var = jnp.mean(oc * oc, axis=-1, keepdims=True)
            y = oc * lax.rsqrt(var + GN_EPS) * gn_ref[:, h * dv:(h + 1) * dv]
            y = gates[c][h] * _sigmoid(gates[c][h]) * y
            mix = mix + _dot(y.astype(BF16), w_o_ref[h * dv:(h + 1) * dv, :])
        o_ref[c * chunk:(c + 1) * chunk, :] = _layer_norm(mix, g_ref[...], b_ref[...])


def _retention_layer(h, pos, w_in_stack, gn_g, w_o_stack, layer, g, b):
    bsz, seq, d = h.shape
    heads = RET_HEADS
    dk = d // heads
    dv = 2 * dk
    n_in = w_in_stack.shape[2]
    log_gamma = jnp.log1p(-jnp.exp2(-5.0 - jnp.arange(heads, dtype=F32)))
    half = dk // 2
    inv = (ROPE_BASE ** (-jnp.arange(half, dtype=F32) / half)).reshape(1, half)
    tile = RET_TILE
    grid_spec = pltpu.PrefetchScalarGridSpec(
        num_scalar_prefetch=1,
        grid=(bsz, seq // tile),
        in_specs=[
            pl.BlockSpec((None, tile, d), lambda bi, ci, lg: (bi, ci, 0)),
            pl.BlockSpec((None, tile, 1), lambda bi, ci, lg: (bi, ci, 0)),
            _resident((1, half)),
            _resident((d, n_in), layer),
            _resident((1, heads * dv)),
            _resident((heads * dv, d), layer),
            _resident((1, d)),
            _resident((1, d)),
        ],
        out_specs=pl.BlockSpec((None, tile, d), lambda bi, ci, lg: (bi, ci, 0)),
        scratch_shapes=[pltpu.VMEM((heads, dk, dv), F32)],
    )
    return pl.pallas_call(
        _ret_kernel,
        out_shape=jax.ShapeDtypeStruct((bsz, seq, d), F32),
        grid_spec=grid_spec,
        compiler_params=pltpu.CompilerParams(
            dimension_semantics=("parallel", "arbitrary"), vmem_limit_bytes=V7X_VMEM_LIMIT),
        name="retention_layer",
    )(log_gamma, h, pos, inv, w_in_stack, gn_g.reshape(1, -1), w_o_stack,
      g.reshape(1, d), b.reshape(1, d))


def _gelu_tanh(z):
    c = math.sqrt(2.0 / math.pi)
    return z * _sigmoid((2.0 * c) * (z + 0.044715 * (z * z * z)))


def _softplus(z):
    return jnp.maximum(z, 0.0) + jnp.log1p(jnp.exp(-jnp.abs(z)))


def _lru_kernel(x_ref, w_in_ref, conv_w_ref, conv_b_ref, w_a_ref, b_a_ref, w_x_ref, b_x_ref,
                lam_ref, w_o_ref, g_ref, b_ref, o_ref, tail_ref, h_ref):
    sub = LRU_SUB
    ts = range(x_ref.shape[0] // sub)
    width = conv_b_ref.shape[1]
    nblk, bdim, _ = w_a_ref.shape
    slab = 8

    @pl.when(pl.program_id(1) == 0)
    def _():
        tail_ref[...] = jnp.zeros_like(tail_ref)
        h_ref[...] = jnp.zeros_like(h_ref)

    xb = [x_ref[t * sub:(t + 1) * sub, :].astype(BF16) for t in ts]
    gate_raw = [_dot(xb[t], w_in_ref[:, :width]) for t in ts]
    rec = [_dot(xb[t], w_in_ref[:, width:]) for t in ts]

    row8 = lax.broadcasted_iota(jnp.int32, (slab, width), 0)
    tail = tail_ref[...]
    us = []
    for t in ts:
        u = rec[t] * conv_w_ref[CONV_WIDTH - 1:CONV_WIDTH, :] + conv_b_ref[...]
        for s in range(1, CONV_WIDTH):
            shifted = pltpu.roll(rec[t], s, 0)
            head = jnp.where(row8 < s, pltpu.roll(tail, s, 0), shifted[:slab])
            shifted = jnp.concatenate([head, shifted[slab:]], axis=0)
            u = u + shifted * conv_w_ref[CONV_WIDTH - 1 - s:CONV_WIDTH - s, :]
        tail = rec[t][sub - slab:, :]
        us.append(u)
    tail_ref[...] = tail

    pre_a, pre_x = [], []
    for t in ts:
        ubs = [us[t][:, n * bdim:(n + 1) * bdim].astype(BF16) for n in range(nblk)]
        pre_a.append(jnp.concatenate([_dot(ubs[n], w_a_ref[n]) for n in range(nblk)], axis=-1))
        pre_x.append(jnp.concatenate([_dot(ubs[n], w_x_ref[n]) for n in range(nblk)], axis=-1))

    neg_softplus_lam = -_softplus(-lam_ref[...])
    carry = h_ref[...]
    ys = []
    for t in ts:
        r = _sigmoid(pre_a[t] + b_a_ref[...])
        i_gate = _sigmoid(pre_x[t] + b_x_ref[...])
        log_a = LRU_C * r * neg_softplus_lam
        a = jnp.exp(log_a)
        b_in = jnp.sqrt(1.0 - jnp.exp(2.0 * log_a)) * (i_gate * us[t])
        a3 = a.reshape(sub // slab, slab, width)
        b3 = b_in.reshape(sub // slab, slab, width)
        rows = lax.broadcasted_iota(jnp.int32, a3.shape, 1)
        step = 1
        while step < slab:
            a_sh = pltpu.roll(a3, step, 1)
            b_sh = pltpu.roll(b3, step, 1)
            live = rows >= step
            b3 = jnp.where(live, a3 * b_sh + b3, b3)
            a3 = jnp.where(live, a3 * a_sh, a3)
            step *= 2
        slabs = []
        for i in range(sub // slab):
            hs_i = a3[i] * carry + b3[i]
            carry = hs_i[slab - 1:slab, :]
            slabs.append(hs_i)
        hs = jnp.concatenate(slabs, axis=0)
        ys.append((hs * _gelu_tanh(gate_raw[t])).astype(BF16))
    h_ref[...] = carry

    for t in ts:
        mix = ALPHA * x_ref[t * sub:(t + 1) * sub, :] + _dot(ys[t], w_o_ref[...])
        o_ref[t * sub:(t + 1) * sub, :] = _layer_norm(mix, g_ref[...], b_ref[...])


def _rglru_layer(h, w_in, conv_w, conv_b, w_a, b_a, w_x, b_x, lam, w_o, g, b):
    bsz, seq, d = h.shape
    width = conv_b.shape[0]
    tile = LRU_TILE
    row = lambda t: t.reshape(1, -1)
    return pl.pallas_call(
        _lru_kernel,
        out_shape=jax.ShapeDtypeStruct((bsz, seq, d), F32),
        grid=(bsz, seq // tile),
        in_specs=[
            pl.BlockSpec((None, tile, d), lambda bi, ti: (bi, ti, 0)),
            _resident((d, 2 * width)),
            _resident((CONV_WIDTH, width)),
            _resident((1, width)),
            _resident(w_a.shape),
            _resident((1, width)),
            _resident(w_x.shape),
            _resident((1, width)),
            _resident((1, width)),
            _resident((width, d)),
            _resident((1, d)),
            _resident((1, d)),
        ],
        out_specs=pl.BlockSpec((None, tile, d), lambda bi, ti: (bi, ti, 0)),
        scratch_shapes=[pltpu.VMEM((8, width), F32), pltpu.VMEM((1, width), F32)],
        compiler_params=pltpu.CompilerParams(
            dimension_semantics=("parallel", "arbitrary"), vmem_limit_bytes=V7X_VMEM_LIMIT),
        name="rglru_layer",
    )(h, w_in.astype(BF16), conv_w, row(conv_b), w_a.astype(BF16), row(b_a),
      w_x.astype(BF16), row(b_x), row(lam), w_o.astype(BF16), row(g), row(b))


def _mla_proj_kernel(x_ref, pos_ref, inv_ref, w_down_ref, w_kpe_ref, qn_g_ref, kvn_g_ref,
                     w_uq_ref, w_ukv_ref, q_ref, k_ref, v_ref):
    heads = q_ref.shape[0]
    nope, lanes = MLA_NOPE, 2 * MLA_ROPE
    q_rank = qn_g_ref.shape[1]
    scale = (MLA_NOPE + MLA_ROPE) ** -0.5 * LOG2_E
    xb = x_ref[...].astype(BF16)
    ang = pos_ref[...] * inv_ref[...]
    cos = jnp.cos(ang)
    sin = jnp.sin(ang)

    down = _dot(xb, w_down_ref[...])
    kpe = _dot(xb, w_kpe_ref[...])
    c_q = _rms_norm(down[:, :q_rank], qn_g_ref[...]).astype(BF16)
    c_kv = _rms_norm(down[:, q_rank:], kvn_g_ref[...]).astype(BF16)
    uq = _dot(c_q, w_uq_ref[...])
    ukv = _dot(c_kv, w_ukv_ref[...])

    k_pe = [(kpe[:, par * lanes:(par + 1) * lanes] * cos
             + kpe[:, (2 + par) * lanes:(3 + par) * lanes] * sin).astype(BF16) for par in (0, 1)]
    pairs = heads // 2
    pe_off = heads * nope
    rot_off = pe_off + pairs * lanes
    q_pe = [((uq[:, pe_off + g * lanes:pe_off + (g + 1) * lanes] * cos
              + uq[:, rot_off + g * lanes:rot_off + (g + 1) * lanes] * sin) * scale).astype(BF16)
            for g in range(pairs)]
    pad_rows = v_ref.shape[1] - MLA_V
    ones_row = jnp.where(lax.broadcasted_iota(jnp.int32, (pad_rows, v_ref.shape[2]), 0) == 0,
                         1.0, 0.0).astype(BF16)
    v_off = heads * nope
    for h in range(heads):
        q_ref[h, :, :nope] = (uq[:, h * nope:(h + 1) * nope] * scale).astype(BF16)
        q_ref[h, :, nope:] = q_pe[h // 2]
        k_ref[h, :, :nope] = ukv[:, h * nope:(h + 1) * nope].astype(BF16)
        k_ref[h, :, nope:] = k_pe[h % 2]
        v_ref[h, :MLA_V] = ukv[:, v_off + h * MLA_V:v_off + (h + 1) * MLA_V].T.astype(BF16)
        v_ref[h, MLA_V:] = ones_row


def _attn_kernel(q_ref, k_ref, vt_ref, o_ref, s_ref):
    i = pl.program_id(2)
    tq, dv = o_ref.shape
    tk = vt_ref.shape[2]
    sub = tk
    nsub = tq // sub
    kpq = tq // tk
    qs = [q_ref[s * sub:(s + 1) * sub, :] for s in range(nsub)]

    def scores(j, slot):
        kb = k_ref[pl.ds(pl.multiple_of(j * tk, tk), tk), :]
        maxima = []
        for s in range(nsub):
            st = _dot_nt(kb, qs[s])
            s_ref[slot, s] = st
            maxima.append(jnp.max(st, axis=0, keepdims=True))
        return tuple(maxima)

    def absorb(j, slot, stats, maxima):
        vt = vt_ref[j]
        ps, scaled = [], []
        for s, ((m_prev, acc), m_blk) in enumerate(zip(stats, maxima)):
            m_new = jnp.maximum(m_prev, m_blk)
            ps.append(jnp.exp2(s_ref[slot, s] - m_new).astype(BF16))
            scaled.append((m_new, jnp.exp2(m_prev - m_new) * acc))
        return tuple((m, a + _dot(vt, p)) for (m, a), p in zip(scaled, ps))

    tri = (lax.broadcasted_iota(jnp.int32, (tk, sub), 0)
           <= lax.broadcasted_iota(jnp.int32, (tk, sub), 1))
    diag_scores = {}
    for c in range(kpq):
        kb = k_ref[pl.ds(pl.multiple_of(i * tq + c * tk, tk), tk), :]
        for s in range(c, nsub):
            st = _dot_nt(kb, qs[s])
            diag_scores[c, s] = jnp.where(tri, st, -jnp.inf) if c == s else st
    diag_p = {}
    diag_m = []
    for s in range(nsub):
        m0 = jnp.max(diag_scores[0, s], axis=0, keepdims=True)
        for c in range(1, s + 1):
            m0 = jnp.maximum(m0, jnp.max(diag_scores[c, s], axis=0, keepdims=True))
        for c in range(s + 1):
            diag_p[c, s] = jnp.exp2(diag_scores[c, s] - m0).astype(BF16)
        diag_m.append(m0)
    stats = []
    for s in range(nsub):
        acc = _dot(vt_ref[i * kpq], diag_p[0, s])
        for c in range(1, s + 1):
            acc = acc + _dot(vt_ref[i * kpq + c], diag_p[c, s])
        stats.append((diag_m[s], acc))

    nslot = s_ref.shape[0]

    def body(t, carry):
        stats, maxima = carry
        for c in range(nslot):
            nxt = jnp.minimum(t * nslot + c + 1, i * kpq - 1)
            maxima_next = scores(nxt, (c + 1) % nslot)
            stats = absorb(t * nslot + c, c, stats, maxima)
            maxima = maxima_next
        return stats, maxima

    stats, _ = lax.fori_loop(0, i * (kpq // nslot), body, (tuple(stats), scores(0, 0)))
    for s in range(nsub):
        _, acc = stats[s]
        o_ref[s * sub:(s + 1) * sub, :] = (acc[:dv] / acc[dv:dv + 1]).T.astype(o_ref.dtype)


def _rot_half_cols(w):
    half = w.shape[-1] // 2
    return jnp.concatenate([-w[..., half:], w[..., :half]], axis=-1)


def _mla_layer(h, pos, w_in, q_norm_g, kv_norm_g, w_uq, w_ukv):
    bsz, seq, d = h.shape
    heads, nope, rope_d, vd = MLA_HEADS, MLA_NOPE, MLA_ROPE, MLA_V
    dqk = nope + 2 * rope_d
    half = rope_d // 2
    inv = ROPE_BASE ** (-jnp.arange(half, dtype=F32) / half)
    inv = jnp.tile(inv, 4).reshape(1, 2 * rope_d)

    w_down = w_in[:, :MLA_Q_RANK + MLA_KV_RANK]
    w_pe = w_in[:, MLA_Q_RANK + MLA_KV_RANK:]
    zero = jnp.zeros_like(w_pe)
    w_pe_rot = _rot_half_cols(w_pe)
    w_kpe = jnp.concatenate([w_pe, zero, zero, w_pe, w_pe_rot, zero, zero, w_pe_rot], axis=1)
    w_uq3 = w_uq.reshape(MLA_Q_RANK, heads, nope + rope_d)
    w_qp = w_uq3[:, :, nope:]
    w_uq_all = jnp.concatenate(
        [w_uq3[:, :, :nope].reshape(MLA_Q_RANK, heads * nope),
         w_qp.reshape(MLA_Q_RANK, heads * rope_d),
         _rot_half_cols(w_qp).reshape(MLA_Q_RANK, heads * rope_d)], axis=1)
    w_ukv3 = w_ukv.reshape(MLA_KV_RANK, heads, nope + vd)
    w_ukv_all = jnp.concatenate(
        [w_ukv3[:, :, :nope].reshape(MLA_KV_RANK, heads * nope),
         w_ukv3[:, :, nope:].reshape(MLA_KV_RANK, heads * vd)], axis=1)

    tile = MLA_PROJ_TILE
    per_tk = ATTN_TK // tile
    bf = lambda t: t.astype(BF16)
    q, k, v = pl.pallas_call(
        _mla_proj_kernel,
        out_shape=(jax.ShapeDtypeStruct((bsz, heads, seq, dqk), BF16),
                   jax.ShapeDtypeStruct((bsz, heads, seq, dqk), BF16),
                   jax.ShapeDtypeStruct((bsz, heads, seq // ATTN_TK, vd + ATTN_VT_PAD, ATTN_TK), BF16)),
        grid=(bsz, seq // tile),
        in_specs=[
            pl.BlockSpec((None, tile, d), lambda bi, ti: (bi, ti, 0)),
            pl.BlockSpec((None, tile, 1), lambda bi, ti: (bi, ti, 0)),
            _resident((1, 2 * rope_d)),
            _resident(w_down.shape), _resident(w_kpe.shape),
            _resident((1, MLA_Q_RANK)), _resident((1, MLA_KV_RANK)),
            _resident(w_uq_all.shape), _resident(w_ukv_all.shape),
        ],
        out_specs=(pl.BlockSpec((None, heads, tile, dqk), lambda bi, ti: (bi, 0, ti, 0)),
                   pl.BlockSpec((None, heads, tile, dqk), lambda bi, ti: (bi, 0, ti, 0)),
                   pl.BlockSpec((None, heads, None, vd + ATTN_VT_PAD, tile),
                                lambda bi, ti: (bi, 0, ti // per_tk, 0, ti % per_tk))),
        compiler_params=pltpu.CompilerParams(
            dimension_semantics=("parallel", "parallel"), vmem_limit_bytes=V7X_VMEM_LIMIT),
        name="mla_proj",
    )(h, pos, inv, bf(w_down), bf(w_kpe), q_norm_g.reshape(1, -1), kv_norm_g.reshape(1, -1),
      bf(w_uq_all), bf(w_ukv_all))

    attn = pl.pallas_call(
        _attn_kernel,
        out_shape=jax.ShapeDtypeStruct((bsz, seq, heads * vd), BF16),
        grid=(bsz, heads, seq // ATTN_TQ),
        in_specs=[
            pl.BlockSpec((None, None, ATTN_TQ, dqk), lambda bi, hi, qi: (bi, hi, qi, 0)),
            pl.BlockSpec((None, None, seq, dqk), lambda bi, hi, qi: (bi, hi, 0, 0)),
            pl.BlockSpec((None, None, seq // ATTN_TK, vd + ATTN_VT_PAD, ATTN_TK),
                         lambda bi, hi, qi: (bi, hi, 0, 0, 0)),
        ],
        out_specs=pl.BlockSpec((None, ATTN_TQ, vd), lambda bi, hi, qi: (bi, qi, hi)),
        scratch_shapes=[pltpu.VMEM((ATTN_SLOTS, ATTN_TQ // ATTN_TK, ATTN_TK, ATTN_TK),
                                   F32)],
        compiler_params=pltpu.CompilerParams(
            dimension_semantics=("parallel", "parallel", "arbitrary"),
            vmem_limit_bytes=V7X_VMEM_LIMIT),
        name="mla_attention",
    )(q, k, v)

    return attn.reshape(bsz * seq, heads * vd)


def kernel(x, positions, ret_w_in, ret_gn_g, ret_w_o, lru_w_in, lru_conv_w, lru_conv_b, lru_w_a,
           lru_b_a, lru_w_x, lru_b_x, lru_lam, lru_w_o, mla_w_in, mla_q_norm, mla_kv_norm,
           mla_w_uq, mla_w_ukv, mla_w_o, ln_g, ln_b, mlp_w1, mlp_w2):
    bsz, seq, d = x.shape
    pos = positions.astype(F32).reshape(bsz, seq, 1)
    ret_w_in_b, ret_w_o_b = ret_w_in.astype(BF16), ret_w_o.astype(BF16)
    mlp_w1_b, mlp_w2_b = mlp_w1.astype(BF16), mlp_w2.astype(BF16)
    h = x
    for i in range(DEPTH):
        kind, j = i % N_MIXERS, i // N_MIXERS
        pre = None
        if kind == 0:
            h = _retention_layer(h, pos, ret_w_in_b, ret_gn_g[j], ret_w_o_b, j,
                                 ln_g[i, 0], ln_b[i, 0])
        elif kind == 1:
            h = _rglru_layer(h, lru_w_in[j], lru_conv_w[j], lru_conv_b[j], lru_w_a[j],
                             lru_b_a[j], lru_w_x[j], lru_b_x[j], lru_lam[j], lru_w_o[j],
                             ln_g[i, 0], ln_b[i, 0])
        else:
            y = _mla_layer(h, pos, mla_w_in[j], mla_q_norm[j], mla_kv_norm[j], mla_w_uq[j],
                           mla_w_ukv[j])
            pre = (y, mla_w_o[j].astype(BF16), ln_g[i, 0], ln_b[i, 0])
        h = _mlp_layer(h.reshape(bsz * seq, d), mlp_w1_b, mlp_w2_b, i,
                       ln_g[i, 1], ln_b[i, 1], pre=pre).reshape(bsz, seq, d)
    return h
```

```python
import functools
import math

import jax
import jax.numpy as jnp
from jax import lax
from jax.experimental import pallas as pl
from jax.experimental.pallas import tpu as pltpu

F32 = jnp.float32
BF16 = jnp.bfloat16

DEPTH = 4
N_MIXERS = 3
RET_HEADS = 4
GN_EPS = 1e-5
LRU_BLOCKS = 10
CONV_WIDTH = 4
LRU_C = 8.0
MLA_HEADS = 8
MLA_NOPE = 128
MLA_ROPE = 64
MLA_V = 128
MLA_Q_RANK = 384
MLA_KV_RANK = 256
ROPE_BASE = 10000.0
LN_EPS = 1e-5
RMS_EPS = 1e-6
ALPHA = (2.0 * DEPTH) ** 0.25
LOG2_E = math.log2(math.e)

RET_CHUNK = 256
RET_TILE = 512
LRU_TILE = 512
LRU_SUB = 256
MLP_TILE = 1024
MLP_SUB = 512
MLA_PROJ_TILE = 256
ATTN_TQ = 1024
ATTN_SUB = 256
ATTN_TK = 512
ATTN_SLOTS = 2
ATTN_VT_PAD = 16
FF_CHUNK = 1024

V7X_VMEM_LIMIT = 56 * 1024 * 1024


def _resident(shape, layer=None):
    nd = len(shape)
    if layer is None:
        return pl.BlockSpec(shape, lambda *_: (0,) * nd, pipeline_mode=pl.Buffered(1))
    return pl.BlockSpec((None,) + tuple(shape), lambda *_: (layer,) + (0,) * nd,
                        pipeline_mode=pl.Buffered(1))


def _dot(a, b):
    return jnp.dot(a, b, preferred_element_type=F32)


def _dot_nt(a, b):
    return lax.dot_general(a, b, (((1,), (1,)), ((), ())), preferred_element_type=F32)


def _layer_norm(z, g, b):
    mu = jnp.mean(z, axis=-1, keepdims=True)
    zc = z - mu
    var = jnp.mean(zc * zc, axis=-1, keepdims=True)
    return zc * lax.rsqrt(var + LN_EPS) * g + b


def _rms_norm(z, g):
    return z * lax.rsqrt(jnp.mean(z * z, axis=-1, keepdims=True) + RMS_EPS) * g


def _sigmoid(z):
    return jax.nn.sigmoid(z)


def _mlp_residual(x, w1_ref, w2_ref):
    xb = x.astype(BF16)
    acc = ALPHA * x
    for c in range(w1_ref.shape[1] // FF_CHUNK):
        a = _dot(xb, w1_ref[:, c * FF_CHUNK:(c + 1) * FF_CHUNK])
        a = jnp.square(jnp.maximum(a, 0.0)).astype(BF16)
        acc = acc + _dot(a, w2_ref[c * FF_CHUNK:(c + 1) * FF_CHUNK, :])
    return acc


def _mlp_kernel(*refs, pre_proj):
    if pre_proj:
        h_ref, y_ref, w_o_ref, g0_ref, b0_ref, w1_ref, w2_ref, g_ref, b_ref, o_ref = refs
    else:
        h_ref, w1_ref, w2_ref, g_ref, b_ref, o_ref = refs
    parts = range(h_ref.shape[0] // MLP_SUB)
    accs = []
    for p in parts:
        x = h_ref[p * MLP_SUB:(p + 1) * MLP_SUB, :]
        if pre_proj:
            mix = ALPHA * x + _dot(y_ref[p * MLP_SUB:(p + 1) * MLP_SUB, :], w_o_ref[...])
            x = _layer_norm(mix, g0_ref[...], b0_ref[...])
        accs.append(_mlp_residual(x, w1_ref, w2_ref))
    for p in parts:
        o_ref[p * MLP_SUB:(p + 1) * MLP_SUB, :] = _layer_norm(accs[p], g_ref[...], b_ref[...])


def _mlp_layer(h, w1_stack, w2_stack, layer, g, b, pre=None):
    m, d = h.shape
    d_ff = w1_stack.shape[2]
    tile = MLP_TILE if pre is None else MLP_SUB
    rows = lambda width: pl.BlockSpec((tile, width), lambda i: (i, 0))
    operands = [h]
    in_specs = [rows(d)]
    if pre is not None:
        y, w_o, g0, b0 = pre
        operands += [y, w_o, g0.reshape(1, d), b0.reshape(1, d)]
        in_specs += [rows(y.shape[1]), _resident(w_o.shape), _resident((1, d)), _resident((1, d))]
    operands += [w1_stack, w2_stack, g.reshape(1, d), b.reshape(1, d)]
    in_specs += [_resident((d, d_ff), layer), _resident((d_ff, d), layer),
                 _resident((1, d)), _resident((1, d))]
    return pl.pallas_call(
        functools.partial(_mlp_kernel, pre_proj=pre is not None),
        out_shape=jax.ShapeDtypeStruct((m, d), F32),
        grid=(m // tile,),
        in_specs=in_specs,
        out_specs=rows(d),
        compiler_params=pltpu.CompilerParams(
            dimension_semantics=("parallel",), vmem_limit_bytes=V7X_VMEM_LIMIT),
        name="mlp_ln",
    )(*operands)


def _ret_kernel(lg_ref, x_ref, pos_ref, inv_ref, w_in_ref, gn_ref, w_o_ref, g_ref, b_ref,
                o_ref, state_ref):
    heads, dk, dv = state_ref.shape
    half = dk // 2
    chunk = RET_CHUNK
    cs = range(x_ref.shape[0] // chunk)
    hs = range(heads)

    @pl.when(pl.program_id(1) == 0)
    def _():
        state_ref[...] = jnp.zeros_like(state_ref)

    def rows(ref, c):
        return ref[c * chunk:(c + 1) * chunk, :]

    k_off = heads * dk
    v_off = 2 * heads * dk
    g_off = v_off + heads * dv
    xb = [rows(x_ref, c).astype(BF16) for c in cs]
    q_raw = [[_dot(xb[c], w_in_ref[:, h * dk:(h + 1) * dk]) for h in hs] for c in cs]
    k_raw = [[_dot(xb[c], w_in_ref[:, k_off + h * dk:k_off + (h + 1) * dk]) for h in hs]
             for c in cs]
    vb = [[_dot(xb[c], w_in_ref[:, v_off + h * dv:v_off + (h + 1) * dv]).astype(BF16)
           for h in hs] for c in cs]
    gates = [[_dot(xb[c], w_in_ref[:, g_off + h * dv:g_off + (h + 1) * dv]) for h in hs]
             for c in cs]

    lgs = [lg_ref[h] for h in hs]
    row = lax.broadcasted_iota(jnp.int32, (chunk, chunk), 0)
    col = lax.broadcasted_iota(jnp.int32, (chunk, chunk), 1)
    diff = jnp.maximum(row - col, 0).astype(F32)
    idx = lax.broadcasted_iota(jnp.int32, (chunk, 1), 0).astype(F32)
    intra = [jnp.where(row >= col, jnp.exp(lgs[h] * diff), 0.0) for h in hs]
    q_dec = [jnp.exp(lgs[h] * (idx + 1.0)) for h in hs]
    k_dec = [jnp.exp(lgs[h] * (chunk - 1.0 - idx)) for h in hs]
    chunk_dec = [jnp.exp(lgs[h] * chunk) for h in hs]

    state = [state_ref[h] for h in hs]
    outs = []
    for c in cs:
        ang = rows(pos_ref, c) * inv_ref[...]
        cos = jnp.cos(ang)
        sin = jnp.sin(ang)

        def rope(t):
            t1, t2 = t[:, :half], t[:, half:]
            return jnp.concatenate([t1 * cos - t2 * sin, t1 * sin + t2 * cos], axis=-1)

        q = [rope(t) for t in q_raw[c]]
        k = [rope(t) * (dk ** -0.5) for t in k_raw[c]]
        scores = [_dot_nt(q[h].astype(BF16), k[h].astype(BF16)) for h in hs]
        old = [state[h].astype(BF16) for h in hs]
        state = [state[h] * chunk_dec[h] + _dot((k[h] * k_dec[h]).T.astype(BF16), vb[c][h])
                 for h in hs]
        outs.append([
            _dot(jnp.concatenate([(scores[h] * intra[h]).astype(BF16),
                                  (q[h] * q_dec[h]).astype(BF16)], axis=1),
                 jnp.concatenate([vb[c][h], old[h]], axis=0))
            for h in hs])
    for h in hs:
        state_ref[h] = state[h]

    for c in cs:
        mix = ALPHA * rows(x_ref, c)
        for h in hs:
            o = outs[c][h]
            mu = jnp.mean(o, axis=-1, keepdims=True)
            oc = o - mu
            var = jnp.mean(oc * oc, axis=-1, keepdims=True)
            y = oc * lax.rsqrt(var + GN_EPS) * gn_ref[:, h * dv:(h + 1) * dv]
            y = gates[c][h] * _sigmoid(gates[c][h]) * y
            mix = mix + _dot(y.astype(BF16), w_o_ref[h * dv:(h + 1) * dv, :])
        o_ref[c * chunk:(c + 1) * chunk, :] = _layer_norm(mix, g_ref[...], b_ref[...])


def _retention_layer(h, pos, w_in_stack, gn_g, w_o_stack, layer, g, b):
    bsz, seq, d = h.shape
    heads = RET_HEADS
    dk = d // heads
    dv = 2 * dk
    n_in = w_in_stack.shape[2]
    log_gamma = jnp.log1p(-jnp.exp2(-5.0 - jnp.arange(heads, dtype=F32)))
    half = dk // 2
    inv = (ROPE_BASE ** (-jnp.arange(half, dtype=F32) / half)).reshape(1, half)
    tile = RET_TILE
    grid_spec = pltpu.PrefetchScalarGridSpec(
        num_scalar_prefetch=1,
        grid=(bsz, seq // tile),
        in_specs=[
            pl.BlockSpec((None, tile, d), lambda bi, ci, lg: (bi, ci, 0)),
            pl.BlockSpec((None, tile, 1), lambda bi, ci, lg: (bi, ci, 0)),
            _resident((1, half)),
            _resident((d, n_in), layer),
            _resident((1, heads * dv)),
            _resident((heads * dv, d), layer),
            _resident((1, d)),
            _resident((1, d)),
        ],
        out_specs=pl.BlockSpec((None, tile, d), lambda bi, ci, lg: (bi, ci, 0)),
        scratch_shapes=[pltpu.VMEM((heads, dk, dv), F32)],
    )
    return pl.pallas_call(
        _ret_kernel,
        out_shape=jax.ShapeDtypeStruct((bsz, seq, d), F32),
        grid_spec=grid_spec,
        compiler_params=pltpu.CompilerParams(
            dimension_semantics=("parallel", "arbitrary"), vmem_limit_bytes=V7X_VMEM_LIMIT),
        name="retention_layer",
    )(log_gamma, h, pos, inv, w_in_stack, gn_g.reshape(1, -1), w_o_stack,
      g.reshape(1, d), b.reshape(1, d))


def _gelu_tanh(z):
    c = math.sqrt(2.0 / math.pi)
    return z * _sigmoid((2.0 * c) * (z + 0.044715 * (z * z * z)))


def _softplus(z):
    return jnp.maximum(z, 0.0) + jnp.log1p(jnp.exp(-jnp.abs(z)))


def _lru_kernel(x_ref, w_in_ref, conv_w_ref, conv_b_ref, w_a_ref, b_a_ref, w_x_ref, b_x_ref,
                lam_ref, w_o_ref, g_ref, b_ref, w1_ref, w2_ref, g2_ref, b2_ref,
                o_ref, tail_ref, h_ref, pend_ref, *, steps_per_seq):
    sub = LRU_SUB
    ts = range(x_ref.shape[0] // sub)
    width = conv_b_ref.shape[1]
    nblk, bdim, _ = w_a_ref.shape
    slab = 8
    blk = pl.program_id(0)

    @pl.when(blk % steps_per_seq == 0)
    def _():
        tail_ref[...] = jnp.zeros_like(tail_ref)
        h_ref[...] = jnp.zeros_like(h_ref)

    @pl.when(blk == 0)
    def _():
        pend_ref[...] = jnp.zeros_like(pend_ref)

    pending = [pend_ref[t * sub:(t + 1) * sub, :] for t in ts]

    xb = [x_ref[t * sub:(t + 1) * sub, :].astype(BF16) for t in ts]
    gate_raw = [_dot(xb[t], w_in_ref[:, :width]) for t in ts]
    rec = [_dot(xb[t], w_in_ref[:, width:]) for t in ts]

    row8 = lax.broadcasted_iota(jnp.int32, (slab, width), 0)
    tail = tail_ref[...]
    us = []
    for t in ts:
        u = rec[t] * conv_w_ref[CONV_WIDTH - 1:CONV_WIDTH, :] + conv_b_ref[...]
        for s in range(1, CONV_WIDTH):
            shifted = pltpu.roll(rec[t], s, 0)
            head = jnp.where(row8 < s, pltpu.roll(tail, s, 0), shifted[:slab])
            shifted = jnp.concatenate([head, shifted[slab:]], axis=0)
            u = u + shifted * conv_w_ref[CONV_WIDTH - 1 - s:CONV_WIDTH - s, :]
        tail = rec[t][sub - slab:, :]
        us.append(u)
    tail_ref[...] = tail

    def pending_mlp(t):
        o_ref[t * sub:(t + 1) * sub, :] = _layer_norm(
            _mlp_residual(pending[t], w1_ref, w2_ref), g2_ref[...], b2_ref[...])

    for t in ts[:len(ts) // 2]:
        pending_mlp(t)
    pre_a, pre_x = [], []
    for t in ts:
        ubs = [us[t][:, n * bdim:(n + 1) * bdim].astype(BF16) for n in range(nblk)]
        pre_a.append(jnp.concatenate([_dot(ubs[n], w_a_ref[n]) for n in range(nblk)], axis=-1))
        pre_x.append(jnp.concatenate([_dot(ubs[n], w_x_ref[n]) for n in range(nblk)], axis=-1))
    for t in ts[len(ts) // 2:]:
        pending_mlp(t)

    neg_softplus_lam = -_softplus(-lam_ref[...])
    carry = h_ref[...]
    ys = []
    for t in ts:
        r = _sigmoid(pre_a[t] + b_a_ref[...])
        i_gate = _sigmoid(pre_x[t] + b_x_ref[...])
        log_a = LRU_C * r * neg_softplus_lam
        a = jnp.exp(log_a)
        b_in = jnp.sqrt(1.0 - jnp.exp(2.0 * log_a)) * (i_gate * us[t])
        a3 = a.reshape(sub // slab, slab, width)
        b3 = b_in.reshape(sub // slab, slab, width)
        rows = lax.broadcasted_iota(jnp.int32, a3.shape, 1)
        step = 1
        while step < slab:
            a_sh = pltpu.roll(a3, step, 1)
            b_sh = pltpu.roll(b3, step, 1)
            live = rows >= step
            b3 = jnp.where(live, a3 * b_sh + b3, b3)
            a3 = jnp.where(live, a3 * a_sh, a3)
            step *= 2
        slabs = []
        for i in range(sub // slab):
            hs_i = a3[i] * carry + b3[i]
            carry = hs_i[slab - 1:slab, :]
            slabs.append(hs_i)
        hs = jnp.concatenate(slabs, axis=0)
        ys.append((hs * _gelu_tanh(gate_raw[t])).astype(BF16))
    h_ref[...] = carry

    for t in ts:
        mix = ALPHA * x_ref[t * sub:(t + 1) * sub, :] + _dot(ys[t], w_o_ref[...])
        pend_ref[t * sub:(t + 1) * sub, :] = _layer_norm(mix, g_ref[...], b_ref[...])


def _rglru_layer(h, w_in, conv_w, conv_b, w_a, b_a, w_x, b_x, lam, w_o, g, b,
                 w1_stack, w2_stack, layer, g2, b2):
    bsz, seq, d = h.shape
    d_ff = w1_stack.shape[2]
    width = conv_b.shape[0]
    tile = LRU_TILE
    row = lambda t: t.reshape(1, -1)
    per_seq = seq // tile
    blocks = bsz * per_seq

    def mixer_block(s):
        s = jnp.minimum(s, blocks - 1)
        return s // per_seq, s % per_seq, 0

    def mlp_block(s):
        s = jnp.maximum(s - 1, 0)
        return s // per_seq, s % per_seq, 0

    return pl.pallas_call(
        functools.partial(_lru_kernel, steps_per_seq=per_seq),
        out_shape=jax.ShapeDtypeStruct((bsz, seq, d), F32),
        grid=(blocks + 1,),
        in_specs=[
            pl.BlockSpec((None, tile, d), mixer_block),
            _resident((d, 2 * width)),
            _resident((CONV_WIDTH, width)),
            _resident((1, width)),
            _resident(w_a.shape),
            _resident((1, width)),
            _resident(w_x.shape),
            _resident((1, width)),
            _resident((1, width)),
            _resident((width, d)),
            _resident((1, d)),
            _resident((1, d)),
            _resident((d, d_ff), layer),
            _resident((d_ff, d), layer),
            _resident((1, d)),
            _resident((1, d)),
        ],
        out_specs=pl.BlockSpec((None, tile, d), mlp_block),
        scratch_shapes=[pltpu.VMEM((8, width), F32), pltpu.VMEM((1, width), F32),
                        pltpu.VMEM((tile, d), F32)],
        compiler_params=pltpu.CompilerParams(
            dimension_semantics=("arbitrary",), vmem_limit_bytes=V7X_VMEM_LIMIT),
        name="rglru_layer",
    )(h, w_in.astype(BF16), conv_w, row(conv_b), w_a.astype(BF16), row(b_a),
      w_x.astype(BF16), row(b_x), row(lam), w_o.astype(BF16), row(g), row(b),
      w1_stack, w2_stack, row(g2), row(b2))


def _mla_proj_kernel(x_ref, pos_ref, inv_ref, w_down_ref, w_kpe_ref, qn_g_ref, kvn_g_ref,
                     w_uq_ref, w_ukv_ref, q_ref, k_ref, v_ref):
    heads = q_ref.shape[0]
    nope, lanes = MLA_NOPE, 2 * MLA_ROPE
    q_rank = qn_g_ref.shape[1]
    scale = (MLA_NOPE + MLA_ROPE) ** -0.5 * LOG2_E
    xb = x_ref[...].astype(BF16)
    ang = pos_ref[...] * inv_ref[...]
    cos = jnp.cos(ang)
    sin = jnp.sin(ang)

    down = _dot(xb, w_down_ref[...])
    kpe = _dot(xb, w_kpe_ref[...])
    c_q = _rms_norm(down[:, :q_rank], qn_g_ref[...]).astype(BF16)
    c_kv = _rms_norm(down[:, q_rank:], kvn_g_ref[...]).astype(BF16)
    uq = _dot(c_q, w_uq_ref[...])
    ukv = _dot(c_kv, w_ukv_ref[...])

    k_pe = [(kpe[:, par * lanes:(par + 1) * lanes] * cos
             + kpe[:, (2 + par) * lanes:(3 + par) * lanes] * sin).astype(BF16) for par in (0, 1)]
    pairs = heads // 2
    pe_off = heads * nope
    rot_off = pe_off + pairs * lanes
    q_pe = [((uq[:, pe_off + g * lanes:pe_off + (g + 1) * lanes] * cos
              + uq[:, rot_off + g * lanes:rot_off + (g + 1) * lanes] * sin) * scale).astype(BF16)
            for g in range(pairs)]
    pad_rows = v_ref.shape[1] - MLA_V
    ones_row = jnp.where(lax.broadcasted_iota(jnp.int32, (pad_rows, v_ref.shape[2]), 0) == 0,
                         1.0, 0.0).astype(BF16)
    v_off = heads * nope
    for h in range(heads):
        q_ref[h, :, :nope] = (uq[:, h * nope:(h + 1) * nope] * scale).astype(BF16)
        q_ref[h, :, nope:] = q_pe[h // 2]
        k_ref[h, :, :nope] = ukv[:, h * nope:(h + 1) * nope].astype(BF16)
        k_ref[h, :, nope:] = k_pe[h % 2]
        v_ref[h, :MLA_V] = ukv[:, v_off + h * MLA_V:v_off + (h + 1) * MLA_V].T.astype(BF16)
        v_ref[h, MLA_V:] = ones_row


def _attn_kernel(q_ref, k_ref, vt_ref, o_ref, s_ref):
    i = pl.program_id(2)
    tq, dv = o_ref.shape
    kb_keys = vt_ref.shape[2]
    sub = tk = ATTN_SUB
    nsub = tq // sub
    kpq = tq // tk
    per_kb = kb_keys // tk
    qs = [q_ref[s * sub:(s + 1) * sub, :] for s in range(nsub)]

    def diag_vt(c):
        return vt_ref[i * (kpq // per_kb) + c // per_kb][:, (c % per_kb) * tk:
                                                         (c % per_kb + 1) * tk]

    def scores(j, slot):
        kb = k_ref[pl.ds(pl.multiple_of(j * kb_keys, kb_keys), kb_keys), :]
        maxima = []
        for s in range(nsub):
            st = _dot_nt(kb, qs[s])
            s_ref[slot, s] = st
            maxima.append(jnp.max(st, axis=0, keepdims=True))
        return tuple(maxima)

    def absorb(j, slot, stats, maxima):
        vt = vt_ref[j]
        ps, scaled = [], []
        for s, ((m_prev, acc), m_blk) in enumerate(zip(stats, maxima)):
            m_new = jnp.maximum(m_prev, m_blk)
            ps.append(jnp.exp2(s_ref[slot, s] - m_new).astype(BF16))
            scaled.append((m_new, jnp.exp2(m_prev - m_new) * acc))
        return tuple((m, a + _dot(vt, p)) for (m, a), p in zip(scaled, ps))

    tri = (lax.broadcasted_iota(jnp.int32, (tk, sub), 0)
           <= lax.broadcasted_iota(jnp.int32, (tk, sub), 1))
    diag_scores = {}
    for c in range(kpq):
        kb = k_ref[pl.ds(pl.multiple_of(i * tq + c * tk, tk), tk), :]
        for s in range(c, nsub):
            st = _dot_nt(kb, qs[s])
            diag_scores[c, s] = jnp.where(tri, st, -jnp.inf) if c == s else st
    diag_p = {}
    diag_m = []
    for s in range(nsub):
        m0 = jnp.max(diag_scores[0, s], axis=0, keepdims=True)
        for c in range(1, s + 1):
            m0 = jnp.maximum(m0, jnp.max(diag_scores[c, s], axis=0, keepdims=True))
        for c in range(s + 1):
            diag_p[c, s] = jnp.exp2(diag_scores[c, s] - m0).astype(BF16)
        diag_m.append(m0)
    stats = []
    for s in range(nsub):
        acc = _dot(diag_vt(0), diag_p[0, s])
        for c in range(1, s + 1):
            acc = acc + _dot(diag_vt(c), diag_p[c, s])
        stats.append((diag_m[s], acc))

    nslot = s_ref.shape[0]
    bpq = tq // kb_keys

    def body(t, carry):
        stats, maxima = carry
        for c in range(nslot):
            nxt = jnp.minimum(t * nslot + c + 1, i * bpq - 1)
            maxima_next = scores(nxt, (c + 1) % nslot)
            stats = absorb(t * nslot + c, c, stats, maxima)
            maxima = maxima_next
        return stats, maxima

    stats, _ = lax.fori_loop(0, i * (bpq // nslot), body, (tuple(stats), scores(0, 0)))
    for s in range(nsub):
        _, acc = stats[s]
        o_ref[s * sub:(s + 1) * sub, :] = (acc[:dv] / acc[dv:dv + 1]).T.astype(o_ref.dtype)


def _rot_half_cols(w):
    half = w.shape[-1] // 2
    return jnp.concatenate([-w[..., half:], w[..., :half]], axis=-1)


def _mla_layer(h, pos, w_in, q_norm_g, kv_norm_g, w_uq, w_ukv):
    bsz, seq, d = h.shape
    heads, nope, rope_d, vd = MLA_HEADS, MLA_NOPE, MLA_ROPE, MLA_V
    dqk = nope + 2 * rope_d
    half = rope_d // 2
    inv = ROPE_BASE ** (-jnp.arange(half, dtype=F32) / half)
    inv = jnp.tile(inv, 4).reshape(1, 2 * rope_d)

    w_down = w_in[:, :MLA_Q_RANK + MLA_KV_RANK]
    w_pe = w_in[:, MLA_Q_RANK + MLA_KV_RANK:]
    zero = jnp.zeros_like(w_pe)
    w_pe_rot = _rot_half_cols(w_pe)
    w_kpe = jnp.concatenate([w_pe, zero, zero, w_pe, w_pe_rot, zero, zero, w_pe_rot], axis=1)
    w_uq3 = w_uq.reshape(MLA_Q_RANK, heads, nope + rope_d)
    w_qp = w_uq3[:, :, nope:]
    w_uq_all = jnp.concatenate(
        [w_uq3[:, :, :nope].reshape(MLA_Q_RANK, heads * nope),
         w_qp.reshape(MLA_Q_RANK, heads * rope_d),
         _rot_half_cols(w_qp).reshape(MLA_Q_RANK, heads * rope_d)], axis=1)
    w_ukv3 = w_ukv.reshape(MLA_KV_RANK, heads, nope + vd)
    w_ukv_all = jnp.concatenate(
        [w_ukv3[:, :, :nope].reshape(MLA_KV_RANK, heads * nope),
         w_ukv3[:, :, nope:].reshape(MLA_KV_RANK, heads * vd)], axis=1)

    tile = MLA_PROJ_TILE
    per_tk = ATTN_TK // tile
    bf = lambda t: t.astype(BF16)
    q, k, v = pl.pallas_call(
        _mla_proj_kernel,
        out_shape=(jax.ShapeDtypeStruct((bsz, heads, seq, dqk), BF16),
                   jax.ShapeDtypeStruct((bsz, heads, seq, dqk), BF16),
                   jax.ShapeDtypeStruct((bsz, heads, seq // ATTN_TK, vd + ATTN_VT_PAD, ATTN_TK), BF16)),
        grid=(bsz, seq // tile),
        in_specs=[
            pl.BlockSpec((None, tile, d), lambda bi, ti: (bi, ti, 0)),
            pl.BlockSpec((None, tile, 1), lambda bi, ti: (bi, ti, 0)),
            _resident((1, 2 * rope_d)),
            _resident(w_down.shape), _resident(w_kpe.shape),
            _resident((1, MLA_Q_RANK)), _resident((1, MLA_KV_RANK)),
            _resident(w_uq_all.shape), _resident(w_ukv_all.shape),
        ],
        out_specs=(pl.BlockSpec((None, heads, tile, dqk), lambda bi, ti: (bi, 0, ti, 0)),
                   pl.BlockSpec((None, heads, tile, dqk), lambda bi, ti: (bi, 0, ti, 0)),
                   pl.BlockSpec((None, heads, None, vd + ATTN_VT_PAD, tile),
                                lambda bi, ti: (bi, 0, ti // per_tk, 0, ti % per_tk))),
        compiler_params=pltpu.CompilerParams(
            dimension_semantics=("parallel", "parallel"), vmem_limit_bytes=V7X_VMEM_LIMIT),
        name="mla_proj",
    )(h, pos, inv, bf(w_down), bf(w_kpe), q_norm_g.reshape(1, -1), kv_norm_g.reshape(1, -1),
      bf(w_uq_all), bf(w_ukv_all))

    attn = pl.pallas_call(
        _attn_kernel,
        out_shape=jax.ShapeDtypeStruct((bsz, seq, heads * vd), BF16),
        grid=(bsz, heads, seq // ATTN_TQ),
        in_specs=[
            pl.BlockSpec((None, None, ATTN_TQ, dqk), lambda bi, hi, qi: (bi, hi, qi, 0)),
            pl.BlockSpec((None, None, seq, dqk), lambda bi, hi, qi: (bi, hi, 0, 0)),
            pl.BlockSpec((None, None, seq // ATTN_TK, vd + ATTN_VT_PAD, ATTN_TK),
                         lambda bi, hi, qi: (bi, hi, 0, 0, 0)),
        ],
        out_specs=pl.BlockSpec((None, ATTN_TQ, vd), lambda bi, hi, qi: (bi, qi, hi)),
        scratch_shapes=[pltpu.VMEM((ATTN_SLOTS, ATTN_TQ // ATTN_SUB, ATTN_TK, ATTN_SUB),
                                   F32)],
        compiler_params=pltpu.CompilerParams(
            dimension_semantics=("parallel", "parallel", "arbitrary"),
            vmem_limit_bytes=V7X_VMEM_LIMIT),
        name="mla_attention",
    )(q, k, v)

    return attn.reshape(bsz * seq, heads * vd)


def kernel(x, positions, ret_w_in, ret_gn_g, ret_w_o, lru_w_in, lru_conv_w, lru_conv_b, lru_w_a,
           lru_b_a, lru_w_x, lru_b_x, lru_lam, lru_w_o, mla_w_in, mla_q_norm, mla_kv_norm,
           mla_w_uq, mla_w_ukv, mla_w_o, ln_g, ln_b, mlp_w1, mlp_w2):
    bsz, seq, d = x.shape
    pos = positions.astype(F32).reshape(bsz, seq, 1)
    ret_w_in_b, ret_w_o_b = ret_w_in.astype(BF16), ret_w_o.astype(BF16)
    mlp_w1_b, mlp_w2_b = mlp_w1.astype(BF16), mlp_w2.astype(BF16)
    h = x
    for i in range(DEPTH):
        kind, j = i % N_MIXERS, i // N_MIXERS
        pre = None
        if kind == 0:
            h = _retention_layer(h, pos, ret_w_in_b, ret_gn_g[j], ret_w_o_b, j,
                                 ln_g[i, 0], ln_b[i, 0])
        elif kind == 1:
            h = _rglru_layer(h, lru_w_in[j], lru_conv_w[j], lru_conv_b[j], lru_w_a[j],
                             lru_b_a[j], lru_w_x[j], lru_b_x[j], lru_lam[j], lru_w_o[j],
                             ln_g[i, 0], ln_b[i, 0], mlp_w1_b, mlp_w2_b, i,
                             ln_g[i, 1], ln_b[i, 1])
            continue
        else:
            y = _mla_layer(h, pos, mla_w_in[j], mla_q_norm[j], mla_kv_norm[j], mla_w_uq[j],
                           mla_w_ukv[j])
            pre = (y, mla_w_o[j].astype(BF16), ln_g[i, 0], ln_b[i, 0])
        h = _mlp_layer(h.reshape(bsz * seq, d), mlp_w1_b, mlp_w2_b, i,
                       ln_g[i, 1], ln_b[i, 1], pre=pre).reshape(bsz, seq, d)
    return h
```

```python
import functools
import math

import jax
import jax.numpy as jnp
from jax import lax
from jax.experimental import pallas as pl
from jax.experimental.pallas import tpu as pltpu

F32 = jnp.float32
BF16 = jnp.bfloat16

DEPTH = 4
N_MIXERS = 3
RET_HEADS = 4
GN_EPS = 1e-5
LRU_BLOCKS = 10
CONV_WIDTH = 4
LRU_C = 8.0
MLA_HEADS = 8
MLA_NOPE = 128
MLA_ROPE = 64
MLA_V = 128
MLA_Q_RANK = 384
MLA_KV_RANK = 256
ROPE_BASE = 10000.0
LN_EPS = 1e-5
RMS_EPS = 1e-6
ALPHA = (2.0 * DEPTH) ** 0.25
LOG2_E = math.log2(math.e)

RET_CHUNK = 256
RET_TILE = 512
LRU_TILE = 512
LRU_SUB = 256
MLP_TILE = 1024
MLP_SUB = 512
MLA_PROJ_TILE = 256
ATTN_TQ = 1024
ATTN_SUB = 256
ATTN_TK = 512
ATTN_SLOTS = 2
ATTN_VT_PAD = 16
FF_CHUNK = 1024

V7X_VMEM_LIMIT = 56 * 1024 * 1024


def _resident(shape, layer=None):
    nd = len(shape)
    if layer is None:
        return pl.BlockSpec(shape, lambda *_: (0,) * nd, pipeline_mode=pl.Buffered(1))
    return pl.BlockSpec((None,) + tuple(shape), lambda *_: (layer,) + (0,) * nd,
                        pipeline_mode=pl.Buffered(1))


def _dot(a, b):
    return jnp.dot(a, b, preferred_element_type=F32)


def _layer_norm(z, g, b):
    mu = jnp.mean(z, axis=-1, keepdims=True)
    zc = z - mu
    var = jnp.mean(zc * zc, axis=-1, keepdims=True)
    return zc * lax.rsqrt(var + LN_EPS) * g + b


def _rms_norm(z, g):
    return z * lax.rsqrt(jnp.mean(z * z, axis=-1, keepdims=True) + RMS_EPS) * g


def _sigmoid(z):
    return jax.nn.sigmoid(z)


def _mlp_residual(x, w1_ref, w2_ref):
    xb = x.astype(BF16)
    acc = ALPHA * x
    for c in range(w1_ref.shape[1] // FF_CHUNK):
        a = _dot(xb, w1_ref[:, c * FF_CHUNK:(c + 1) * FF_CHUNK])
        a = jnp.square(jnp.maximum(a, 0.0)).astype(BF16)
        acc = acc + _dot(a, w2_ref[c * FF_CHUNK:(c + 1) * FF_CHUNK, :])
    return acc


def _mlp_kernel(*refs, pre_proj):
    if pre_proj:
        h_ref, y_ref, w_o_ref, g0_ref, b0_ref, w1_ref, w2_ref, g_ref, b_ref, o_ref = refs
    else:
        h_ref, w1_ref, w2_ref, g_ref, b_ref, o_ref = refs
    parts = range(h_ref.shape[0] // MLP_SUB)
    accs = []
    for p in parts:
        x = h_ref[p * MLP_SUB:(p + 1) * MLP_SUB, :]
        if pre_proj:
            mix = ALPHA * x + _dot(y_ref[p * MLP_SUB:(p + 1) * MLP_SUB, :], w_o_ref[...])
            x = _layer_norm(mix, g0_ref[...], b0_ref[...])
        accs.append(_mlp_residual(x, w1_ref, w2_ref))
    for p in parts:
        o_ref[p * MLP_SUB:(p + 1) * MLP_SUB, :] = _layer_norm(accs[p], g_ref[...], b_ref[...])


def _mlp_layer(h, w1_stack, w2_stack, layer, g, b, pre=None):
    m, d = h.shape
    d_ff = w1_stack.shape[2]
    tile = MLP_TILE if pre is None else MLP_SUB
    rows = lambda width: pl.BlockSpec((tile, width), lambda i: (i, 0))
    operands = [h]
    in_specs = [rows(d)]
    if pre is not None:
        y, w_o, g0, b0 = pre
        operands += [y, w_o, g0.reshape(1, d), b0.reshape(1, d)]
        in_specs += [rows(y.shape[1]), _resident(w_o.shape), _resident((1, d)), _resident((1, d))]
    operands += [w1_stack, w2_stack, g.reshape(1, d), b.reshape(1, d)]
    in_specs += [_resident((d, d_ff), layer), _resident((d_ff, d), layer),
                 _resident((1, d)), _resident((1, d))]
    return pl.pallas_call(
        functools.partial(_mlp_kernel, pre_proj=pre is not None),
        out_shape=jax.ShapeDtypeStruct((m, d), F32),
        grid=(m // tile,),
        in_specs=in_specs,
        out_specs=rows(d),
        compiler_params=pltpu.CompilerParams(
            dimension_semantics=("parallel",), vmem_limit_bytes=V7X_VMEM_LIMIT),
        name="mlp_ln",
    )(*operands)


def _ret_kernel(lg_ref, x_ref, pos_ref, inv_ref, w_in_ref, gn_ref, w_o_ref, g_ref, b_ref,
                o_ref, state_ref):
    heads, dk, dv = state_ref.shape
    half = dk // 2
    chunk = RET_CHUNK
    cs = range(x_ref.shape[0] // chunk)
    hs = range(heads)

    @pl.when(pl.program_id(1) == 0)
    def _():
        state_ref[...] = jnp.zeros_like(state_ref)

    def rows(ref, c):
        return ref[c * chunk:(c + 1) * chunk, :]

    k_off = heads * dk
    v_off = 2 * heads * dk
    g_off = v_off + heads * dv
    xb = [rows(x_ref, c).astype(BF16) for c in cs]
    q_raw = [[_dot(xb[c], w_in_ref[:, h * dk:(h + 1) * dk]) for h in hs] for c in cs]
    k_raw = [[_dot(xb[c], w_in_ref[:, k_off + h * dk:k_off + (h + 1) * dk]) for h in hs]
             for c in cs]
    vb = [[_dot(xb[c], w_in_ref[:, v_off + h * dv:v_off + (h + 1) * dv]).astype(BF16)
           for h in hs] for c in cs]
    gates = [[_dot(xb[c], w_in_ref[:, g_off + h * dv:g_off + (h + 1) * dv]) for h in hs]
             for c in cs]

    lgs = [lg_ref[h] for h in hs]
    row = lax.broadcasted_iota(jnp.int32, (chunk, chunk), 0)
    col = lax.broadcasted_iota(jnp.int32, (chunk, chunk), 1)
    diff = jnp.maximum(row - col, 0).astype(F32)
    idx = lax.broadcasted_iota(jnp.int32, (chunk, 1), 0).astype(F32)
    intra = [jnp.where(row >= col, jnp.exp(lgs[h] * diff), 0.0) for h in hs]
    q_dec = [jnp.exp(lgs[h] * (idx + 1.0)) for h in hs]
    idx_t = lax.broadcasted_iota(jnp.int32, (1, chunk), 1).astype(F32)
    k_dec_t = [jnp.exp(lgs[h] * (chunk - 1.0 - idx_t)) for h in hs]
    chunk_dec = [jnp.exp(lgs[h] * chunk) for h in hs]

    state = [state_ref[h] for h in hs]
    outs = []
    for c in cs:
        ang = rows(pos_ref, c) * inv_ref[...]
        cos = jnp.cos(ang)
        sin = jnp.sin(ang)

        def rope(t):
            t1, t2 = t[:, :half], t[:, half:]
            return jnp.concatenate([t1 * cos - t2 * sin, t1 * sin + t2 * cos], axis=-1)

        q = [rope(t) for t in q_raw[c]]
        k_t = [(rope(t) * (dk ** -0.5)).T for t in k_raw[c]]
        scores = [_dot(q[h].astype(BF16), k_t[h].astype(BF16)) for h in hs]
        old = [state[h].astype(BF16) for h in hs]
        state = [state[h] * chunk_dec[h] + _dot((k_t[h] * k_dec_t[h]).astype(BF16), vb[c][h])
                 for h in hs]
        outs.append([
            _dot(jnp.concatenate([(scores[h] * intra[h]).astype(BF16),
                                  (q[h] * q_dec[h]).astype(BF16)], axis=1),
                 jnp.concatenate([vb[c][h], old[h]], axis=0))
            for h in hs])
    for h in hs:
        state_ref[h] = state[h]

    for c in cs:
        mix = ALPHA * rows(x_ref, c)
        for h in hs:
            o = outs[c][h]
            mu = jnp.mean(o, axis=-1, keepdims=True)
            oc = o - mu
            var = jnp.mean(oc * oc, axis=-1, keepdims=True)
            y = oc * lax.rsqrt(var + GN_EPS) * gn_ref[:, h * dv:(h + 1) * dv]
            y = gates[c][h] * _sigmoid(gates[c][h]) * y
            mix = mix + _dot(y.astype(BF16), w_o_ref[h * dv:(h + 1) * dv, :])
        o_ref[c * chunk:(c + 1) * chunk, :] = _layer_norm(mix, g_ref[...], b_ref[...])


def _retention_layer(h, pos, w_in_stack, gn_g, w_o_stack, layer, g, b):
    bsz, seq, d = h.shape
    heads = RET_HEADS
    dk = d // heads
    dv = 2 * dk
    n_in = w_in_stack.shape[2]
    log_gamma = jnp.log1p(-jnp.exp2(-5.0 - jnp.arange(heads, dtype=F32)))
    half = dk // 2
    inv = (ROPE_BASE ** (-jnp.arange(half, dtype=F32) / half)).reshape(1, half)
    tile = RET_TILE
    grid_spec = pltpu.PrefetchScalarGridSpec(
        num_scalar_prefetch=1,
        grid=(bsz, seq // tile),
        in_specs=[
            pl.BlockSpec((None, tile, d), lambda bi, ci, lg: (bi, ci, 0)),
            pl.BlockSpec((None, tile, 1), lambda bi, ci, lg: (bi, ci, 0)),
            _resident((1, half)),
            _resident((d, n_in), layer),
            _resident((1, heads * dv)),
            _resident((heads * dv, d), layer),
            _resident((1, d)),
            _resident((1, d)),
        ],
        out_specs=pl.BlockSpec((None, tile, d), lambda bi, ci, lg: (bi, ci, 0)),
        scratch_shapes=[pltpu.VMEM((heads, dk, dv), F32)],
    )
    return pl.pallas_call(
        _ret_kernel,
        out_shape=jax.ShapeDtypeStruct((bsz, seq, d), F32),
        grid_spec=grid_spec,
        compiler_params=pltpu.CompilerParams(
            dimension_semantics=("parallel", "arbitrary"), vmem_limit_bytes=V7X_VMEM_LIMIT),
        name="retention_layer",
    )(log_gamma, h, pos, inv, w_in_stack, gn_g.reshape(1, -1), w_o_stack,
      g.reshape(1, d), b.reshape(1, d))


def _gelu_tanh(z):
    c = math.sqrt(2.0 / math.pi)
    return z * _sigmoid((2.0 * c) * (z + 0.044715 * (z * z * z)))


def _softplus(z):
    return jnp.maximum(z, 0.0) + jnp.log1p(jnp.exp(-jnp.abs(z)))


def _lru_kernel(x_ref, w_in_ref, conv_w_ref, conv_b_ref, w_a_ref, b_a_ref, w_x_ref, b_x_ref,
                lam_ref, w_o_ref, g_ref, b_ref, w1_ref, w2_ref, g2_ref, b2_ref,
                o_ref, tail_ref, h_ref, pend_ref, *, steps_per_seq):
    sub = LRU_SUB
    ts = range(x_ref.shape[0] // sub)
    width = conv_b_ref.shape[1]
    nblk, bdim, _ = w_a_ref.shape
    slab = 8
    blk = pl.program_id(0)

    @pl.when(blk % steps_per_seq == 0)
    def _():
        tail_ref[...] = jnp.zeros_like(tail_ref)
        h_ref[...] = jnp.zeros_like(h_ref)

    @pl.when(blk == 0)
    def _():
        pend_ref[...] = jnp.zeros_like(pend_ref)

    pending = [pend_ref[t * sub:(t + 1) * sub, :] for t in ts]

    xb = [x_ref[t * sub:(t + 1) * sub, :].astype(BF16) for t in ts]
    gate_raw = [_dot(xb[t], w_in_ref[:, :width]) for t in ts]
    rec = [_dot(xb[t], w_in_ref[:, width:]) for t in ts]

    row8 = lax.broadcasted_iota(jnp.int32, (slab, width), 0)
    tail = tail_ref[...]
    us = []
    for t in ts:
        u = rec[t] * conv_w_ref[CONV_WIDTH - 1:CONV_WIDTH, :] + conv_b_ref[...]
        for s in range(1, CONV_WIDTH):
            shifted = pltpu.roll(rec[t], s, 0)
            head = jnp.where(row8 < s, pltpu.roll(tail, s, 0), shifted[:slab])
            shifted = jnp.concatenate([head, shifted[slab:]], axis=0)
            u = u + shifted * conv_w_ref[CONV_WIDTH - 1 - s:CONV_WIDTH - s, :]
        tail = rec[t][sub - slab:, :]
        us.append(u)
    tail_ref[...] = tail

    def pending_mlp(t):
        o_ref[t * sub:(t + 1) * sub, :] = _layer_norm(
            _mlp_residual(pending[t], w1_ref, w2_ref), g2_ref[...], b2_ref[...])

    for t in ts[:len(ts) // 2]:
        pending_mlp(t)
    pre_a, pre_x = [], []
    for t in ts:
        ubs = [us[t][:, n * bdim:(n + 1) * bdim].astype(BF16) for n in range(nblk)]
        pre_a.append(jnp.concatenate([_dot(ubs[n], w_a_ref[n]) for n in range(nblk)], axis=-1))
        pre_x.append(jnp.concatenate([_dot(ubs[n], w_x_ref[n]) for n in range(nblk)], axis=-1))
    for t in ts[len(ts) // 2:]:
        pending_mlp(t)

    neg_softplus_lam = -_softplus(-lam_ref[...])
    carry = h_ref[...]
    ys = []
    for t in ts:
        r = _sigmoid(pre_a[t] + b_a_ref[...])
        i_gate = _sigmoid(pre_x[t] + b_x_ref[...])
        log_a = LRU_C * r * neg_softplus_lam
        a = jnp.exp(log_a)
        b_in = jnp.sqrt(1.0 - jnp.exp(2.0 * log_a)) * (i_gate * us[t])
        a3 = a.reshape(sub // slab, slab, width)
        b3 = b_in.reshape(sub // slab, slab, width)
        rows = lax.broadcasted_iota(jnp.int32, a3.shape, 1)
        step = 1
        while step < slab:
            a_sh = pltpu.roll(a3, step, 1)
            b_sh = pltpu.roll(b3, step, 1)
            live = rows >= step
            b3 = jnp.where(live, a3 * b_sh + b3, b3)
            a3 = jnp.where(live, a3 * a_sh, a3)
            step *= 2
        slabs = []
        for i in range(sub // slab):
            hs_i = a3[i] * carry + b3[i]
            carry = hs_i[slab - 1:slab, :]
            slabs.append(hs_i)
        hs = jnp.concatenate(slabs, axis=0)
        ys.append((hs * _gelu_tanh(gate_raw[t])).astype(BF16))
    h_ref[...] = carry

    for t in ts:
        mix = ALPHA * x_ref[t * sub:(t + 1) * sub, :] + _dot(ys[t], w_o_ref[...])
        pend_ref[t * sub:(t + 1) * sub, :] = _layer_norm(mix, g_ref[...], b_ref[...])


def _rglru_layer(h, w_in, conv_w, conv_b, w_a, b_a, w_x, b_x, lam, w_o, g, b,
                 w1_stack, w2_stack, layer, g2, b2):
    bsz, seq, d = h.shape
    d_ff = w1_stack.shape[2]
    width = conv_b.shape[0]
    tile = LRU_TILE
    row = lambda t: t.reshape(1, -1)
    per_seq = seq // tile
    blocks = bsz * per_seq

    def mixer_block(s):
        s = jnp.minimum(s, blocks - 1)
        return s // per_seq, s % per_seq, 0

    def mlp_block(s):
        s = jnp.maximum(s - 1, 0)
        return s // per_seq, s % per_seq, 0

    return pl.pallas_call(
        functools.partial(_lru_kernel, steps_per_seq=per_seq),
        out_shape=jax.ShapeDtypeStruct((bsz, seq, d), F32),
        grid=(blocks + 1,),
        in_specs=[
            pl.BlockSpec((None, tile, d), mixer_block),
            _resident((d, 2 * width)),
            _resident((CONV_WIDTH, width)),
            _resident((1, width)),
            _resident(w_a.shape),
            _resident((1, width)),
            _resident(w_x.shape),
            _resident((1, width)),
            _resident((1, width)),
            _resident((width, d)),
            _resident((1, d)),
            _resident((1, d)),
            _resident((d, d_ff), layer),
            _resident((d_ff, d), layer),
            _resident((1, d)),
            _resident((1, d)),
        ],
        out_specs=pl.BlockSpec((None, tile, d), mlp_block),
        scratch_shapes=[pltpu.VMEM((8, width), F32), pltpu.VMEM((1, width), F32),
                        pltpu.VMEM((tile, d), F32)],
        compiler_params=pltpu.CompilerParams(
            dimension_semantics=("arbitrary",), vmem_limit_bytes=V7X_VMEM_LIMIT),
        name="rglru_layer",
    )(h, w_in.astype(BF16), conv_w, row(conv_b), w_a.astype(BF16), row(b_a),
      w_x.astype(BF16), row(b_x), row(lam), w_o.astype(BF16), row(g), row(b),
      w1_stack, w2_stack, row(g2), row(b2))


def _mla_proj_kernel(x_ref, pos_ref, inv_ref, w_down_ref, w_kpe_ref, qn_g_ref, kvn_g_ref,
                     w_uq_ref, w_ukv_ref, q_ref, k_ref, v_ref):
    heads = q_ref.shape[0]
    nope, lanes = MLA_NOPE, 2 * MLA_ROPE
    q_rank = qn_g_ref.shape[1]
    scale = (MLA_NOPE + MLA_ROPE) ** -0.5 * LOG2_E
    xb = x_ref[...].astype(BF16)
    ang = pos_ref[...] * inv_ref[...]
    cos = jnp.cos(ang)
    sin = jnp.sin(ang)

    down = _dot(xb, w_down_ref[...])
    kpe = _dot(xb, w_kpe_ref[...])
    c_q = _rms_norm(down[:, :q_rank], qn_g_ref[...]).astype(BF16)
    c_kv = _rms_norm(down[:, q_rank:], kvn_g_ref[...]).astype(BF16)
    uq = _dot(c_q, w_uq_ref[...])
    ukv = _dot(c_kv, w_ukv_ref[...])

    k_pe = [(kpe[:, par * lanes:(par + 1) * lanes] * cos
             + kpe[:, (2 + par) * lanes:(3 + par) * lanes] * sin).astype(BF16) for par in (0, 1)]
    pairs = heads // 2
    pe_off = heads * nope
    rot_off = pe_off + pairs * lanes
    q_pe_t = [((uq[:, pe_off + g * lanes:pe_off + (g + 1) * lanes] * cos
                + uq[:, rot_off + g * lanes:rot_off + (g + 1) * lanes] * sin) * scale
               ).T.astype(BF16) for g in range(pairs)]
    pad_rows = v_ref.shape[1] - MLA_V
    ones_row = jnp.where(lax.broadcasted_iota(jnp.int32, (pad_rows, v_ref.shape[2]), 0) == 0,
                         1.0, 0.0).astype(BF16)
    v_off = heads * nope
    for h in range(heads):
        q_ref[h, :nope, :] = (uq[:, h * nope:(h + 1) * nope] * scale).T.astype(BF16)
        q_ref[h, nope:, :] = q_pe_t[h // 2]
        k_ref[h, :, :nope] = ukv[:, h * nope:(h + 1) * nope].astype(BF16)
        k_ref[h, :, nope:] = k_pe[h % 2]
        v_ref[h, :MLA_V] = ukv[:, v_off + h * MLA_V:v_off + (h + 1) * MLA_V].T.astype(BF16)
        v_ref[h, MLA_V:] = ones_row


def _attn_kernel(q_ref, k_ref, vt_ref, o_ref, s_ref):
    i = pl.program_id(2)
    tq, dv = o_ref.shape
    kb_keys = vt_ref.shape[2]
    sub = tk = ATTN_SUB
    nsub = tq // sub
    kpq = tq // tk
    per_kb = kb_keys // tk
    qs = [q_ref[:, s * sub:(s + 1) * sub] for s in range(nsub)]

    def diag_vt(c):
        return vt_ref[i * (kpq // per_kb) + c // per_kb][:, (c % per_kb) * tk:
                                                         (c % per_kb + 1) * tk]

    def scores(j, slot):
        kb = k_ref[pl.ds(pl.multiple_of(j * kb_keys, kb_keys), kb_keys), :]
        maxima = []
        for s in range(nsub):
            st = _dot(kb, qs[s])
            s_ref[slot, s] = st
            maxima.append(jnp.max(st, axis=0, keepdims=True))
        return tuple(maxima)

    def absorb(j, slot, stats, maxima):
        vt = vt_ref[j]
        ps, scaled = [], []
        for s, ((m_prev, acc), m_blk) in enumerate(zip(stats, maxima)):
            m_new = jnp.maximum(m_prev, m_blk)
            ps.append(jnp.exp2(s_ref[slot, s] - m_new).astype(BF16))
            scaled.append((m_new, jnp.exp2(m_prev - m_new) * acc))
        return tuple((m, a + _dot(vt, p)) for (m, a), p in zip(scaled, ps))

    tri = (lax.broadcasted_iota(jnp.int32, (tk, sub), 0)
           <= lax.broadcasted_iota(jnp.int32, (tk, sub), 1))
    diag_scores = {}
    for c in range(kpq):
        kb = k_ref[pl.ds(pl.multiple_of(i * tq + c * tk, tk), tk), :]
        for s in range(c, nsub):
            st = _dot(kb, qs[s])
            diag_scores[c, s] = jnp.where(tri, st, -jnp.inf) if c == s else st
    diag_p = {}
    diag_m = []
    for s in range(nsub):
        m0 = jnp.max(diag_scores[0, s], axis=0, keepdims=True)
        for c in range(1, s + 1):
            m0 = jnp.maximum(m0, jnp.max(diag_scores[c, s], axis=0, keepdims=True))
        for c in range(s + 1):
            diag_p[c, s] = jnp.exp2(diag_scores[c, s] - m0).astype(BF16)
        diag_m.append(m0)
    stats = []
    for s in range(nsub):
        acc = _dot(diag_vt(0), diag_p[0, s])
        for c in range(1, s + 1):
            acc = acc + _dot(diag_vt(c), diag_p[c, s])
        stats.append((diag_m[s], acc))

    nslot = s_ref.shape[0]
    bpq = tq // kb_keys

    def body(t, carry):
        stats, maxima = carry
        for c in range(nslot):
            nxt = jnp.minimum(t * nslot + c + 1, i * bpq - 1)
            maxima_next = scores(nxt, (c + 1) % nslot)
            stats = absorb(t * nslot + c, c, stats, maxima)
            maxima = maxima_next
        return stats, maxima

    stats, _ = lax.fori_loop(0, i * (bpq // nslot), body, (tuple(stats), scores(0, 0)))
    for s in range(nsub):
        _, acc = stats[s]
        o_ref[s * sub:(s + 1) * sub, :] = (acc[:dv] / acc[dv:dv + 1]).T.astype(o_ref.dtype)


def _rot_half_cols(w):
    half = w.shape[-1] // 2
    return jnp.concatenate([-w[..., half:], w[..., :half]], axis=-1)


def _mla_layer(h, pos, w_in, q_norm_g, kv_norm_g, w_uq, w_ukv):
    bsz, seq, d = h.shape
    heads, nope, rope_d, vd = MLA_HEADS, MLA_NOPE, MLA_ROPE, MLA_V
    dqk = nope + 2 * rope_d
    half = rope_d // 2
    inv = ROPE_BASE ** (-jnp.arange(half, dtype=F32) / half)
    inv = jnp.tile(inv, 4).reshape(1, 2 * rope_d)

    w_down = w_in[:, :MLA_Q_RANK + MLA_KV_RANK]
    w_pe = w_in[:, MLA_Q_RANK + MLA_KV_RANK:]
    zero = jnp.zeros_like(w_pe)
    w_pe_rot = _rot_half_cols(w_pe)
    w_kpe = jnp.concatenate([w_pe, zero, zero, w_pe, w_pe_rot, zero, zero, w_pe_rot], axis=1)
    w_uq3 = w_uq.reshape(MLA_Q_RANK, heads, nope + rope_d)
    w_qp = w_uq3[:, :, nope:]
    w_uq_all = jnp.concatenate(
        [w_uq3[:, :, :nope].reshape(MLA_Q_RANK, heads * nope),
         w_qp.reshape(MLA_Q_RANK, heads * rope_d),
         _rot_half_cols(w_qp).reshape(MLA_Q_RANK, heads * rope_d)], axis=1)
    w_ukv3 = w_ukv.reshape(MLA_KV_RANK, heads, nope + vd)
    w_ukv_all = jnp.concatenate(
        [w_ukv3[:, :, :nope].reshape(MLA_KV_RANK, heads * nope),
         w_ukv3[:, :, nope:].reshape(MLA_KV_RANK, heads * vd)], axis=1)

    tile = MLA_PROJ_TILE
    per_tk = ATTN_TK // tile
    bf = lambda t: t.astype(BF16)
    q, k, v = pl.pallas_call(
        _mla_proj_kernel,
        out_shape=(jax.ShapeDtypeStruct((bsz, heads, dqk, seq), BF16),
                   jax.ShapeDtypeStruct((bsz, heads, seq, dqk), BF16),
                   jax.ShapeDtypeStruct((bsz, heads, seq // ATTN_TK, vd + ATTN_VT_PAD, ATTN_TK), BF16)),
        grid=(bsz, seq // tile),
        in_specs=[
            pl.BlockSpec((None, tile, d), lambda bi, ti: (bi, ti, 0)),
            pl.BlockSpec((None, tile, 1), lambda bi, ti: (bi, ti, 0)),
            _resident((1, 2 * rope_d)),
            _resident(w_down.shape), _resident(w_kpe.shape),
            _resident((1, MLA_Q_RANK)), _resident((1, MLA_KV_RANK)),
            _resident(w_uq_all.shape), _resident(w_ukv_all.shape),
        ],
        out_specs=(pl.BlockSpec((None, heads, dqk, tile), lambda bi, ti: (bi, 0, 0, ti)),
                   pl.BlockSpec((None, heads, tile, dqk), lambda bi, ti: (bi, 0, ti, 0)),
                   pl.BlockSpec((None, heads, None, vd + ATTN_VT_PAD, tile),
                                lambda bi, ti: (bi, 0, ti // per_tk, 0, ti % per_tk))),
        compiler_params=pltpu.CompilerParams(
            dimension_semantics=("parallel", "parallel"), vmem_limit_bytes=V7X_VMEM_LIMIT),
        name="mla_proj",
    )(h, pos, inv, bf(w_down), bf(w_kpe), q_norm_g.reshape(1, -1), kv_norm_g.reshape(1, -1),
      bf(w_uq_all), bf(w_ukv_all))

    attn = pl.pallas_call(
        _attn_kernel,
        out_shape=jax.ShapeDtypeStruct((bsz, seq, heads * vd), BF16),
        grid=(bsz, heads, seq // ATTN_TQ),
        in_specs=[
            pl.BlockSpec((None, None, dqk, ATTN_TQ), lambda bi, hi, qi: (bi, hi, 0, qi)),
            pl.BlockSpec((None, None, seq, dqk), lambda bi, hi, qi: (bi, hi, 0, 0)),
            pl.BlockSpec((None, None, seq // ATTN_TK, vd + ATTN_VT_PAD, ATTN_TK),
                         lambda bi, hi, qi: (bi, hi, 0, 0, 0)),
        ],
        out_specs=pl.BlockSpec((None, ATTN_TQ, vd), lambda bi, hi, qi: (bi, qi, hi)),
        scratch_shapes=[pltpu.VMEM((ATTN_SLOTS, ATTN_TQ // ATTN_SUB, ATTN_TK, ATTN_SUB),
                                   F32)],
        compiler_params=pltpu.CompilerParams(
            dimension_semantics=("parallel", "parallel", "arbitrary"),
            vmem_limit_bytes=V7X_VMEM_LIMIT),
        name="mla_attention",
    )(q, k, v)

    return attn.reshape(bsz * seq, heads * vd)


def kernel(x, positions, ret_w_in, ret_gn_g, ret_w_o, lru_w_in, lru_conv_w, lru_conv_b, lru_w_a,
           lru_b_a, lru_w_x, lru_b_x, lru_lam, lru_w_o, mla_w_in, mla_q_norm, mla_kv_norm,
           mla_w_uq, mla_w_ukv, mla_w_o, ln_g, ln_b, mlp_w1, mlp_w2):
    bsz, seq, d = x.shape
    pos = positions.astype(F32).reshape(bsz, seq, 1)
    ret_w_in_b, ret_w_o_b = ret_w_in.astype(BF16), ret_w_o.astype(BF16)
    mlp_w1_b, mlp_w2_b = mlp_w1.astype(BF16), mlp_w2.astype(BF16)
    h = x
    for i in range(DEPTH):
        kind, j = i % N_MIXERS, i // N_MIXERS
        pre = None
        if kind == 0:
            h = _retention_layer(h, pos, ret_w_in_b, ret_gn_g[j], ret_w_o_b, j,
                                 ln_g[i, 0], ln_b[i, 0])
        elif kind == 1:
            h = _rglru_layer(h, lru_w_in[j], lru_conv_w[j], lru_conv_b[j], lru_w_a[j],
                             lru_b_a[j], lru_w_x[j], lru_b_x[j], lru_lam[j], lru_w_o[j],
                             ln_g[i, 0], ln_b[i, 0], mlp_w1_b, mlp_w2_b, i,
                             ln_g[i, 1], ln_b[i, 1])
            continue
        else:
            y = _mla_layer(h, pos, mla_w_in[j], mla_q_norm[j], mla_kv_norm[j], mla_w_uq[j],
                           mla_w_ukv[j])
            pre = (y, mla_w_o[j].astype(BF16), ln_g[i, 0], ln_b[i, 0])
        h = _mlp_layer(h.reshape(bsz * seq, d), mlp_w1_b, mlp_w2_b, i,
                       ln_g[i, 1], ln_b[i, 1], pre=pre).reshape(bsz, seq, d)
    return h
```

```python
import functools
import math

import jax
import jax.numpy as jnp
from jax import lax
from jax.experimental import pallas as pl
from jax.experimental.pallas import tpu as pltpu

F32 = jnp.float32
BF16 = jnp.bfloat16

DEPTH = 4
N_MIXERS = 3
RET_HEADS = 4
GN_EPS = 1e-5
LRU_BLOCKS = 10
CONV_WIDTH = 4
LRU_C = 8.0
MLA_HEADS = 8
MLA_NOPE = 128
MLA_ROPE = 64
MLA_V = 128
MLA_Q_RANK = 384
MLA_KV_RANK = 256
ROPE_BASE = 10000.0
LN_EPS = 1e-5
RMS_EPS = 1e-6
ALPHA = (2.0 * DEPTH) ** 0.25
LOG2_E = math.log2(math.e)

RET_CHUNK = 256
RET_TILE = 512
LRU_TILE = 512
LRU_SUB = 256
MLP_TILE = 1024
MLP_SUB = 512
MLA_PROJ_TILE = 512
MLA_PROJ_SUB = 256
ATTN_TQ = 1024
ATTN_SUB = 256
ATTN_TK = 512
ATTN_SLOTS = 2
ATTN_VT_PAD = 16
FF_CHUNK = 1024

V7X_VMEM_LIMIT = 56 * 1024 * 1024


def _resident(shape, layer=None):
    nd = len(shape)
    if layer is None:
        return pl.BlockSpec(shape, lambda *_: (0,) * nd, pipeline_mode=pl.Buffered(1))
    return pl.BlockSpec((None,) + tuple(shape), lambda *_: (layer,) + (0,) * nd,
                        pipeline_mode=pl.Buffered(1))


def _dot(a, b):
    return jnp.dot(a, b, preferred_element_type=F32)


def _layer_norm(z, g, b):
    mu = jnp.mean(z, axis=-1, keepdims=True)
    zc = z - mu
    var = jnp.mean(zc * zc, axis=-1, keepdims=True)
    return zc * lax.rsqrt(var + LN_EPS) * g + b


def _rms_norm(z, g):
    return z * lax.rsqrt(jnp.mean(z * z, axis=-1, keepdims=True) + RMS_EPS) * g


def _sigmoid(z):
    return jax.nn.sigmoid(z)


def _mlp_residual(x, w1_ref, w2_ref):
    xb = x.astype(BF16)
    acc = ALPHA * x
    for c in range(w1_ref.shape[1] // FF_CHUNK):
        a = _dot(xb, w1_ref[:, c * FF_CHUNK:(c + 1) * FF_CHUNK])
        a = jnp.square(jnp.maximum(a, 0.0)).astype(BF16)
        acc = acc + _dot(a, w2_ref[c * FF_CHUNK:(c + 1) * FF_CHUNK, :])
    return acc


def _mlp_kernel(*refs, pre_proj, sub):
    if pre_proj:
        h_ref, y_ref, w_o_ref, g0_ref, b0_ref, w1_ref, w2_ref, g_ref, b_ref, o_ref = refs
    else:
        h_ref, w1_ref, w2_ref, g_ref, b_ref, o_ref = refs
    parts = range(h_ref.shape[0] // sub)
    accs = []
    for p in parts:
        x = h_ref[p * sub:(p + 1) * sub, :]
        if pre_proj:
            mix = ALPHA * x + _dot(y_ref[p * sub:(p + 1) * sub, :], w_o_ref[...])
            x = _layer_norm(mix, g0_ref[...], b0_ref[...])
        accs.append(_mlp_residual(x, w1_ref, w2_ref))
    for p in parts:
        o_ref[p * sub:(p + 1) * sub, :] = _layer_norm(accs[p], g_ref[...], b_ref[...])


def _mlp_layer(h, w1_stack, w2_stack, layer, g, b, pre=None):
    m, d = h.shape
    d_ff = w1_stack.shape[2]
    tile, sub = (MLP_TILE, MLP_SUB) if pre is None else (MLP_SUB, MLP_SUB)
    rows = lambda width: pl.BlockSpec((tile, width), lambda i: (i, 0))
    operands = [h]
    in_specs = [rows(d)]
    if pre is not None:
        y, w_o, g0, b0 = pre
        operands += [y, w_o, g0.reshape(1, d), b0.reshape(1, d)]
        in_specs += [rows(y.shape[1]), _resident(w_o.shape), _resident((1, d)), _resident((1, d))]
    operands += [w1_stack, w2_stack, g.reshape(1, d), b.reshape(1, d)]
    in_specs += [_resident((d, d_ff), layer), _resident((d_ff, d), layer),
                 _resident((1, d)), _resident((1, d))]
    return pl.pallas_call(
        functools.partial(_mlp_kernel, pre_proj=pre is not None, sub=sub),
        out_shape=jax.ShapeDtypeStruct((m, d), F32),
        grid=(m // tile,),
        in_specs=in_specs,
        out_specs=rows(d),
        compiler_params=pltpu.CompilerParams(
            dimension_semantics=("parallel",), vmem_limit_bytes=V7X_VMEM_LIMIT),
        name="mlp_ln",
    )(*operands)


def _ret_kernel(lg_ref, x_ref, pos_ref, inv_ref, w_in_ref, gn_ref, w_o_ref, g_ref, b_ref,
                o_ref, state_ref):
    heads, dk, dv = state_ref.shape
    half = dk // 2
    chunk = RET_CHUNK
    cs = range(x_ref.shape[0] // chunk)
    hs = range(heads)

    @pl.when(pl.program_id(1) == 0)
    def _():
        state_ref[...] = jnp.zeros_like(state_ref)

    def rows(ref, c):
        return ref[c * chunk:(c + 1) * chunk, :]

    k_off = heads * dk
    v_off = 2 * heads * dk
    g_off = v_off + heads * dv
    xb = [rows(x_ref, c).astype(BF16) for c in cs]
    q_raw = [[_dot(xb[c], w_in_ref[:, h * dk:(h + 1) * dk]) for h in hs] for c in cs]
    k_raw = [[_dot(xb[c], w_in_ref[:, k_off + h * dk:k_off + (h + 1) * dk]) for h in hs]
             for c in cs]
    vb = [[_dot(xb[c], w_in_ref[:, v_off + h * dv:v_off + (h + 1) * dv]).astype(BF16)
           for h in hs] for c in cs]
    gates = [[_dot(xb[c], w_in_ref[:, g_off + h * dv:g_off + (h + 1) * dv]) for h in hs]
             for c in cs]

    lgs = [lg_ref[h] for h in hs]
    row = lax.broadcasted_iota(jnp.int32, (chunk, chunk), 0)
    col = lax.broadcasted_iota(jnp.int32, (chunk, chunk), 1)
    diff = jnp.maximum(row - col, 0).astype(F32)
    idx = lax.broadcasted_iota(jnp.int32, (chunk, 1), 0).astype(F32)
    intra = [jnp.where(row >= col, jnp.exp(lgs[h] * diff), 0.0) for h in hs]
    q_dec = [jnp.exp(lgs[h] * (idx + 1.0)) for h in hs]
    idx_t = lax.broadcasted_iota(jnp.int32, (1, chunk), 1).astype(F32)
    k_dec_t = [jnp.exp(lgs[h] * (chunk - 1.0 - idx_t)) for h in hs]
    chunk_dec = [jnp.exp(lgs[h] * chunk) for h in hs]

    state = [state_ref[h] for h in hs]
    outs = []
    for c in cs:
        ang = rows(pos_ref, c) * inv_ref[...]
        cos = jnp.cos(ang)
        sin = jnp.sin(ang)

        def rope(t):
            t1, t2 = t[:, :half], t[:, half:]
            return jnp.concatenate([t1 * cos - t2 * sin, t1 * sin + t2 * cos], axis=-1)

        q = [rope(t) for t in q_raw[c]]
        k_t = [(rope(t) * (dk ** -0.5)).T for t in k_raw[c]]
        old = [state[h].astype(BF16) for h in hs]

        def chunk_out(h, scores_h):
            return _dot(jnp.concatenate([(scores_h * intra[h]).astype(BF16),
                                         (q[h] * q_dec[h]).astype(BF16)], axis=1),
                        jnp.concatenate([vb[c][h], old[h]], axis=0))

        out_c, prev_scores = [], None
        for h in hs:
            scores_h = _dot(q[h].astype(BF16), k_t[h].astype(BF16))
            if prev_scores is not None:
                out_c.append(chunk_out(h - 1, prev_scores))
            state[h] = state[h] * chunk_dec[h] + _dot((k_t[h] * k_dec_t[h]).astype(BF16),
                                                      vb[c][h])
            prev_scores = scores_h
        out_c.append(chunk_out(heads - 1, prev_scores))
        outs.append(out_c)
    for h in hs:
        state_ref[h] = state[h]

    for c in cs:
        mix = ALPHA * rows(x_ref, c)
        for h in hs:
            o = outs[c][h]
            mu = jnp.mean(o, axis=-1, keepdims=True)
            oc = o - mu
            var = jnp.mean(oc * oc, axis=-1, keepdims=True)
            y = oc * lax.rsqrt(var + GN_EPS) * gn_ref[:, h * dv:(h + 1) * dv]
            y = gates[c][h] * _sigmoid(gates[c][h]) * y
            mix = mix + _dot(y.astype(BF16), w_o_ref[h * dv:(h + 1) * dv, :])
        o_ref[c * chunk:(c + 1) * chunk, :] = _layer_norm(mix, g_ref[...], b_ref[...])


def _retention_layer(h, pos, w_in_stack, gn_g, w_o_stack, layer, g, b):
    bsz, seq, d = h.shape
    heads = RET_HEADS
    dk = d // heads
    dv = 2 * dk
    n_in = w_in_stack.shape[2]
    log_gamma = jnp.log1p(-jnp.exp2(-5.0 - jnp.arange(heads, dtype=F32)))
    half = dk // 2
    inv = (ROPE_BASE ** (-jnp.arange(half, dtype=F32) / half)).reshape(1, half)
    tile = RET_TILE
    grid_spec = pltpu.PrefetchScalarGridSpec(
        num_scalar_prefetch=1,
        grid=(bsz, seq // tile),
        in_specs=[
            pl.BlockSpec((None, tile, d), lambda bi, ci, lg: (bi, ci, 0)),
            pl.BlockSpec((None, tile, 1), lambda bi, ci, lg: (bi, ci, 0)),
            _resident((1, half)),
            _resident((d, n_in), layer),
            _resident((1, heads * dv)),
            _resident((heads * dv, d), layer),
            _resident((1, d)),
            _resident((1, d)),
        ],
        out_specs=pl.BlockSpec((None, tile, d), lambda bi, ci, lg: (bi, ci, 0)),
        scratch_shapes=[pltpu.VMEM((heads, dk, dv), F32)],
    )
    return pl.pallas_call(
        _ret_kernel,
        out_shape=jax.ShapeDtypeStruct((bsz, seq, d), F32),
        grid_spec=grid_spec,
        compiler_params=pltpu.CompilerParams(
            dimension_semantics=("parallel", "arbitrary"), vmem_limit_bytes=V7X_VMEM_LIMIT),
        name="retention_layer",
    )(log_gamma, h, pos, inv, w_in_stack, gn_g.reshape(1, -1), w_o_stack,
      g.reshape(1, d), b.reshape(1, d))


def _gelu_tanh(z):
    c = math.sqrt(2.0 / math.pi)
    return z * _sigmoid((2.0 * c) * (z + 0.044715 * (z * z * z)))


def _softplus(z):
    return jnp.maximum(z, 0.0) + jnp.log1p(jnp.exp(-jnp.abs(z)))


def _lru_kernel(x_ref, w_in_ref, conv_w_ref, conv_b_ref, w_a_ref, b_a_ref, w_x_ref, b_x_ref,
                lam_ref, w_o_ref, g_ref, b_ref, w1_ref, w2_ref, g2_ref, b2_ref,
                o_ref, tail_ref, h_ref, pend_ref, *, steps_per_seq):
    sub = LRU_SUB
    ts = range(x_ref.shape[0] // sub)
    width = conv_b_ref.shape[1]
    nblk, bdim, _ = w_a_ref.shape
    slab = 8
    blk = pl.program_id(0)

    @pl.when(blk % steps_per_seq == 0)
    def _():
        tail_ref[...] = jnp.zeros_like(tail_ref)
        h_ref[...] = jnp.zeros_like(h_ref)

    @pl.when(blk == 0)
    def _():
        pend_ref[...] = jnp.zeros_like(pend_ref)

    pending = [pend_ref[t * sub:(t + 1) * sub, :] for t in ts]

    xb = [x_ref[t * sub:(t + 1) * sub, :].astype(BF16) for t in ts]
    gate_raw = [_dot(xb[t], w_in_ref[:, :width]) for t in ts]
    rec = [_dot(xb[t], w_in_ref[:, width:]) for t in ts]

    row8 = lax.broadcasted_iota(jnp.int32, (slab, width), 0)
    tail = tail_ref[...]
    us = []
    for t in ts:
        u = rec[t] * conv_w_ref[CONV_WIDTH - 1:CONV_WIDTH, :] + conv_b_ref[...]
        for s in range(1, CONV_WIDTH):
            shifted = pltpu.roll(rec[t], s, 0)
            head = jnp.where(row8 < s, pltpu.roll(tail, s, 0), shifted[:slab])
            shifted = jnp.concatenate([head, shifted[slab:]], axis=0)
            u = u + shifted * conv_w_ref[CONV_WIDTH - 1 - s:CONV_WIDTH - s, :]
        tail = rec[t][sub - slab:, :]
        us.append(u)
    tail_ref[...] = tail

    def pending_mlp(t):
        o_ref[t * sub:(t + 1) * sub, :] = _layer_norm(
            _mlp_residual(pending[t], w1_ref, w2_ref), g2_ref[...], b2_ref[...])

    for t in ts[:len(ts) // 2]:
        pending_mlp(t)
    pre_a, pre_x = [], []
    for t in ts:
        ubs = [us[t][:, n * bdim:(n + 1) * bdim].astype(BF16) for n in range(nblk)]
        pre_a.append(jnp.concatenate([_dot(ubs[n], w_a_ref[n]) for n in range(nblk)], axis=-1))
        pre_x.append(jnp.concatenate([_dot(ubs[n], w_x_ref[n]) for n in range(nblk)], axis=-1))
    for t in ts[len(ts) // 2:]:
        pending_mlp(t)

    neg_softplus_lam = -_softplus(-lam_ref[...])
    carry = h_ref[...]
    ys = []
    for t in ts:
        r = _sigmoid(pre_a[t] + b_a_ref[...])
        i_gate = _sigmoid(pre_x[t] + b_x_ref[...])
        log_a = LRU_C * r * neg_softplus_lam
        a = jnp.exp(log_a)
        b_in = jnp.sqrt(1.0 - jnp.exp(2.0 * log_a)) * (i_gate * us[t])
        a3 = a.reshape(sub // slab, slab, width)
        b3 = b_in.reshape(sub // slab, slab, width)
        rows = lax.broadcasted_iota(jnp.int32, a3.shape, 1)
        step = 1
        while step < slab:
            a_sh = pltpu.roll(a3, step, 1)
            b_sh = pltpu.roll(b3, step, 1)
            live = rows >= step
            b3 = jnp.where(live, a3 * b_sh + b3, b3)
            a3 = jnp.where(live, a3 * a_sh, a3)
            step *= 2
        slabs = []
        for i in range(sub // slab):
            hs_i = a3[i] * carry + b3[i]
            carry = hs_i[slab - 1:slab, :]
            slabs.append(hs_i)
        hs = jnp.concatenate(slabs, axis=0)
        ys.append((hs * _gelu_tanh(gate_raw[t])).astype(BF16))
    h_ref[...] = carry

    for t in ts:
        mix = ALPHA * x_ref[t * sub:(t + 1) * sub, :] + _dot(ys[t], w_o_ref[...])
        pend_ref[t * sub:(t + 1) * sub, :] = _layer_norm(mix, g_ref[...], b_ref[...])


def _rglru_layer(h, w_in, conv_w, conv_b, w_a, b_a, w_x, b_x, lam, w_o, g, b,
                 w1_stack, w2_stack, layer, g2, b2):
    bsz, seq, d = h.shape
    d_ff = w1_stack.shape[2]
    width = conv_b.shape[0]
    tile = LRU_TILE
    row = lambda t: t.reshape(1, -1)
    per_seq = seq // tile
    blocks = bsz * per_seq

    def mixer_block(s):
        s = jnp.minimum(s, blocks - 1)
        return s // per_seq, s % per_seq, 0

    def mlp_block(s):
        s = jnp.maximum(s - 1, 0)
        return s // per_seq, s % per_seq, 0

    return pl.pallas_call(
        functools.partial(_lru_kernel, steps_per_seq=per_seq),
        out_shape=jax.ShapeDtypeStruct((bsz, seq, d), F32),
        grid=(blocks + 1,),
        in_specs=[
            pl.BlockSpec((None, tile, d), mixer_block),
            _resident((d, 2 * width)),
            _resident((CONV_WIDTH, width)),
            _resident((1, width)),
            _resident(w_a.shape),
            _resident((1, width)),
            _resident(w_x.shape),
            _resident((1, width)),
            _resident((1, width)),
            _resident((width, d)),
            _resident((1, d)),
            _resident((1, d)),
            _resident((d, d_ff), layer),
            _resident((d_ff, d), layer),
            _resident((1, d)),
            _resident((1, d)),
        ],
        out_specs=pl.BlockSpec((None, tile, d), mlp_block),
        scratch_shapes=[pltpu.VMEM((8, width), F32), pltpu.VMEM((1, width), F32),
                        pltpu.VMEM((tile, d), F32)],
        compiler_params=pltpu.CompilerParams(
            dimension_semantics=("arbitrary",), vmem_limit_bytes=V7X_VMEM_LIMIT),
        name="rglru_layer",
    )(h, w_in.astype(BF16), conv_w, row(conv_b), w_a.astype(BF16), row(b_a),
      w_x.astype(BF16), row(b_x), row(lam), w_o.astype(BF16), row(g), row(b),
      w1_stack, w2_stack, row(g2), row(b2))


def _mla_proj_kernel(x_ref, pos_ref, inv_ref, w_down_ref, w_kpe_ref, qn_g_ref, kvn_g_ref,
                     w_uq_ref, w_ukv_ref, q_ref, k_ref, v_ref):
    heads = q_ref.shape[0]
    nope, lanes = MLA_NOPE, 2 * MLA_ROPE
    q_rank = qn_g_ref.shape[1]
    scale = (MLA_NOPE + MLA_ROPE) ** -0.5 * LOG2_E
    sub = MLA_PROJ_SUB
    ts = range(x_ref.shape[0] // sub)
    pairs = heads // 2
    pe_off = heads * nope
    rot_off = pe_off + pairs * lanes
    v_off = heads * nope
    pad_rows = v_ref.shape[1] - MLA_V
    ones_row = jnp.where(lax.broadcasted_iota(jnp.int32, (pad_rows, v_ref.shape[2]), 0) == 0,
                         1.0, 0.0).astype(BF16)

    xb = [x_ref[t * sub:(t + 1) * sub, :].astype(BF16) for t in ts]
    down = [_dot(xb[t], w_down_ref[...]) for t in ts]
    kpe = [_dot(xb[t], w_kpe_ref[...]) for t in ts]
    c_q = [_rms_norm(down[t][:, :q_rank], qn_g_ref[...]).astype(BF16) for t in ts]
    c_kv = [_rms_norm(down[t][:, q_rank:], kvn_g_ref[...]).astype(BF16) for t in ts]
    uq = [_dot(c_q[t], w_uq_ref[...]) for t in ts]
    ukv = [_dot(c_kv[t], w_ukv_ref[...]) for t in ts]

    for t in ts:
        rows = slice(t * sub, (t + 1) * sub)
        ang = pos_ref[rows, :] * inv_ref[...]
        cos = jnp.cos(ang)
        sin = jnp.sin(ang)
        k_pe = [(kpe[t][:, par * lanes:(par + 1) * lanes] * cos
                 + kpe[t][:, (2 + par) * lanes:(3 + par) * lanes] * sin).astype(BF16)
                for par in (0, 1)]
        q_pe_t = [((uq[t][:, pe_off + g * lanes:pe_off + (g + 1) * lanes] * cos
                    + uq[t][:, rot_off + g * lanes:rot_off + (g + 1) * lanes] * sin) * scale
                   ).T.astype(BF16) for g in range(pairs)]
        for h in range(heads):
            q_ref[h, :nope, rows] = (uq[t][:, h * nope:(h + 1) * nope] * scale).T.astype(BF16)
            q_ref[h, nope:, rows] = q_pe_t[h // 2]
            k_ref[h, rows, :nope] = ukv[t][:, h * nope:(h + 1) * nope].astype(BF16)
            k_ref[h, rows, nope:] = k_pe[h % 2]
            v_ref[h, :MLA_V, rows] = (
                ukv[t][:, v_off + h * MLA_V:v_off + (h + 1) * MLA_V].T.astype(BF16))
    for h in range(heads):
        v_ref[h, MLA_V:, :] = ones_row


def _attn_kernel(q_ref, k_ref, vt_ref, o_ref, s_ref):
    i = pl.program_id(2)
    tq, dv = o_ref.shape
    kb_keys = vt_ref.shape[2]
    sub = tk = ATTN_SUB
    nsub = tq // sub
    kpq = tq // tk
    per_kb = kb_keys // tk
    qs = [q_ref[:, s * sub:(s + 1) * sub] for s in range(nsub)]

    def diag_vt(c):
        return vt_ref[i * (kpq // per_kb) + c // per_kb][:, (c % per_kb) * tk:
                                                         (c % per_kb + 1) * tk]

    def scores(j, slot):
        kb = k_ref[pl.ds(pl.multiple_of(j * kb_keys, kb_keys), kb_keys), :]
        maxima = []
        for s in range(nsub):
            st = _dot(kb, qs[s])
            s_ref[slot, s] = st
            maxima.append(jnp.max(st, axis=0, keepdims=True))
        return tuple(maxima)

    def absorb(j, slot, stats, maxima):
        vt = vt_ref[j]
        ps, scaled = [], []
        for s, ((m_prev, acc), m_blk) in enumerate(zip(stats, maxima)):
            m_new = jnp.maximum(m_prev, m_blk)
            ps.append(jnp.exp2(s_ref[slot, s] - m_new).astype(BF16))
            scaled.append((m_new, jnp.exp2(m_prev - m_new) * acc))
        return tuple((m, a + _dot(vt, p)) for (m, a), p in zip(scaled, ps))

    tri = (lax.broadcasted_iota(jnp.int32, (tk, sub), 0)
           <= lax.broadcasted_iota(jnp.int32, (tk, sub), 1))
    diag_scores = {}
    for c in range(kpq):
        kb = k_ref[pl.ds(pl.multiple_of(i * tq + c * tk, tk), tk), :]
        for s in range(c, nsub):
            st = _dot(kb, qs[s])
            diag_scores[c, s] = jnp.where(tri, st, -jnp.inf) if c == s else st
    first_maxima = scores(0, 0)
    diag_p = {}
    diag_m = []
    for s in range(nsub):
        m0 = jnp.max(diag_scores[0, s], axis=0, keepdims=True)
        for c in range(1, s + 1):
            m0 = jnp.maximum(m0, jnp.max(diag_scores[c, s], axis=0, keepdims=True))
        for c in range(s + 1):
            diag_p[c, s] = jnp.exp2(diag_scores[c, s] - m0).astype(BF16)
        diag_m.append(m0)
    stats = []
    for s in range(nsub):
        acc = _dot(diag_vt(0), diag_p[0, s])
        for c in range(1, s + 1):
            acc = acc + _dot(diag_vt(c), diag_p[c, s])
        stats.append((diag_m[s], acc))

    nslot = s_ref.shape[0]
    bpq = tq // kb_keys

    def body(t, carry):
        stats, maxima = carry
        for c in range(nslot):
            cur = t * nslot + c
            nxt = jnp.minimum(cur + 1, i * bpq - 1)
            kb = k_ref[pl.ds(pl.multiple_of(nxt * kb_keys, kb_keys), kb_keys), :]
            vt = vt_ref[cur]
            new_stats, maxima_next = [], []
            for s in range(nsub):
                st = _dot(kb, qs[s])
                s_ref[(c + 1) % nslot, s] = st
                maxima_next.append(jnp.max(st, axis=0, keepdims=True))
                m_prev, acc = stats[s]
                m_new = jnp.maximum(m_prev, maxima[s])
                p = jnp.exp2(s_ref[c, s] - m_new).astype(BF16)
                new_stats.append((m_new, jnp.exp2(m_prev - m_new) * acc + _dot(vt, p)))
            stats, maxima = tuple(new_stats), tuple(maxima_next)
        return stats, maxima

    stats, _ = lax.fori_loop(0, i * (bpq // nslot), body, (tuple(stats), first_maxima))
    for s in range(nsub):
        _, acc = stats[s]
        o_ref[s * sub:(s + 1) * sub, :] = (acc[:dv] / acc[dv:dv + 1]).T.astype(o_ref.dtype)


def _rot_half_cols(w):
    half = w.shape[-1] // 2
    return jnp.concatenate([-w[..., half:], w[..., :half]], axis=-1)


def _mla_layer(h, pos, w_in, q_norm_g, kv_norm_g, w_uq, w_ukv):
    bsz, seq, d = h.shape
    heads, nope, rope_d, vd = MLA_HEADS, MLA_NOPE, MLA_ROPE, MLA_V
    dqk = nope + 2 * rope_d
    half = rope_d // 2
    inv = ROPE_BASE ** (-jnp.arange(half, dtype=F32) / half)
    inv = jnp.tile(inv, 4).reshape(1, 2 * rope_d)

    w_down = w_in[:, :MLA_Q_RANK + MLA_KV_RANK]
    w_pe = w_in[:, MLA_Q_RANK + MLA_KV_RANK:]
    zero = jnp.zeros_like(w_pe)
    w_pe_rot = _rot_half_cols(w_pe)
    w_kpe = jnp.concatenate([w_pe, zero, zero, w_pe, w_pe_rot, zero, zero, w_pe_rot], axis=1)
    w_uq3 = w_uq.reshape(MLA_Q_RANK, heads, nope + rope_d)
    w_qp = w_uq3[:, :, nope:]
    w_uq_all = jnp.concatenate(
        [w_uq3[:, :, :nope].reshape(MLA_Q_RANK, heads * nope),
         w_qp.reshape(MLA_Q_RANK, heads * rope_d),
         _rot_half_cols(w_qp).reshape(MLA_Q_RANK, heads * rope_d)], axis=1)
    w_ukv3 = w_ukv.reshape(MLA_KV_RANK, heads, nope + vd)
    w_ukv_all = jnp.concatenate(
        [w_ukv3[:, :, :nope].reshape(MLA_KV_RANK, heads * nope),
         w_ukv3[:, :, nope:].reshape(MLA_KV_RANK, heads * vd)], axis=1)

    tile = MLA_PROJ_TILE
    per_tk = ATTN_TK // tile
    bf = lambda t: t.astype(BF16)
    q, k, v = pl.pallas_call(
        _mla_proj_kernel,
        out_shape=(jax.ShapeDtypeStruct((bsz, heads, dqk, seq), BF16),
                   jax.ShapeDtypeStruct((bsz, heads, seq, dqk), BF16),
                   jax.ShapeDtypeStruct((bsz, heads, seq // ATTN_TK, vd + ATTN_VT_PAD, ATTN_TK), BF16)),
        grid=(bsz, seq // tile),
        in_specs=[
            pl.BlockSpec((None, tile, d), lambda bi, ti: (bi, ti, 0)),
            pl.BlockSpec((None, tile, 1), lambda bi, ti: (bi, ti, 0)),
            _resident((1, 2 * rope_d)),
            _resident(w_down.shape), _resident(w_kpe.shape),
            _resident((1, MLA_Q_RANK)), _resident((1, MLA_KV_RANK)),
            _resident(w_uq_all.shape), _resident(w_ukv_all.shape),
        ],
        out_specs=(pl.BlockSpec((None, heads, dqk, tile), lambda bi, ti: (bi, 0, 0, ti)),
                   pl.BlockSpec((None, heads, tile, dqk), lambda bi, ti: (bi, 0, ti, 0)),
                   pl.BlockSpec((None, heads, None, vd + ATTN_VT_PAD, tile),
                                lambda bi, ti: (bi, 0, ti // per_tk, 0, ti % per_tk))),
        compiler_params=pltpu.CompilerParams(
            dimension_semantics=("parallel", "parallel"), vmem_limit_bytes=V7X_VMEM_LIMIT),
        name="mla_proj",
    )(h, pos, inv, bf(w_down), bf(w_kpe), q_norm_g.reshape(1, -1), kv_norm_g.reshape(1, -1),
      bf(w_uq_all), bf(w_ukv_all))

    attn = pl.pallas_call(
        _attn_kernel,
        out_shape=jax.ShapeDtypeStruct((bsz, seq, heads * vd), BF16),
        grid=(bsz, heads, seq // ATTN_TQ),
        in_specs=[
            pl.BlockSpec((None, None, dqk, ATTN_TQ), lambda bi, hi, qi: (bi, hi, 0, qi)),
            pl.BlockSpec((None, None, seq, dqk), lambda bi, hi, qi: (bi, hi, 0, 0)),
            pl.BlockSpec((None, None, seq // ATTN_TK, vd + ATTN_VT_PAD, ATTN_TK),
                         lambda bi, hi, qi: (bi, hi, 0, 0, 0)),
        ],
        out_specs=pl.BlockSpec((None, ATTN_TQ, vd), lambda bi, hi, qi: (bi, qi, hi)),
        scratch_shapes=[pltpu.VMEM((ATTN_SLOTS, ATTN_TQ // ATTN_SUB, ATTN_TK, ATTN_SUB),
                                   F32)],
        compiler_params=pltpu.CompilerParams(
            dimension_semantics=("parallel", "parallel", "arbitrary"),
            vmem_limit_bytes=V7X_VMEM_LIMIT),
        name="mla_attention",
    )(q, k, v)

    return attn.reshape(bsz * seq, heads * vd)


def kernel(x, positions, ret_w_in, ret_gn_g, ret_w_o, lru_w_in, lru_conv_w, lru_conv_b, lru_w_a,
           lru_b_a, lru_w_x, lru_b_x, lru_lam, lru_w_o, mla_w_in, mla_q_norm, mla_kv_norm,
           mla_w_uq, mla_w_ukv, mla_w_o, ln_g, ln_b, mlp_w1, mlp_w2):
    bsz, seq, d = x.shape
    pos = positions.astype(F32).reshape(bsz, seq, 1)
    ret_w_in_b, ret_w_o_b = ret_w_in.astype(BF16), ret_w_o.astype(BF16)
    mlp_w1_b, mlp_w2_b = mlp_w1.astype(BF16), mlp_w2.astype(BF16)
    h = x
    for i in range(DEPTH):
        kind, j = i % N_MIXERS, i // N_MIXERS
        pre = None
        if kind == 0:
            h = _retention_layer(h, pos, ret_w_in_b, ret_gn_g[j], ret_w_o_b, j,
                                 ln_g[i, 0], ln_b[i, 0])
        elif kind == 1:
            h = _rglru_layer(h, lru_w_in[j], lru_conv_w[j], lru_conv_b[j], lru_w_a[j],
                             lru_b_a[j], lru_w_x[j], lru_b_x[j], lru_lam[j], lru_w_o[j],
                             ln_g[i, 0], ln_b[i, 0], mlp_w1_b, mlp_w2_b, i,
                             ln_g[i, 1], ln_b[i, 1])
            continue
        else:
            y = _mla_layer(h, pos, mla_w_in[j], mla_q_norm[j], mla_kv_norm[j], mla_w_uq[j],
                           mla_w_ukv[j])
            pre = (y, mla_w_o[j].astype(BF16), ln_g[i, 0], ln_b[i, 0])
        h = _mlp_layer(h.reshape(bsz * seq, d), mlp_w1_b, mlp_w2_b, i,
                       ln_g[i, 1], ln_b[i, 1], pre=pre).reshape(bsz, seq, d)
    return h
```

```python
import functools
import math

import jax
import jax.numpy as jnp
from jax import lax
from jax.experimental import pallas as pl
from jax.experimental.pallas import tpu as pltpu

F32 = jnp.float32
BF16 = jnp.bfloat16

DEPTH = 4
N_MIXERS = 3
RET_HEADS = 4
GN_EPS = 1e-5
LRU_BLOCKS = 10
CONV_WIDTH = 4
LRU_C = 8.0
MLA_HEADS = 8
MLA_NOPE = 128
MLA_ROPE = 64
MLA_V = 128
MLA_Q_RANK = 384
MLA_KV_RANK = 256
ROPE_BASE = 10000.0
LN_EPS = 1e-5
RMS_EPS = 1e-6
ALPHA = (2.0 * DEPTH) ** 0.25
LOG2_E = math.log2(math.e)

RET_CHUNK = 256
RET_TILE = 512
LRU_TILE = 512
LRU_SUB = 256
MLP_TILE = 1024
MLP_SUB = 256
MLP_PRE_TILE = 512
MLA_PROJ_TILE = 512
MLA_PROJ_SUB = 256
ATTN_TQ = 1024
ATTN_SUB = 256
ATTN_TK = 512
ATTN_SLOTS = 2
ATTN_VT_PAD = 16
FF_CHUNK = 1024

V7X_VMEM_LIMIT = 56 * 1024 * 1024


def _resident(shape, layer=None):
    nd = len(shape)
    if layer is None:
        return pl.BlockSpec(shape, lambda *_: (0,) * nd, pipeline_mode=pl.Buffered(1))
    return pl.BlockSpec((None,) + tuple(shape), lambda *_: (layer,) + (0,) * nd,
                        pipeline_mode=pl.Buffered(1))


def _dot(a, b):
    return jnp.dot(a, b, preferred_element_type=F32)


def _layer_norm(z, g, b):
    mu = jnp.mean(z, axis=-1, keepdims=True)
    zc = z - mu
    var = jnp.mean(zc * zc, axis=-1, keepdims=True)
    return zc * lax.rsqrt(var + LN_EPS) * g + b


def _rms_norm(z, g):
    return z * lax.rsqrt(jnp.mean(z * z, axis=-1, keepdims=True) + RMS_EPS) * g


def _sigmoid(z):
    return jax.nn.sigmoid(z)


def _mlp_residual(x, w1_ref, w2_ref):
    xb = x.astype(BF16)
    acc = ALPHA * x
    for c in range(w1_ref.shape[1] // FF_CHUNK):
        a = _dot(xb, w1_ref[:, c * FF_CHUNK:(c + 1) * FF_CHUNK])
        a = jnp.square(jnp.maximum(a, 0.0)).astype(BF16)
        acc = acc + _dot(a, w2_ref[c * FF_CHUNK:(c + 1) * FF_CHUNK, :])
    return acc


def _mlp_kernel(*refs, pre_proj, sub):
    if pre_proj:
        h_ref, y_ref, w_o_ref, g0_ref, b0_ref, w1_ref, w2_ref, g_ref, b_ref, o_ref = refs
    else:
        h_ref, w1_ref, w2_ref, g_ref, b_ref, o_ref = refs
    parts = range(h_ref.shape[0] // sub)
    accs = []
    for p in parts:
        x = h_ref[p * sub:(p + 1) * sub, :]
        if pre_proj:
            mix = ALPHA * x + _dot(y_ref[p * sub:(p + 1) * sub, :], w_o_ref[...])
            x = _layer_norm(mix, g0_ref[...], b0_ref[...])
        accs.append(_mlp_residual(x, w1_ref, w2_ref))
    for p in parts:
        o_ref[p * sub:(p + 1) * sub, :] = _layer_norm(accs[p], g_ref[...], b_ref[...])


def _mlp_layer(h, w1_stack, w2_stack, layer, g, b, pre=None):
    m, d = h.shape
    d_ff = w1_stack.shape[2]
    tile, sub = (MLP_TILE, MLP_SUB) if pre is None else (MLP_PRE_TILE, MLP_PRE_TILE)
    rows = lambda width: pl.BlockSpec((tile, width), lambda i: (i, 0))
    operands = [h]
    in_specs = [rows(d)]
    if pre is not None:
        y, w_o, g0, b0 = pre
        operands += [y, w_o, g0.reshape(1, d), b0.reshape(1, d)]
        in_specs += [rows(y.shape[1]), _resident(w_o.shape), _resident((1, d)), _resident((1, d))]
    operands += [w1_stack, w2_stack, g.reshape(1, d), b.reshape(1, d)]
    in_specs += [_resident((d, d_ff), layer), _resident((d_ff, d), layer),
                 _resident((1, d)), _resident((1, d))]
    return pl.pallas_call(
        functools.partial(_mlp_kernel, pre_proj=pre is not None, sub=sub),
        out_shape=jax.ShapeDtypeStruct((m, d), F32),
        grid=(m // tile,),
        in_specs=in_specs,
        out_specs=rows(d),
        compiler_params=pltpu.CompilerParams(
            dimension_semantics=("parallel",), vmem_limit_bytes=V7X_VMEM_LIMIT),
        name="mlp_ln",
    )(*operands)


def _ret_kernel(lg_ref, x_ref, pos_ref, inv_ref, w_in_ref, gn_ref, w_o_ref, g_ref, b_ref,
                o_ref, state_ref):
    heads, dk, dv = state_ref.shape
    half = dk // 2
    chunk = RET_CHUNK
    cs = range(x_ref.shape[0] // chunk)
    hs = range(heads)

    @pl.when(pl.program_id(1) == 0)
    def _():
        state_ref[...] = jnp.zeros_like(state_ref)

    def rows(ref, c):
        return ref[c * chunk:(c + 1) * chunk, :]

    k_off = heads * dk
    v_off = 2 * heads * dk
    g_off = v_off + heads * dv
    xb = [rows(x_ref, c).astype(BF16) for c in cs]
    q_raw = [[_dot(xb[c], w_in_ref[:, h * dk:(h + 1) * dk]) for h in hs] for c in cs]
    k_raw = [[_dot(xb[c], w_in_ref[:, k_off + h * dk:k_off + (h + 1) * dk]) for h in hs]
             for c in cs]
    vb = [[_dot(xb[c], w_in_ref[:, v_off + h * dv:v_off + (h + 1) * dv]).astype(BF16)
           for h in hs] for c in cs]
    gates = [[_dot(xb[c], w_in_ref[:, g_off + h * dv:g_off + (h + 1) * dv]) for h in hs]
             for c in cs]

    lgs = [lg_ref[h] for h in hs]
    row = lax.broadcasted_iota(jnp.int32, (chunk, chunk), 0)
    col = lax.broadcasted_iota(jnp.int32, (chunk, chunk), 1)
    diff = jnp.maximum(row - col, 0).astype(F32)
    idx = lax.broadcasted_iota(jnp.int32, (chunk, 1), 0).astype(F32)
    intra = [jnp.where(row >= col, jnp.exp(lgs[h] * diff), 0.0) for h in hs]
    q_dec = [jnp.exp(lgs[h] * (idx + 1.0)) for h in hs]
    idx_t = lax.broadcasted_iota(jnp.int32, (1, chunk), 1).astype(F32)
    k_dec_t = [jnp.exp(lgs[h] * (chunk - 1.0 - idx_t)) for h in hs]
    chunk_dec = [jnp.exp(lgs[h] * chunk) for h in hs]

    state = [state_ref[h] for h in hs]
    outs = []
    for c in cs:
        ang = rows(pos_ref, c) * inv_ref[...]
        cos = jnp.cos(ang)
        sin = jnp.sin(ang)

        def rope(t):
            t1, t2 = t[:, :half], t[:, half:]
            return jnp.concatenate([t1 * cos - t2 * sin, t1 * sin + t2 * cos], axis=-1)

        q = [rope(t) for t in q_raw[c]]
        k_t = [(rope(t) * (dk ** -0.5)).T for t in k_raw[c]]
        old = [state[h].astype(BF16) for h in hs]

        def chunk_out(h, scores_h):
            return _dot(jnp.concatenate([(scores_h * intra[h]).astype(BF16),
                                         (q[h] * q_dec[h]).astype(BF16)], axis=1),
                        jnp.concatenate([vb[c][h], old[h]], axis=0))

        out_c, prev_scores = [], None
        for h in hs:
            scores_h = _dot(q[h].astype(BF16), k_t[h].astype(BF16))
            if prev_scores is not None:
                out_c.append(chunk_out(h - 1, prev_scores))
            state[h] = state[h] * chunk_dec[h] + _dot((k_t[h] * k_dec_t[h]).astype(BF16),
                                                      vb[c][h])
            prev_scores = scores_h
        out_c.append(chunk_out(heads - 1, prev_scores))
        outs.append(out_c)
    for h in hs:
        state_ref[h] = state[h]

    for c in cs:
        mix = ALPHA * rows(x_ref, c)
        for h in hs:
            o = outs[c][h]
            mu = jnp.mean(o, axis=-1, keepdims=True)
            oc = o - mu
            var = jnp.mean(oc * oc, axis=-1, keepdims=True)
            y = oc * lax.rsqrt(var + GN_EPS) * gn_ref[:, h * dv:(h + 1) * dv]
            y = gates[c][h] * _sigmoid(gates[c][h]) * y
            mix = mix + _dot(y.astype(BF16), w_o_ref[h * dv:(h + 1) * dv, :])
        o_ref[c * chunk:(c + 1) * chunk, :] = _layer_norm(mix, g_ref[...], b_ref[...])


def _retention_layer(h, pos, w_in_stack, gn_g, w_o_stack, layer, g, b):
    bsz, seq, d = h.shape
    heads = RET_HEADS
    dk = d // heads
    dv = 2 * dk
    n_in = w_in_stack.shape[2]
    log_gamma = jnp.log1p(-jnp.exp2(-5.0 - jnp.arange(heads, dtype=F32)))
    half = dk // 2
    inv = (ROPE_BASE ** (-jnp.arange(half, dtype=F32) / half)).reshape(1, half)
    tile = RET_TILE
    grid_spec = pltpu.PrefetchScalarGridSpec(
        num_scalar_prefetch=1,
        grid=(bsz, seq // tile),
        in_specs=[
            pl.BlockSpec((None, tile, d), lambda bi, ci, lg: (bi, ci, 0)),
            pl.BlockSpec((None, tile, 1), lambda bi, ci, lg: (bi, ci, 0)),
            _resident((1, half)),
            _resident((d, n_in), layer),
            _resident((1, heads * dv)),
            _resident((heads * dv, d), layer),
            _resident((1, d)),
            _resident((1, d)),
        ],
        out_specs=pl.BlockSpec((None, tile, d), lambda bi, ci, lg: (bi, ci, 0)),
        scratch_shapes=[pltpu.VMEM((heads, dk, dv), F32)],
    )
    return pl.pallas_call(
        _ret_kernel,
        out_shape=jax.ShapeDtypeStruct((bsz, seq, d), F32),
        grid_spec=grid_spec,
        compiler_params=pltpu.CompilerParams(
            dimension_semantics=("parallel", "arbitrary"), vmem_limit_bytes=V7X_VMEM_LIMIT),
        name="retention_layer",
    )(log_gamma, h, pos, inv, w_in_stack, gn_g.reshape(1, -1), w_o_stack,
      g.reshape(1, d), b.reshape(1, d))


def _gelu_tanh(z):
    c = math.sqrt(2.0 / math.pi)
    return z * _sigmoid((2.0 * c) * (z + 0.044715 * (z * z * z)))


def _softplus(z):
    return jnp.maximum(z, 0.0) + jnp.log1p(jnp.exp(-jnp.abs(z)))


def _lru_kernel(x_ref, w_in_ref, conv_w_ref, conv_b_ref, w_gate_ref, b_a_ref, b_x_ref,
                lam_ref, w_o_ref, g_ref, b_ref, w1_ref, w2_ref, g2_ref, b2_ref,
                o_ref, tail_ref, h_ref, pend_ref, *, steps_per_seq):
    sub = LRU_SUB
    ts = range(x_ref.shape[0] // sub)
    width = conv_b_ref.shape[1]
    npair, pair_dim, _ = w_gate_ref.shape
    bdim = pair_dim // 2
    slab = 8
    blk = pl.program_id(0)

    @pl.when(blk % steps_per_seq == 0)
    def _():
        tail_ref[...] = jnp.zeros_like(tail_ref)
        h_ref[...] = jnp.zeros_like(h_ref)

    @pl.when(blk == 0)
    def _():
        pend_ref[...] = jnp.zeros_like(pend_ref)

    pending = [pend_ref[t * sub:(t + 1) * sub, :] for t in ts]

    xb = [x_ref[t * sub:(t + 1) * sub, :].astype(BF16) for t in ts]
    rec = [_dot(xb[t], w_in_ref[:, width:]) for t in ts]
    gate_raw = [_dot(xb[t], w_in_ref[:, :width]) for t in ts]

    row8 = lax.broadcasted_iota(jnp.int32, (slab, width), 0)
    tail = tail_ref[...]
    us = []
    for t in ts:
        u = rec[t] * conv_w_ref[CONV_WIDTH - 1:CONV_WIDTH, :] + conv_b_ref[...]
        for s in range(1, CONV_WIDTH):
            shifted = pltpu.roll(rec[t], s, 0)
            head = jnp.where(row8 < s, pltpu.roll(tail, s, 0), shifted[:slab])
            shifted = jnp.concatenate([head, shifted[slab:]], axis=0)
            u = u + shifted * conv_w_ref[CONV_WIDTH - 1 - s:CONV_WIDTH - s, :]
        tail = rec[t][sub - slab:, :]
        us.append(u)
    tail_ref[...] = tail

    def pending_mlp(t):
        o_ref[t * sub:(t + 1) * sub, :] = _layer_norm(
            _mlp_residual(pending[t], w1_ref, w2_ref), g2_ref[...], b2_ref[...])

    pre_a, pre_x = [], []
    for t in ts:
        ub = us[t].astype(BF16)
        both = [_dot(ub[:, p * pair_dim:(p + 1) * pair_dim], w_gate_ref[p]) for p in range(npair)]
        pre_a.append(jnp.concatenate(
            [o[:, k * bdim:(k + 1) * bdim] for o in both for k in (0, 2)], axis=-1))
        pre_x.append(jnp.concatenate(
            [o[:, k * bdim:(k + 1) * bdim] for o in both for k in (1, 3)], axis=-1))
    for t in ts:
        pending_mlp(t)

    neg_softplus_lam = -_softplus(-lam_ref[...])
    carry = h_ref[...]
    ys = []
    for t in ts:
        r = _sigmoid(pre_a[t] + b_a_ref[...])
        i_gate = _sigmoid(pre_x[t] + b_x_ref[...])
        log_a = LRU_C * r * neg_softplus_lam
        a = jnp.exp(log_a)
        b_in = jnp.sqrt(1.0 - jnp.exp(2.0 * log_a)) * (i_gate * us[t])
        a3 = a.reshape(sub // slab, slab, width)
        b3 = b_in.reshape(sub // slab, slab, width)
        rows = lax.broadcasted_iota(jnp.int32, a3.shape, 1)
        step = 1
        while step < slab:
            a_sh = pltpu.roll(a3, step, 1)
            b_sh = pltpu.roll(b3, step, 1)
            live = rows >= step
            b3 = jnp.where(live, a3 * b_sh + b3, b3)
            a3 = jnp.where(live, a3 * a_sh, a3)
            step *= 2
        slabs = []
        for i in range(sub // slab):
            hs_i = a3[i] * carry + b3[i]
            carry = hs_i[slab - 1:slab, :]
            slabs.append(hs_i)
        hs = jnp.concatenate(slabs, axis=0)
        ys.append((hs * _gelu_tanh(gate_raw[t])).astype(BF16))
    h_ref[...] = carry

    for t in ts:
        mix = ALPHA * x_ref[t * sub:(t + 1) * sub, :] + _dot(ys[t], w_o_ref[...])
        pend_ref[t * sub:(t + 1) * sub, :] = _layer_norm(mix, g_ref[...], b_ref[...])


def _rglru_layer(h, w_in, conv_w, conv_b, w_a, b_a, w_x, b_x, lam, w_o, g, b,
                 w1_stack, w2_stack, layer, g2, b2):
    bsz, seq, d = h.shape
    d_ff = w1_stack.shape[2]
    width = conv_b.shape[0]
    tile = LRU_TILE
    row = lambda t: t.reshape(1, -1)
    per_seq = seq // tile
    blocks = bsz * per_seq
    zero = jnp.zeros_like(w_a[0::2])
    w_gate = jnp.concatenate(
        [jnp.concatenate([w_a[0::2], w_x[0::2], zero, zero], axis=-1),
         jnp.concatenate([zero, zero, w_a[1::2], w_x[1::2]], axis=-1)], axis=1)

    def mixer_block(s):
        s = jnp.minimum(s, blocks - 1)
        return s // per_seq, s % per_seq, 0

    def mlp_block(s):
        s = jnp.maximum(s - 1, 0)
        return s // per_seq, s % per_seq, 0

    return pl.pallas_call(
        functools.partial(_lru_kernel, steps_per_seq=per_seq),
        out_shape=jax.ShapeDtypeStruct((bsz, seq, d), F32),
        grid=(blocks + 1,),
        in_specs=[
            pl.BlockSpec((None, tile, d), mixer_block),
            _resident((d, 2 * width)),
            _resident((CONV_WIDTH, width)),
            _resident((1, width)),
            _resident(w_gate.shape),
            _resident((1, width)),
            _resident((1, width)),
            _resident((1, width)),
            _resident((width, d)),
            _resident((1, d)),
            _resident((1, d)),
            _resident((d, d_ff), layer),
            _resident((d_ff, d), layer),
            _resident((1, d)),
            _resident((1, d)),
        ],
        out_specs=pl.BlockSpec((None, tile, d), mlp_block),
        scratch_shapes=[pltpu.VMEM((8, width), F32), pltpu.VMEM((1, width), F32),
                        pltpu.VMEM((tile, d), F32)],
        compiler_params=pltpu.CompilerParams(
            dimension_semantics=("arbitrary",), vmem_limit_bytes=V7X_VMEM_LIMIT),
        name="rglru_layer",
    )(h, w_in.astype(BF16), conv_w, row(conv_b), w_gate.astype(BF16), row(b_a),
      row(b_x), row(lam), w_o.astype(BF16), row(g), row(b),
      w1_stack, w2_stack, row(g2), row(b2))


def _mla_proj_kernel(x_ref, pos_ref, inv_ref, w_down_ref, w_kpe_ref, qn_g_ref, kvn_g_ref,
                     w_uq_ref, w_ukv_ref, q_ref, k_ref, v_ref):
    heads = q_ref.shape[0]
    nope, lanes = MLA_NOPE, 2 * MLA_ROPE
    q_rank = qn_g_ref.shape[1]
    scale = (MLA_NOPE + MLA_ROPE) ** -0.5 * LOG2_E
    sub = MLA_PROJ_SUB
    ts = range(x_ref.shape[0] // sub)
    pairs = heads // 2
    pe_off = heads * nope
    rot_off = pe_off + pairs * lanes
    v_off = heads * nope
    pad_rows = v_ref.shape[1] - MLA_V
    ones_row = jnp.where(lax.broadcasted_iota(jnp.int32, (pad_rows, v_ref.shape[2]), 0) == 0,
                         1.0, 0.0).astype(BF16)

    xb = [x_ref[t * sub:(t + 1) * sub, :].astype(BF16) for t in ts]
    down = [_dot(xb[t], w_down_ref[...]) for t in ts]
    kpe = [_dot(xb[t], w_kpe_ref[...]) for t in ts]
    c_q = [_rms_norm(down[t][:, :q_rank], qn_g_ref[...]).astype(BF16) for t in ts]
    c_kv = [_rms_norm(down[t][:, q_rank:], kvn_g_ref[...]).astype(BF16) for t in ts]
    uq = [_dot(c_q[t], w_uq_ref[...]) for t in ts]
    ukv = [_dot(c_kv[t], w_ukv_ref[...]) for t in ts]

    for t in ts:
        rows = slice(t * sub, (t + 1) * sub)
        ang = pos_ref[rows, :] * inv_ref[...]
        cos = jnp.cos(ang)
        sin = jnp.sin(ang)
        k_pe = [(kpe[t][:, par * lanes:(par + 1) * lanes] * cos
                 + kpe[t][:, (2 + par) * lanes:(3 + par) * lanes] * sin).astype(BF16)
                for par in (0, 1)]
        q_pe_t = [((uq[t][:, pe_off + g * lanes:pe_off + (g + 1) * lanes] * cos
                    + uq[t][:, rot_off + g * lanes:rot_off + (g + 1) * lanes] * sin) * scale
                   ).T.astype(BF16) for g in range(pairs)]
        for h in range(heads):
            q_ref[h, :nope, rows] = (uq[t][:, h * nope:(h + 1) * nope] * scale).T.astype(BF16)
            q_ref[h, nope:, rows] = q_pe_t[h // 2]
            k_ref[h, rows, :nope] = ukv[t][:, h * nope:(h + 1) * nope].astype(BF16)
            k_ref[h, rows, nope:] = k_pe[h % 2]
            v_ref[h, :MLA_V, rows] = (
                ukv[t][:, v_off + h * MLA_V:v_off + (h + 1) * MLA_V].T.astype(BF16))
    for h in range(heads):
        v_ref[h, MLA_V:, :] = ones_row


def _attn_kernel(q_ref, k_ref, vt_ref, o_ref, s_ref):
    i = pl.program_id(2)
    tq, dv = o_ref.shape
    kb_keys = vt_ref.shape[2]
    sub = tk = ATTN_SUB
    nsub = tq // sub
    kpq = tq // tk
    per_kb = kb_keys // tk
    qs = [q_ref[:, s * sub:(s + 1) * sub] for s in range(nsub)]

    def diag_vt(c):
        return vt_ref[i * (kpq // per_kb) + c // per_kb][:, (c % per_kb) * tk:
                                                         (c % per_kb + 1) * tk]

    def scores(j, slot):
        kb = k_ref[pl.ds(pl.multiple_of(j * kb_keys, kb_keys), kb_keys), :]
        maxima = []
        for s in range(nsub):
            st = _dot(kb, qs[s])
            s_ref[slot, s] = st
            maxima.append(jnp.max(st, axis=0, keepdims=True))
        return tuple(maxima)

    def absorb(j, slot, stats, maxima):
        vt = vt_ref[j]
        ps, scaled = [], []
        for s, ((m_prev, acc), m_blk) in enumerate(zip(stats, maxima)):
            m_new = jnp.maximum(m_prev, m_blk)
            ps.append(jnp.exp2(s_ref[slot, s] - m_new).astype(BF16))
            scaled.append((m_new, jnp.exp2(m_prev - m_new) * acc))
        return tuple((m, a + _dot(vt, p)) for (m, a), p in zip(scaled, ps))

    tri = (lax.broadcasted_iota(jnp.int32, (tk, sub), 0)
           <= lax.broadcasted_iota(jnp.int32, (tk, sub), 1))
    diag_scores = {}
    for c in range(kpq):
        kb = k_ref[pl.ds(pl.multiple_of(i * tq + c * tk, tk), tk), :]
        for s in range(c, nsub):
            st = _dot(kb, qs[s])
            diag_scores[c, s] = jnp.where(tri, st, -jnp.inf) if c == s else st
    first_maxima = scores(0, 0)
    diag_p = {}
    diag_m = []
    for s in range(nsub):
        m0 = jnp.max(diag_scores[0, s], axis=0, keepdims=True)
        for c in range(1, s + 1):
            m0 = jnp.maximum(m0, jnp.max(diag_scores[c, s], axis=0, keepdims=True))
        for c in range(s + 1):
            diag_p[c, s] = jnp.exp2(diag_scores[c, s] - m0).astype(BF16)
        diag_m.append(m0)
    stats = []
    for s in range(nsub):
        acc = _dot(diag_vt(0), diag_p[0, s])
        for c in range(1, s + 1):
            acc = acc + _dot(diag_vt(c), diag_p[c, s])
        stats.append((diag_m[s], acc))

    nslot = s_ref.shape[0]
    bpq = tq // kb_keys

    def body(t, carry):
        stats, maxima = carry
        for c in range(nslot):
            cur = t * nslot + c
            nxt = jnp.minimum(cur + 1, i * bpq - 1)
            kb = k_ref[pl.ds(pl.multiple_of(nxt * kb_keys, kb_keys), kb_keys), :]
            vt = vt_ref[cur]
            new_stats, maxima_next = [], []
            for s in range(nsub):
                st = _dot(kb, qs[s])
                s_ref[(c + 1) % nslot, s] = st
                maxima_next.append(jnp.max(st, axis=0, keepdims=True))
                m_prev, acc = stats[s]
                m_new = jnp.maximum(m_prev, maxima[s])
                p = jnp.exp2(s_ref[c, s] - m_new).astype(BF16)
                new_stats.append((m_new, jnp.exp2(m_prev - m_new) * acc + _dot(vt, p)))
            stats, maxima = tuple(new_stats), tuple(maxima_next)
        return stats, maxima

    stats, _ = lax.fori_loop(0, i * (bpq // nslot), body, (tuple(stats), first_maxima))
    for s in range(nsub):
        _, acc = stats[s]
        o_ref[s * sub:(s + 1) * sub, :] = (acc[:dv] / acc[dv:dv + 1]).T.astype(o_ref.dtype)


def _rot_half_cols(w):
    half = w.shape[-1] // 2
    return jnp.concatenate([-w[..., half:], w[..., :half]], axis=-1)


def _mla_layer(h, pos, w_in, q_norm_g, kv_norm_g, w_uq, w_ukv):
    bsz, seq, d = h.shape
    heads, nope, rope_d, vd = MLA_HEADS, MLA_NOPE, MLA_ROPE, MLA_V
    dqk = nope + 2 * rope_d
    half = rope_d // 2
    inv = ROPE_BASE ** (-jnp.arange(half, dtype=F32) / half)
    inv = jnp.tile(inv, 4).reshape(1, 2 * rope_d)

    w_down = w_in[:, :MLA_Q_RANK + MLA_KV_RANK]
    w_pe = w_in[:, MLA_Q_RANK + MLA_KV_RANK:]
    zero = jnp.zeros_like(w_pe)
    w_pe_rot = _rot_half_cols(w_pe)
    w_kpe = jnp.concatenate([w_pe, zero, zero, w_pe, w_pe_rot, zero, zero, w_pe_rot], axis=1)
    w_uq3 = w_uq.reshape(MLA_Q_RANK, heads, nope + rope_d)
    w_qp = w_uq3[:, :, nope:]
    w_uq_all = jnp.concatenate(
        [w_uq3[:, :, :nope].reshape(MLA_Q_RANK, heads * nope),
         w_qp.reshape(MLA_Q_RANK, heads * rope_d),
         _rot_half_cols(w_qp).reshape(MLA_Q_RANK, heads * rope_d)], axis=1)
    w_ukv3 = w_ukv.reshape(MLA_KV_RANK, heads, nope + vd)
    w_ukv_all = jnp.concatenate(
        [w_ukv3[:, :, :nope].reshape(MLA_KV_RANK, heads * nope),
         w_ukv3[:, :, nope:].reshape(MLA_KV_RANK, heads * vd)], axis=1)

    tile = MLA_PROJ_TILE
    per_tk = ATTN_TK // tile
    bf = lambda t: t.astype(BF16)
    q, k, v = pl.pallas_call(
        _mla_proj_kernel,
        out_shape=(jax.ShapeDtypeStruct((bsz, heads, dqk, seq), BF16),
                   jax.ShapeDtypeStruct((bsz, heads, seq, dqk), BF16),
                   jax.ShapeDtypeStruct((bsz, heads, seq // ATTN_TK, vd + ATTN_VT_PAD, ATTN_TK), BF16)),
        grid=(bsz, seq // tile),
        in_specs=[
            pl.BlockSpec((None, tile, d), lambda bi, ti: (bi, ti, 0)),
            pl.BlockSpec((None, tile, 1), lambda bi, ti: (bi, ti, 0)),
            _resident((1, 2 * rope_d)),
            _resident(w_down.shape), _resident(w_kpe.shape),
            _resident((1, MLA_Q_RANK)), _resident((1, MLA_KV_RANK)),
            _resident(w_uq_all.shape), _resident(w_ukv_all.shape),
        ],
        out_specs=(pl.BlockSpec((None, heads, dqk, tile), lambda bi, ti: (bi, 0, 0, ti)),
                   pl.BlockSpec((None, heads, tile, dqk), lambda bi, ti: (bi, 0, ti, 0)),
                   pl.BlockSpec((None, heads, None, vd + ATTN_VT_PAD, tile),
                                lambda bi, ti: (bi, 0, ti // per_tk, 0, ti % per_tk))),
        compiler_params=pltpu.CompilerParams(
            dimension_semantics=("parallel", "parallel"), vmem_limit_bytes=V7X_VMEM_LIMIT),
        name="mla_proj",
    )(h, pos, inv, bf(w_down), bf(w_kpe), q_norm_g.reshape(1, -1), kv_norm_g.reshape(1, -1),
      bf(w_uq_all), bf(w_ukv_all))

    attn = pl.pallas_call(
        _attn_kernel,
        out_shape=jax.ShapeDtypeStruct((bsz, seq, heads * vd), BF16),
        grid=(bsz, heads, seq // ATTN_TQ),
        in_specs=[
            pl.BlockSpec((None, None, dqk, ATTN_TQ), lambda bi, hi, qi: (bi, hi, 0, qi)),
            pl.BlockSpec((None, None, seq, dqk), lambda bi, hi, qi: (bi, hi, 0, 0)),
            pl.BlockSpec((None, None, seq // ATTN_TK, vd + ATTN_VT_PAD, ATTN_TK),
                         lambda bi, hi, qi: (bi, hi, 0, 0, 0)),
        ],
        out_specs=pl.BlockSpec((None, ATTN_TQ, vd), lambda bi, hi, qi: (bi, qi, hi)),
        scratch_shapes=[pltpu.VMEM((ATTN_SLOTS, ATTN_TQ // ATTN_SUB, ATTN_TK, ATTN_SUB),
                                   F32)],
        compiler_params=pltpu.CompilerParams(
            dimension_semantics=("parallel", "parallel", "arbitrary"),
            vmem_limit_bytes=V7X_VMEM_LIMIT),
        name="mla_attention",
    )(q, k, v)

    return attn.reshape(bsz * seq, heads * vd)


def kernel(x, positions, ret_w_in, ret_gn_g, ret_w_o, lru_w_in, lru_conv_w, lru_conv_b, lru_w_a,
           lru_b_a, lru_w_x, lru_b_x, lru_lam, lru_w_o, mla_w_in, mla_q_norm, mla_kv_norm,
           mla_w_uq, mla_w_ukv, mla_w_o, ln_g, ln_b, mlp_w1, mlp_w2):
    bsz, seq, d = x.shape
    pos = positions.astype(F32).reshape(bsz, seq, 1)
    ret_w_in_b, ret_w_o_b = ret_w_in.astype(BF16), ret_w_o.astype(BF16)
    mlp_w1_b, mlp_w2_b = mlp_w1.astype(BF16), mlp_w2.astype(BF16)
    h = x
    for i in range(DEPTH):
        kind, j = i % N_MIXERS, i // N_MIXERS
        pre = None
        if kind == 0:
            h = _retention_layer(h, pos, ret_w_in_b, ret_gn_g[j], ret_w_o_b, j,
                                 ln_g[i, 0], ln_b[i, 0])
        elif kind == 1:
            h = _rglru_layer(h, lru_w_in[j], lru_conv_w[j], lru_conv_b[j], lru_w_a[j],
                             lru_b_a[j], lru_w_x[j], lru_b_x[j], lru_lam[j], lru_w_o[j],
                             ln_g[i, 0], ln_b[i, 0], mlp_w1_b, mlp_w2_b, i,
                             ln_g[i, 1], ln_b[i, 1])
            continue
        else:
            y = _mla_layer(h, pos, mla_w_in[j], mla_q_norm[j], mla_kv_norm[j], mla_w_uq[j],
                           mla_w_ukv[j])
            pre = (y, mla_w_o[j].astype(BF16), ln_g[i, 0], ln_b[i, 0])
        h = _mlp_layer(h.reshape(bsz * seq, d), mlp_w1_b, mlp_w2_b, i,
                       ln_g[i, 1], ln_b[i, 1], pre=pre).reshape(bsz, seq, d)
    return h
```

```python
import functools
import math

import jax
import jax.numpy as jnp
from jax import lax
from jax.experimental import pallas as pl
from jax.experimental.pallas import tpu as pltpu

F32 = jnp.float32
BF16 = jnp.bfloat16

DEPTH = 4
N_MIXERS = 3
RET_HEADS = 4
GN_EPS = 1e-5
LRU_BLOCKS = 10
CONV_WIDTH = 4
LRU_C = 8.0
MLA_HEADS = 8
MLA_NOPE = 128
MLA_ROPE = 64
MLA_V = 128
MLA_Q_RANK = 384
MLA_KV_RANK = 256
ROPE_BASE = 10000.0
LN_EPS = 1e-5
RMS_EPS = 1e-6
ALPHA = (2.0 * DEPTH) ** 0.25
LOG2_E = math.log2(math.e)

RET_CHUNK = 256
RET_TILE = 512
LRU_TILE = 512
LRU_SUB = 256
MLP_TILE = 1024
MLP_SUB = 256
MLP_PRE_TILE = 512
MLA_PROJ_TILE = 512
MLA_PROJ_SUB = 256
ATTN_TQ = 1024
ATTN_SUB = 256
ATTN_TK = 512
ATTN_SLOTS = 2
ATTN_VT_PAD = 16
FF_CHUNK = 1024

V7X_VMEM_LIMIT = 56 * 1024 * 1024


def _resident(shape, layer=None):
    nd = len(shape)
    if layer is None:
        return pl.BlockSpec(shape, lambda *_: (0,) * nd, pipeline_mode=pl.Buffered(1))
    return pl.BlockSpec((None,) + tuple(shape), lambda *_: (layer,) + (0,) * nd,
                        pipeline_mode=pl.Buffered(1))


def _dot(a, b):
    return jnp.dot(a, b, preferred_element_type=F32)


def _layer_norm(z, g, b):
    mu = jnp.mean(z, axis=-1, keepdims=True)
    zc = z - mu
    var = jnp.mean(zc * zc, axis=-1, keepdims=True)
    return zc * lax.rsqrt(var + LN_EPS) * g + b


def _rms_norm(z, g):
    return z * lax.rsqrt(jnp.mean(z * z, axis=-1, keepdims=True) + RMS_EPS) * g


def _sigmoid(z):
    return jax.nn.sigmoid(z)


def _mlp_residual(x, w1_ref, w2_ref):
    xb = x.astype(BF16)
    acc = ALPHA * x
    for c in range(w1_ref.shape[1] // FF_CHUNK):
        a = _dot(xb, w1_ref[:, c * FF_CHUNK:(c + 1) * FF_CHUNK])
        a = jnp.square(jnp.maximum(a, 0.0)).astype(BF16)
        acc = acc + _dot(a, w2_ref[c * FF_CHUNK:(c + 1) * FF_CHUNK, :])
    return acc


def _mlp_kernel(*refs, pre_proj, sub):
    if pre_proj:
        h_ref, y_ref, w_o_ref, g0_ref, b0_ref, w1_ref, w2_ref, g_ref, b_ref, o_ref = refs
    else:
        h_ref, w1_ref, w2_ref, g_ref, b_ref, o_ref = refs
    parts = range(h_ref.shape[0] // sub)
    accs = []
    for p in parts:
        x = h_ref[p * sub:(p + 1) * sub, :]
        if pre_proj:
            mix = ALPHA * x + _dot(y_ref[p * sub:(p + 1) * sub, :], w_o_ref[...])
            x = _layer_norm(mix, g0_ref[...], b0_ref[...])
        accs.append(_mlp_residual(x, w1_ref, w2_ref))
    for p in parts:
        o_ref[p * sub:(p + 1) * sub, :] = _layer_norm(accs[p], g_ref[...], b_ref[...])


def _mlp_layer(h, w1_stack, w2_stack, layer, g, b, pre=None):
    m, d = h.shape
    d_ff = w1_stack.shape[2]
    tile, sub = (MLP_TILE, MLP_SUB) if pre is None else (MLP_PRE_TILE, MLP_PRE_TILE)
    rows = lambda width: pl.BlockSpec((tile, width), lambda i: (i, 0))
    operands = [h]
    in_specs = [rows(d)]
    if pre is not None:
        y, w_o, g0, b0 = pre
        operands += [y, w_o, g0.reshape(1, d), b0.reshape(1, d)]
        in_specs += [rows(y.shape[1]), _resident(w_o.shape), _resident((1, d)), _resident((1, d))]
    operands += [w1_stack, w2_stack, g.reshape(1, d), b.reshape(1, d)]
    in_specs += [_resident((d, d_ff), layer), _resident((d_ff, d), layer),
                 _resident((1, d)), _resident((1, d))]
    return pl.pallas_call(
        functools.partial(_mlp_kernel, pre_proj=pre is not None, sub=sub),
        out_shape=jax.ShapeDtypeStruct((m, d), F32),
        grid=(m // tile,),
        in_specs=in_specs,
        out_specs=rows(d),
        compiler_params=pltpu.CompilerParams(
            dimension_semantics=("parallel",), vmem_limit_bytes=V7X_VMEM_LIMIT,
            allow_input_fusion=[i in (len(operands) - 4, len(operands) - 3)
                                for i in range(len(operands))]),
        name="mlp_ln",
    )(*operands)


def _ret_kernel(lg_ref, x_ref, pos_ref, inv_ref, w_in_ref, gn_ref, w_o_ref, g_ref, b_ref,
                o_ref, state_ref):
    heads, dk, dv = state_ref.shape
    half = dk // 2
    chunk = RET_CHUNK
    cs = range(x_ref.shape[0] // chunk)
    hs = range(heads)

    @pl.when(pl.program_id(1) == 0)
    def _():
        state_ref[...] = jnp.zeros_like(state_ref)

    def rows(ref, c):
        return ref[c * chunk:(c + 1) * chunk, :]

    k_off = heads * dk
    v_off = 2 * heads * dk
    g_off = v_off + heads * dv
    xb = [rows(x_ref, c).astype(BF16) for c in cs]
    q_raw = [[_dot(xb[c], w_in_ref[:, h * dk:(h + 1) * dk]) for h in hs] for c in cs]
    k_raw = [[_dot(xb[c], w_in_ref[:, k_off + h * dk:k_off + (h + 1) * dk]) for h in hs]
             for c in cs]
    vb = [[_dot(xb[c], w_in_ref[:, v_off + h * dv:v_off + (h + 1) * dv]).astype(BF16)
           for h in hs] for c in cs]
    gates = [[_dot(xb[c], w_in_ref[:, g_off + h * dv:g_off + (h + 1) * dv]) for h in hs]
             for c in cs]

    lgs = [lg_ref[h] for h in hs]
    row = lax.broadcasted_iota(jnp.int32, (chunk, chunk), 0)
    col = lax.broadcasted_iota(jnp.int32, (chunk, chunk), 1)
    diff = jnp.maximum(row - col, 0).astype(F32)
    idx = lax.broadcasted_iota(jnp.int32, (chunk, 1), 0).astype(F32)
    intra = [jnp.where(row >= col, jnp.exp(lgs[h] * diff), 0.0) for h in hs]
    q_dec = [jnp.exp(lgs[h] * (idx + 1.0)) for h in hs]
    idx_t = lax.broadcasted_iota(jnp.int32, (1, chunk), 1).astype(F32)
    k_dec_t = [jnp.exp(lgs[h] * (chunk - 1.0 - idx_t)) for h in hs]
    chunk_dec = [jnp.exp(lgs[h] * chunk) for h in hs]

    state = [state_ref[h] for h in hs]
    outs = []
    for c in cs:
        ang = rows(pos_ref, c) * inv_ref[...]
        cos = jnp.cos(ang)
        sin = jnp.sin(ang)

        def rope(t):
            t1, t2 = t[:, :half], t[:, half:]
            return jnp.concatenate([t1 * cos - t2 * sin, t1 * sin + t2 * cos], axis=-1)

        q = [rope(t) for t in q_raw[c]]
        k_t = [(rope(t) * (dk ** -0.5)).T for t in k_raw[c]]
        old = [state[h].astype(BF16) for h in hs]

        def chunk_out(h, scores_h):
            return _dot(jnp.concatenate([(scores_h * intra[h]).astype(BF16),
                                         (q[h] * q_dec[h]).astype(BF16)], axis=1),
                        jnp.concatenate([vb[c][h], old[h]], axis=0))

        out_c, prev_scores = [], None
        for h in hs:
            scores_h = _dot(q[h].astype(BF16), k_t[h].astype(BF16))
            if prev_scores is not None:
                out_c.append(chunk_out(h - 1, prev_scores))
            state[h] = state[h] * chunk_dec[h] + _dot((k_t[h] * k_dec_t[h]).astype(BF16),
                                                      vb[c][h])
            prev_scores = scores_h
        out_c.append(chunk_out(heads - 1, prev_scores))
        outs.append(out_c)
    for h in hs:
        state_ref[h] = state[h]

    for c in cs:
        mix = ALPHA * rows(x_ref, c)
        for h in hs:
            o = outs[c][h]
            mu = jnp.mean(o, axis=-1, keepdims=True)
            oc = o - mu
            var = jnp.mean(oc * oc, axis=-1, keepdims=True)
            y = oc * lax.rsqrt(var + GN_EPS) * gn_ref[:, h * dv:(h + 1) * dv]
            y = gates[c][h] * _sigmoid(gates[c][h]) * y
            mix = mix + _dot(y.astype(BF16), w_o_ref[h * dv:(h + 1) * dv, :])
        o_ref[c * chunk:(c + 1) * chunk, :] = _layer_norm(mix, g_ref[...], b_ref[...])


def _retention_layer(h, pos, w_in_stack, gn_g, w_o_stack, layer, g, b):
    bsz, seq, d = h.shape
    heads = RET_HEADS
    dk = d // heads
    dv = 2 * dk
    n_in = w_in_stack.shape[2]
    log_gamma = jnp.log1p(-jnp.exp2(-5.0 - jnp.arange(heads, dtype=F32)))
    half = dk // 2
    inv = (ROPE_BASE ** (-jnp.arange(half, dtype=F32) / half)).reshape(1, half)
    tile = RET_TILE
    grid_spec = pltpu.PrefetchScalarGridSpec(
        num_scalar_prefetch=1,
        grid=(bsz, seq // tile),
        in_specs=[
            pl.BlockSpec((None, tile, d), lambda bi, ci, lg: (bi, ci, 0)),
            pl.BlockSpec((None, tile, 1), lambda bi, ci, lg: (bi, ci, 0)),
            _resident((1, half)),
            _resident((d, n_in), layer),
            _resident((1, heads * dv)),
            _resident((heads * dv, d), layer),
            _resident((1, d)),
            _resident((1, d)),
        ],
        out_specs=pl.BlockSpec((None, tile, d), lambda bi, ci, lg: (bi, ci, 0)),
        scratch_shapes=[pltpu.VMEM((heads, dk, dv), F32)],
    )
    return pl.pallas_call(
        _ret_kernel,
        out_shape=jax.ShapeDtypeStruct((bsz, seq, d), F32),
        grid_spec=grid_spec,
        compiler_params=pltpu.CompilerParams(
            dimension_semantics=("parallel", "arbitrary"), vmem_limit_bytes=V7X_VMEM_LIMIT),
        name="retention_layer",
    )(log_gamma, h, pos, inv, w_in_stack, gn_g.reshape(1, -1), w_o_stack,
      g.reshape(1, d), b.reshape(1, d))


def _gelu_tanh(z):
    c = math.sqrt(2.0 / math.pi)
    return z * _sigmoid((2.0 * c) * (z + 0.044715 * (z * z * z)))


def _softplus(z):
    return jnp.maximum(z, 0.0) + jnp.log1p(jnp.exp(-jnp.abs(z)))


def _lru_kernel(x_ref, w_in_ref, conv_w_ref, conv_b_ref, w_gate_ref, b_a_ref, b_x_ref,
                lam_ref, w_o_ref, g_ref, b_ref, w1_ref, w2_ref, g2_ref, b2_ref,
                o_ref, tail_ref, h_ref, pend_ref, *, steps_per_seq):
    sub = LRU_SUB
    ts = range(x_ref.shape[0] // sub)
    width = conv_b_ref.shape[1]
    npair, pair_dim, _ = w_gate_ref.shape
    bdim = pair_dim // 2
    slab = 8
    blk = pl.program_id(0)

    @pl.when(blk % steps_per_seq == 0)
    def _():
        tail_ref[...] = jnp.zeros_like(tail_ref)
        h_ref[...] = jnp.zeros_like(h_ref)

    @pl.when(blk == 0)
    def _():
        pend_ref[...] = jnp.zeros_like(pend_ref)

    pending = [pend_ref[t * sub:(t + 1) * sub, :] for t in ts]

    xb = [x_ref[t * sub:(t + 1) * sub, :].astype(BF16) for t in ts]
    rec = [_dot(xb[t], w_in_ref[:, width:]) for t in ts]
    gate_raw = [_dot(xb[t], w_in_ref[:, :width]) for t in ts]

    row8 = lax.broadcasted_iota(jnp.int32, (slab, width), 0)
    tail = tail_ref[...]
    us = []
    for t in ts:
        u = rec[t] * conv_w_ref[CONV_WIDTH - 1:CONV_WIDTH, :] + conv_b_ref[...]
        for s in range(1, CONV_WIDTH):
            shifted = pltpu.roll(rec[t], s, 0)
            head = jnp.where(row8 < s, pltpu.roll(tail, s, 0), shifted[:slab])
            shifted = jnp.concatenate([head, shifted[slab:]], axis=0)
            u = u + shifted * conv_w_ref[CONV_WIDTH - 1 - s:CONV_WIDTH - s, :]
        tail = rec[t][sub - slab:, :]
        us.append(u)
    tail_ref[...] = tail

    def pending_mlp(t):
        o_ref[t * sub:(t + 1) * sub, :] = _layer_norm(
            _mlp_residual(pending[t], w1_ref, w2_ref), g2_ref[...], b2_ref[...])

    pre_a, pre_x = [], []
    for t in ts:
        ub = us[t].astype(BF16)
        both = [_dot(ub[:, p * pair_dim:(p + 1) * pair_dim], w_gate_ref[p]) for p in range(npair)]
        pre_a.append(jnp.concatenate(
            [o[:, k * bdim:(k + 1) * bdim] for o in both for k in (0, 2)], axis=-1))
        pre_x.append(jnp.concatenate(
            [o[:, k * bdim:(k + 1) * bdim] for o in both for k in (1, 3)], axis=-1))
    for t in ts:
        pending_mlp(t)

    neg_softplus_lam = -_softplus(-lam_ref[...])
    carry = h_ref[...]
    ys = []
    for t in ts:
        r = _sigmoid(pre_a[t] + b_a_ref[...])
        i_gate = _sigmoid(pre_x[t] + b_x_ref[...])
        log_a = LRU_C * r * neg_softplus_lam
        a = jnp.exp(log_a)
        b_in = jnp.sqrt(1.0 - jnp.exp(2.0 * log_a)) * (i_gate * us[t])
        a3 = a.reshape(sub // slab, slab, width)
        b3 = b_in.reshape(sub // slab, slab, width)
        rows = lax.broadcasted_iota(jnp.int32, a3.shape, 1)
        step = 1
        while step < slab:
            a_sh = pltpu.roll(a3, step, 1)
            b_sh = pltpu.roll(b3, step, 1)
            live = rows >= step
            b3 = jnp.where(live, a3 * b_sh + b3, b3)
            a3 = jnp.where(live, a3 * a_sh, a3)
            step *= 2
        slabs = []
        for i in range(sub // slab):
            hs_i = a3[i] * carry + b3[i]
            carry = hs_i[slab - 1:slab, :]
            slabs.append(hs_i)
        hs = jnp.concatenate(slabs, axis=0)
        ys.append((hs * _gelu_tanh(gate_raw[t])).astype(BF16))
    h_ref[...] = carry

    for t in ts:
        mix = ALPHA * x_ref[t * sub:(t + 1) * sub, :] + _dot(ys[t], w_o_ref[...])
        pend_ref[t * sub:(t + 1) * sub, :] = _layer_norm(mix, g_ref[...], b_ref[...])


def _rglru_layer(h, w_in, conv_w, conv_b, w_a, b_a, w_x, b_x, lam, w_o, g, b,
                 w1_stack, w2_stack, layer, g2, b2):
    bsz, seq, d = h.shape
    d_ff = w1_stack.shape[2]
    width = conv_b.shape[0]
    tile = LRU_TILE
    row = lambda t: t.reshape(1, -1)
    per_seq = seq // tile
    blocks = bsz * per_seq
    zero = jnp.zeros_like(w_a[0::2])
    w_gate = jnp.concatenate(
        [jnp.concatenate([w_a[0::2], w_x[0::2], zero, zero], axis=-1),
         jnp.concatenate([zero, zero, w_a[1::2], w_x[1::2]], axis=-1)], axis=1)

    def mixer_block(s):
        s = jnp.minimum(s, blocks - 1)
        return s // per_seq, s % per_seq, 0

    def mlp_block(s):
        s = jnp.maximum(s - 1, 0)
        return s // per_seq, s % per_seq, 0

    return pl.pallas_call(
        functools.partial(_lru_kernel, steps_per_seq=per_seq),
        out_shape=jax.ShapeDtypeStruct((bsz, seq, d), F32),
        grid=(blocks + 1,),
        in_specs=[
            pl.BlockSpec((None, tile, d), mixer_block),
            _resident((d, 2 * width)),
            _resident((CONV_WIDTH, width)),
            _resident((1, width)),
            _resident(w_gate.shape),
            _resident((1, width)),
            _resident((1, width)),
            _resident((1, width)),
            _resident((width, d)),
            _resident((1, d)),
            _resident((1, d)),
            _resident((d, d_ff), layer),
            _resident((d_ff, d), layer),
            _resident((1, d)),
            _resident((1, d)),
        ],
        out_specs=pl.BlockSpec((None, tile, d), mlp_block),
        scratch_shapes=[pltpu.VMEM((8, width), F32), pltpu.VMEM((1, width), F32),
                        pltpu.VMEM((tile, d), F32)],
        compiler_params=pltpu.CompilerParams(
            dimension_semantics=("arbitrary",), vmem_limit_bytes=V7X_VMEM_LIMIT,
            allow_input_fusion=[i in (11, 12) for i in range(15)]),
        name="rglru_layer",
    )(h, w_in.astype(BF16), conv_w, row(conv_b), w_gate.astype(BF16), row(b_a),
      row(b_x), row(lam), w_o.astype(BF16), row(g), row(b),
      w1_stack, w2_stack, row(g2), row(b2))


def _mla_proj_kernel(x_ref, pos_ref, inv_ref, w_down_ref, w_kpe_ref, qn_g_ref, kvn_g_ref,
                     w_uq_ref, w_ukv_ref, q_ref, k_ref, v_ref):
    heads = q_ref.shape[0]
    nope, lanes = MLA_NOPE, 2 * MLA_ROPE
    q_rank = qn_g_ref.shape[1]
    scale = (MLA_NOPE + MLA_ROPE) ** -0.5 * LOG2_E
    sub = MLA_PROJ_SUB
    ts = range(x_ref.shape[0] // sub)
    pairs = heads // 2
    pe_off = heads * nope
    rot_off = pe_off + pairs * lanes
    v_off = heads * nope
    pad_rows = v_ref.shape[1] - MLA_V
    ones_row = jnp.where(lax.broadcasted_iota(jnp.int32, (pad_rows, v_ref.shape[2]), 0) == 0,
                         1.0, 0.0).astype(BF16)

    xb = [x_ref[t * sub:(t + 1) * sub, :].astype(BF16) for t in ts]
    down = [_dot(xb[t], w_down_ref[...]) for t in ts]
    kpe = [_dot(xb[t], w_kpe_ref[...]) for t in ts]
    c_q = [_rms_norm(down[t][:, :q_rank], qn_g_ref[...]).astype(BF16) for t in ts]
    c_kv = [_rms_norm(down[t][:, q_rank:], kvn_g_ref[...]).astype(BF16) for t in ts]
    uq = [_dot(c_q[t], w_uq_ref[...]) for t in ts]
    ukv = [_dot(c_kv[t], w_ukv_ref[...]) for t in ts]

    for t in ts:
        rows = slice(t * sub, (t + 1) * sub)
        ang = pos_ref[rows, :] * inv_ref[...]
        cos = jnp.cos(ang)
        sin = jnp.sin(ang)
        k_pe = [(kpe[t][:, par * lanes:(par + 1) * lanes] * cos
                 + kpe[t][:, (2 + par) * lanes:(3 + par) * lanes] * sin).astype(BF16)
                for par in (0, 1)]
        q_pe_t = [((uq[t][:, pe_off + g * lanes:pe_off + (g + 1) * lanes] * cos
                    + uq[t][:, rot_off + g * lanes:rot_off + (g + 1) * lanes] * sin) * scale
                   ).T.astype(BF16) for g in range(pairs)]
        for h in range(heads):
            q_ref[h, :nope, rows] = (uq[t][:, h * nope:(h + 1) * nope] * scale).T.astype(BF16)
            q_ref[h, nope:, rows] = q_pe_t[h // 2]
            k_ref[h, rows, :nope] = ukv[t][:, h * nope:(h + 1) * nope].astype(BF16)
            k_ref[h, rows, nope:] = k_pe[h % 2]
            v_ref[h, :MLA_V, rows] = (
                ukv[t][:, v_off + h * MLA_V:v_off + (h + 1) * MLA_V].T.astype(BF16))
    for h in range(heads):
        v_ref[h, MLA_V:, :] = ones_row


def _attn_kernel(q_ref, k_ref, vt_ref, o_ref, s_ref):
    i = pl.program_id(2)
    tq, dv = o_ref.shape
    kb_keys = vt_ref.shape[2]
    sub = tk = ATTN_SUB
    nsub = tq // sub
    kpq = tq // tk
    per_kb = kb_keys // tk
    qs = [q_ref[:, s * sub:(s + 1) * sub] for s in range(nsub)]

    def diag_vt(c):
        return vt_ref[i * (kpq // per_kb) + c // per_kb][:, (c % per_kb) * tk:
                                                         (c % per_kb + 1) * tk]

    def scores(j, slot):
        kb = k_ref[pl.ds(pl.multiple_of(j * kb_keys, kb_keys), kb_keys), :]
        maxima = []
        for s in range(nsub):
            st = _dot(kb, qs[s])
            s_ref[slot, s] = st
            maxima.append(jnp.max(st, axis=0, keepdims=True))
        return tuple(maxima)

    def absorb(j, slot, stats, maxima):
        vt = vt_ref[j]
        ps, scaled = [], []
        for s, ((m_prev, acc), m_blk) in enumerate(zip(stats, maxima)):
            m_new = jnp.maximum(m_prev, m_blk)
            ps.append(jnp.exp2(s_ref[slot, s] - m_new).astype(BF16))
            scaled.append((m_new, jnp.exp2(m_prev - m_new) * acc))
        return tuple((m, a + _dot(vt, p)) for (m, a), p in zip(scaled, ps))

    tri = (lax.broadcasted_iota(jnp.int32, (tk, sub), 0)
           <= lax.broadcasted_iota(jnp.int32, (tk, sub), 1))
    diag_scores = {}
    for c in range(kpq):
        kb = k_ref[pl.ds(pl.multiple_of(i * tq + c * tk, tk), tk), :]
        for s in range(c, nsub):
            st = _dot(kb, qs[s])
            diag_scores[c, s] = jnp.where(tri, st, -jnp.inf) if c == s else st
    first_maxima = scores(0, 0)
    diag_p = {}
    diag_m = []
    for s in range(nsub):
        m0 = jnp.max(diag_scores[0, s], axis=0, keepdims=True)
        for c in range(1, s + 1):
            m0 = jnp.maximum(m0, jnp.max(diag_scores[c, s], axis=0, keepdims=True))
        for c in range(s + 1):
            diag_p[c, s] = jnp.exp2(diag_scores[c, s] - m0).astype(BF16)
        diag_m.append(m0)
    stats = []
    for s in range(nsub):
        acc = _dot(diag_vt(0), diag_p[0, s])
        for c in range(1, s + 1):
            acc = acc + _dot(diag_vt(c), diag_p[c, s])
        stats.append((diag_m[s], acc))

    nslot = s_ref.shape[0]
    bpq = tq // kb_keys

    def body(t, carry):
        stats, maxima = carry
        for c in range(nslot):
            cur = t * nslot + c
            nxt = jnp.minimum(cur + 1, i * bpq - 1)
            kb = k_ref[pl.ds(pl.multiple_of(nxt * kb_keys, kb_keys), kb_keys), :]
            vt = vt_ref[cur]
            new_stats, maxima_next = [], []
            for s in range(nsub):
                st = _dot(kb, qs[s])
                s_ref[(c + 1) % nslot, s] = st
                maxima_next.append(jnp.max(st, axis=0, keepdims=True))
                m_prev, acc = stats[s]
                m_new = jnp.maximum(m_prev, maxima[s])
                p = jnp.exp2(s_ref[c, s] - m_new).astype(BF16)
                new_stats.append((m_new, jnp.exp2(m_prev - m_new) * acc + _dot(vt, p)))
            stats, maxima = tuple(new_stats), tuple(maxima_next)
        return stats, maxima

    stats, _ = lax.fori_loop(0, i * (bpq // nslot), body, (tuple(stats), first_maxima))
    for s in range(nsub):
        _, acc = stats[s]
        o_ref[s * sub:(s + 1) * sub, :] = (acc[:dv] / acc[dv:dv + 1]).T.astype(o_ref.dtype)


def _rot_half_cols(w):
    half = w.shape[-1] // 2
    return jnp.concatenate([-w[..., half:], w[..., :half]], axis=-1)


def _mla_layer(h, pos, w_in, q_norm_g, kv_norm_g, w_uq, w_ukv):
    bsz, seq, d = h.shape
    heads, nope, rope_d, vd = MLA_HEADS, MLA_NOPE, MLA_ROPE, MLA_V
    dqk = nope + 2 * rope_d
    half = rope_d // 2
    inv = ROPE_BASE ** (-jnp.arange(half, dtype=F32) / half)
    inv = jnp.tile(inv, 4).reshape(1, 2 * rope_d)

    w_down = w_in[:, :MLA_Q_RANK + MLA_KV_RANK]
    w_pe = w_in[:, MLA_Q_RANK + MLA_KV_RANK:]
    zero = jnp.zeros_like(w_pe)
    w_pe_rot = _rot_half_cols(w_pe)
    w_kpe = jnp.concatenate([w_pe, zero, zero, w_pe, w_pe_rot, zero, zero, w_pe_rot], axis=1)
    w_uq3 = w_uq.reshape(MLA_Q_RANK, heads, nope + rope_d)
    w_qp = w_uq3[:, :, nope:]
    w_uq_all = jnp.concatenate(
        [w_uq3[:, :, :nope].reshape(MLA_Q_RANK, heads * nope),
         w_qp.reshape(MLA_Q_RANK, heads * rope_d),
         _rot_half_cols(w_qp).reshape(MLA_Q_RANK, heads * rope_d)], axis=1)
    w_ukv3 = w_ukv.reshape(MLA_KV_RANK, heads, nope + vd)
    w_ukv_all = jnp.concatenate(
        [w_ukv3[:, :, :nope].reshape(MLA_KV_RANK, heads * nope),
         w_ukv3[:, :, nope:].reshape(MLA_KV_RANK, heads * vd)], axis=1)

    tile = MLA_PROJ_TILE
    per_tk = ATTN_TK // tile
    bf = lambda t: t.astype(BF16)
    q, k, v = pl.pallas_call(
        _mla_proj_kernel,
        out_shape=(jax.ShapeDtypeStruct((bsz, heads, dqk, seq), BF16),
                   jax.ShapeDtypeStruct((bsz, heads, seq, dqk), BF16),
                   jax.ShapeDtypeStruct((bsz, heads, seq // ATTN_TK, vd + ATTN_VT_PAD, ATTN_TK), BF16)),
        grid=(bsz, seq // tile),
        in_specs=[
            pl.BlockSpec((None, tile, d), lambda bi, ti: (bi, ti, 0)),
            pl.BlockSpec((None, tile, 1), lambda bi, ti: (bi, ti, 0)),
            _resident((1, 2 * rope_d)),
            _resident(w_down.shape), _resident(w_kpe.shape),
            _resident((1, MLA_Q_RANK)), _resident((1, MLA_KV_RANK)),
            _resident(w_uq_all.shape), _resident(w_ukv_all.shape),
        ],
        out_specs=(pl.BlockSpec((None, heads, dqk, tile), lambda bi, ti: (bi, 0, 0, ti)),
                   pl.BlockSpec((None, heads, tile, dqk), lambda bi, ti: (bi, 0, ti, 0)),
                   pl.BlockSpec((None, heads, None, vd + ATTN_VT_PAD, tile),
                                lambda bi, ti: (bi, 0, ti // per_tk, 0, ti % per_tk))),
        compiler_params=pltpu.CompilerParams(
            dimension_semantics=("parallel", "parallel"), vmem_limit_bytes=V7X_VMEM_LIMIT),
        name="mla_proj",
    )(h, pos, inv, bf(w_down), bf(w_kpe), q_norm_g.reshape(1, -1), kv_norm_g.reshape(1, -1),
      bf(w_uq_all), bf(w_ukv_all))

    attn = pl.pallas_call(
        _attn_kernel,
        out_shape=jax.ShapeDtypeStruct((bsz, seq, heads * vd), BF16),
        grid=(bsz, heads, seq // ATTN_TQ),
        in_specs=[
            pl.BlockSpec((None, None, dqk, ATTN_TQ), lambda bi, hi, qi: (bi, hi, 0, qi)),
            pl.BlockSpec((None, None, seq, dqk), lambda bi, hi, qi: (bi, hi, 0, 0)),
            pl.BlockSpec((None, None, seq // ATTN_TK, vd + ATTN_VT_PAD, ATTN_TK),
                         lambda bi, hi, qi: (bi, hi, 0, 0, 0)),
        ],
        out_specs=pl.BlockSpec((None, ATTN_TQ, vd), lambda bi, hi, qi: (bi, qi, hi)),
        scratch_shapes=[pltpu.VMEM((ATTN_SLOTS, ATTN_TQ // ATTN_SUB, ATTN_TK, ATTN_SUB),
                                   F32)],
        compiler_params=pltpu.CompilerParams(
            dimension_semantics=("parallel", "parallel", "arbitrary"),
            vmem_limit_bytes=V7X_VMEM_LIMIT),
        name="mla_attention",
    )(q, k, v)

    return attn.reshape(bsz * seq, heads * vd)


def kernel(x, positions, ret_w_in, ret_gn_g, ret_w_o, lru_w_in, lru_conv_w, lru_conv_b, lru_w_a,
           lru_b_a, lru_w_x, lru_b_x, lru_lam, lru_w_o, mla_w_in, mla_q_norm, mla_kv_norm,
           mla_w_uq, mla_w_ukv, mla_w_o, ln_g, ln_b, mlp_w1, mlp_w2):
    bsz, seq, d = x.shape
    pos = positions.astype(F32).reshape(bsz, seq, 1)
    ret_w_in_b, ret_w_o_b = ret_w_in.astype(BF16), ret_w_o.astype(BF16)
    mlp_w1_b, mlp_w2_b = mlp_w1.astype(BF16), mlp_w2.astype(BF16)
    h = x
    for i in range(DEPTH):
        kind, j = i % N_MIXERS, i // N_MIXERS
        pre = None
        if kind == 0:
            h = _retention_layer(h, pos, ret_w_in_b, ret_gn_g[j], ret_w_o_b, j,
                                 ln_g[i, 0], ln_b[i, 0])
        elif kind == 1:
            h = _rglru_layer(h, lru_w_in[j], lru_conv_w[j], lru_conv_b[j], lru_w_a[j],
                             lru_b_a[j], lru_w_x[j], lru_b_x[j], lru_lam[j], lru_w_o[j],
                             ln_g[i, 0], ln_b[i, 0], mlp_w1_b, mlp_w2_b, i,
                             ln_g[i, 1], ln_b[i, 1])
            continue
        else:
            y = _mla_layer(h, pos, mla_w_in[j], mla_q_norm[j], mla_kv_norm[j], mla_w_uq[j],
                           mla_w_ukv[j])
            pre = (y, mla_w_o[j].astype(BF16), ln_g[i, 0], ln_b[i, 0])
        h = _mlp_layer(h.reshape(bsz * seq, d), mlp_w1_b, mlp_w2_b, i,
                       ln_g[i, 1], ln_b[i, 1], pre=pre).reshape(bsz, seq, d)
    return h
```

```python
import functools
import math

import jax
import jax.numpy as jnp
from jax import lax
from jax.experimental import pallas as pl
from jax.experimental.pallas import tpu as pltpu

F32 = jnp.float32
BF16 = jnp.bfloat16

DEPTH = 4
N_MIXERS = 3
RET_HEADS = 4
GN_EPS = 1e-5
CONV_WIDTH = 4
LRU_C = 8.0
MLA_HEADS = 8
MLA_NOPE = 128
MLA_ROPE = 64
MLA_V = 128
MLA_Q_RANK = 384
MLA_KV_RANK = 256
ROPE_BASE = 10000.0
LN_EPS = 1e-5
RMS_EPS = 1e-6
ALPHA = (2.0 * DEPTH) ** 0.25
LOG2_E = math.log2(math.e)

RET_CHUNK = 256
RET_TILE = 512
LRU_TILE = 512
LRU_SUB = 256
MLP_TILE = 1024
MLP_SUB = 256
MLP_PRE_TILE = 512
MLA_PROJ_TILE = 512
MLA_PROJ_SUB = 256
ATTN_TQ = 1024
ATTN_SUB = 256
ATTN_TK = 512
ATTN_SLOTS = 2
ATTN_VT_PAD = 16
FF_CHUNK = 1024

V7X_VMEM_LIMIT = 56 * 1024 * 1024


def _resident(shape, layer=None):
    nd = len(shape)
    if layer is None:
        return pl.BlockSpec(shape, lambda *_: (0,) * nd, pipeline_mode=pl.Buffered(1))
    return pl.BlockSpec((None,) + tuple(shape), lambda *_: (layer,) + (0,) * nd,
                        pipeline_mode=pl.Buffered(1))


def _dot(a, b):
    return jnp.dot(a, b, preferred_element_type=F32)


def _layer_norm(z, g, b):
    mu = jnp.mean(z, axis=-1, keepdims=True)
    zc = z - mu
    var = jnp.mean(zc * zc, axis=-1, keepdims=True)
    return zc * lax.rsqrt(var + LN_EPS) * g + b


def _rms_norm(z, g):
    return z * lax.rsqrt(jnp.mean(z * z, axis=-1, keepdims=True) + RMS_EPS) * g


def _sigmoid(z):
    return jax.nn.sigmoid(z)


def _mlp_residual(x, w1_ref, w2_ref):
    xb = x.astype(BF16)
    acc = ALPHA * x
    for c in range(w1_ref.shape[1] // FF_CHUNK):
        a = _dot(xb, w1_ref[:, c * FF_CHUNK:(c + 1) * FF_CHUNK])
        a = jnp.square(jnp.maximum(a, 0.0)).astype(BF16)
        acc = acc + _dot(a, w2_ref[c * FF_CHUNK:(c + 1) * FF_CHUNK, :])
    return acc


def _mlp_kernel(*refs, pre_proj, sub):
    if pre_proj:
        h_ref, y_ref, w_o_ref, g0_ref, b0_ref, w1_ref, w2_ref, g_ref, b_ref, o_ref = refs
    else:
        h_ref, w1_ref, w2_ref, g_ref, b_ref, o_ref = refs
    parts = range(h_ref.shape[0] // sub)
    accs = []
    for p in parts:
        x = h_ref[p * sub:(p + 1) * sub, :]
        if pre_proj:
            mix = ALPHA * x + _dot(y_ref[p * sub:(p + 1) * sub, :], w_o_ref[...])
            x = _layer_norm(mix, g0_ref[...], b0_ref[...])
        accs.append(_mlp_residual(x, w1_ref, w2_ref))
    for p in parts:
        o_ref[p * sub:(p + 1) * sub, :] = _layer_norm(accs[p], g_ref[...], b_ref[...])


def _mlp_layer(h, w1_stack, w2_stack, layer, g, b, pre=None):
    m, d = h.shape
    d_ff = w1_stack.shape[2]
    tile, sub = (MLP_TILE, MLP_SUB) if pre is None else (MLP_PRE_TILE, MLP_PRE_TILE)
    rows = lambda width: pl.BlockSpec((tile, width), lambda i: (i, 0))
    operands = [h]
    in_specs = [rows(d)]
    if pre is not None:
        y, w_o, g0, b0 = pre
        operands += [y, w_o, g0.reshape(1, d), b0.reshape(1, d)]
        in_specs += [rows(y.shape[1]), _resident(w_o.shape), _resident((1, d)), _resident((1, d))]
    operands += [w1_stack, w2_stack, g.reshape(1, d), b.reshape(1, d)]
    in_specs += [_resident((d, d_ff), layer), _resident((d_ff, d), layer),
                 _resident((1, d)), _resident((1, d))]
    return pl.pallas_call(
        functools.partial(_mlp_kernel, pre_proj=pre is not None, sub=sub),
        out_shape=jax.ShapeDtypeStruct((m, d), F32),
        grid=(m // tile,),
        in_specs=in_specs,
        out_specs=rows(d),
        compiler_params=pltpu.CompilerParams(
            dimension_semantics=("parallel",), vmem_limit_bytes=V7X_VMEM_LIMIT),
        name="mlp_ln",
    )(*operands)


def _ret_kernel(lg_ref, x_ref, pos_ref, inv_ref, w_in_ref, gn_ref, w_o_ref, g_ref, b_ref,
                o_ref, state_ref):
    heads, dk, dv = state_ref.shape
    half = dk // 2
    chunk = RET_CHUNK
    cs = range(x_ref.shape[0] // chunk)
    hs = range(heads)

    @pl.when(pl.program_id(1) == 0)
    def _():
        state_ref[...] = jnp.zeros_like(state_ref)

    def rows(ref, c):
        return ref[c * chunk:(c + 1) * chunk, :]

    k_off = heads * dk
    v_off = 2 * heads * dk
    g_off = v_off + heads * dv
    q_raw, k_raw, vb, gates = [], [], [], []
    for c in cs:
        xb = rows(x_ref, c).astype(BF16)
        q_raw.append([_dot(xb, w_in_ref[:, h * dk:(h + 1) * dk]) for h in hs])
        k_raw.append([_dot(xb, w_in_ref[:, k_off + h * dk:k_off + (h + 1) * dk]) for h in hs])
        vb.append([_dot(xb, w_in_ref[:, v_off + h * dv:v_off + (h + 1) * dv]).astype(BF16)
                   for h in hs])
        gates.append([_dot(xb, w_in_ref[:, g_off + h * dv:g_off + (h + 1) * dv]) for h in hs])

    lgs = [lg_ref[h] for h in hs]
    row = lax.broadcasted_iota(jnp.int32, (chunk, chunk), 0)
    col = lax.broadcasted_iota(jnp.int32, (chunk, chunk), 1)
    diff = jnp.maximum(row - col, 0).astype(F32)
    idx = lax.broadcasted_iota(jnp.int32, (chunk, 1), 0).astype(F32)
    intra = [jnp.where(row >= col, jnp.exp(lgs[h] * diff), 0.0) for h in hs]
    q_dec = [jnp.exp(lgs[h] * (idx + 1.0)) for h in hs]
    idx_t = lax.broadcasted_iota(jnp.int32, (1, chunk), 1).astype(F32)
    k_dec_t = [jnp.exp(lgs[h] * (chunk - 1.0 - idx_t)) for h in hs]
    chunk_dec = [jnp.exp(lgs[h] * chunk) for h in hs]

    state = [state_ref[h] for h in hs]
    outs = []
    for c in cs:
        ang = rows(pos_ref, c) * inv_ref[...]
        cos = jnp.cos(ang)
        sin = jnp.sin(ang)

        def rope(t):
            t1, t2 = t[:, :half], t[:, half:]
            return jnp.concatenate([t1 * cos - t2 * sin, t1 * sin + t2 * cos], axis=-1)

        q = [rope(t) for t in q_raw[c]]
        k_t = [(rope(t) * (dk ** -0.5)).T for t in k_raw[c]]
        old = [state[h].astype(BF16) for h in hs]

        def chunk_out(h, scores_h):
            return _dot(jnp.concatenate([(scores_h * intra[h]).astype(BF16),
                                         (q[h] * q_dec[h]).astype(BF16)], axis=1),
                        jnp.concatenate([vb[c][h], old[h]], axis=0))

        out_c, prev_scores = [], None
        for h in hs:
            scores_h = _dot(q[h].astype(BF16), k_t[h].astype(BF16))
            if prev_scores is not None:
                out_c.append(chunk_out(h - 1, prev_scores))
            state[h] = state[h] * chunk_dec[h] + _dot((k_t[h] * k_dec_t[h]).astype(BF16),
                                                      vb[c][h])
            prev_scores = scores_h
        out_c.append(chunk_out(heads - 1, prev_scores))
        outs.append(out_c)
    for h in hs:
        state_ref[h] = state[h]

    for c in cs:
        mix = ALPHA * rows(x_ref, c)
        for h in hs:
            o = outs[c][h]
            mu = jnp.mean(o, axis=-1, keepdims=True)
            oc = o - mu
            var = jnp.mean(oc * oc, axis=-1, keepdims=True)
            y = oc * lax.rsqrt(var + GN_EPS) * gn_ref[:, h * dv:(h + 1) * dv]
            y = gates[c][h] * _sigmoid(gates[c][h]) * y
            mix = mix + _dot(y.astype(BF16), w_o_ref[h * dv:(h + 1) * dv, :])
        o_ref[c * chunk:(c + 1) * chunk, :] = _layer_norm(mix, g_ref[...], b_ref[...])


def _retention_layer(h, pos, w_in_stack, gn_g, w_o_stack, layer, g, b):
    bsz, seq, d = h.shape
    heads = RET_HEADS
    dk = d // heads
    dv = 2 * dk
    n_in = w_in_stack.shape[2]
    log_gamma = jnp.log1p(-jnp.exp2(-5.0 - jnp.arange(heads, dtype=F32)))
    half = dk // 2
    inv = (ROPE_BASE ** (-jnp.arange(half, dtype=F32) / half)).reshape(1, half)
    tile = RET_TILE
    grid_spec = pltpu.PrefetchScalarGridSpec(
        num_scalar_prefetch=1,
        grid=(bsz, seq // tile),
        in_specs=[
            pl.BlockSpec((None, tile, d), lambda bi, ci, lg: (bi, ci, 0)),
            pl.BlockSpec((None, tile, 1), lambda bi, ci, lg: (bi, ci, 0)),
            _resident((1, half)),
            _resident((d, n_in), layer),
            _resident((1, heads * dv)),
            _resident((heads * dv, d), layer),
            _resident((1, d)),
            _resident((1, d)),
        ],
        out_specs=pl.BlockSpec((None, tile, d), lambda bi, ci, lg: (bi, ci, 0)),
        scratch_shapes=[pltpu.VMEM((heads, dk, dv), F32)],
    )
    return pl.pallas_call(
        _ret_kernel,
        out_shape=jax.ShapeDtypeStruct((bsz, seq, d), F32),
        grid_spec=grid_spec,
        compiler_params=pltpu.CompilerParams(
            dimension_semantics=("parallel", "arbitrary"), vmem_limit_bytes=V7X_VMEM_LIMIT),
        name="retention_layer",
    )(log_gamma, h, pos, inv, w_in_stack, gn_g.reshape(1, -1), w_o_stack,
      g.reshape(1, d), b.reshape(1, d))


def _gelu_tanh(z):
    c = math.sqrt(2.0 / math.pi)
    return z * _sigmoid((2.0 * c) * (z + 0.044715 * (z * z * z)))


def _softplus(z):
    return jnp.maximum(z, 0.0) + jnp.log1p(jnp.exp(-jnp.abs(z)))


def _lru_kernel(x_ref, w_in_ref, conv_w_ref, conv_b_ref, w_gate_ref, b_a_ref, b_x_ref,
                lam_ref, w_o_ref, g_ref, b_ref, w1_ref, w2_ref, g2_ref, b2_ref,
                o_ref, tail_ref, h_ref, pend_ref, *, steps_per_seq):
    sub = LRU_SUB
    ts = range(x_ref.shape[0] // sub)
    width = conv_b_ref.shape[1]
    npair, pair_dim, _ = w_gate_ref.shape
    bdim = pair_dim // 2
    slab = 8
    blk = pl.program_id(0)

    @pl.when(blk % steps_per_seq == 0)
    def _():
        tail_ref[...] = jnp.zeros_like(tail_ref)
        h_ref[...] = jnp.zeros_like(h_ref)

    @pl.when(blk == 0)
    def _():
        pend_ref[...] = jnp.zeros_like(pend_ref)

    pending = [pend_ref[t * sub:(t + 1) * sub, :] for t in ts]

    xb = [x_ref[t * sub:(t + 1) * sub, :].astype(BF16) for t in ts]
    rec = [_dot(xb[t], w_in_ref[:, width:]) for t in ts]
    gate_raw = [_dot(xb[t], w_in_ref[:, :width]) for t in ts]

    row8 = lax.broadcasted_iota(jnp.int32, (slab, width), 0)
    tail = tail_ref[...]
    us = []
    for t in ts:
        u = rec[t] * conv_w_ref[CONV_WIDTH - 1:CONV_WIDTH, :] + conv_b_ref[...]
        for s in range(1, CONV_WIDTH):
            shifted = pltpu.roll(rec[t], s, 0)
            head = jnp.where(row8 < s, pltpu.roll(tail, s, 0), shifted[:slab])
            shifted = jnp.concatenate([head, shifted[slab:]], axis=0)
            u = u + shifted * conv_w_ref[CONV_WIDTH - 1 - s:CONV_WIDTH - s, :]
        tail = rec[t][sub - slab:, :]
        us.append(u)
    tail_ref[...] = tail

    def pending_mlp(t):
        o_ref[t * sub:(t + 1) * sub, :] = _layer_norm(
            _mlp_residual(pending[t], w1_ref, w2_ref), g2_ref[...], b2_ref[...])

    pre_a, pre_x = [], []
    for t in ts:
        ub = us[t].astype(BF16)
        both = [_dot(ub[:, p * pair_dim:(p + 1) * pair_dim], w_gate_ref[p]) for p in range(npair)]
        pre_a.append(jnp.concatenate(
            [o[:, k * bdim:(k + 1) * bdim] for o in both for k in (0, 2)], axis=-1))
        pre_x.append(jnp.concatenate(
            [o[:, k * bdim:(k + 1) * bdim] for o in both for k in (1, 3)], axis=-1))
    for t in ts:
        pending_mlp(t)

    neg_softplus_lam = -_softplus(-lam_ref[...])
    carry = h_ref[...]
    ys = []
    for t in ts:
        r = _sigmoid(pre_a[t] + b_a_ref[...])
        i_gate = _sigmoid(pre_x[t] + b_x_ref[...])
        log_a = LRU_C * r * neg_softplus_lam
        a = jnp.exp(log_a)
        b_in = jnp.sqrt(1.0 - jnp.exp(2.0 * log_a)) * (i_gate * us[t])
        a3 = a.reshape(sub // slab, slab, width)
        b3 = b_in.reshape(sub // slab, slab, width)
        rows = lax.broadcasted_iota(jnp.int32, a3.shape, 1)
        step = 1
        while step < slab:
            a_sh = pltpu.roll(a3, step, 1)
            b_sh = pltpu.roll(b3, step, 1)
            live = rows >= step
            b3 = jnp.where(live, a3 * b_sh + b3, b3)
            a3 = jnp.where(live, a3 * a_sh, a3)
            step *= 2
        slabs = []
        for i in range(sub // slab):
            hs_i = a3[i] * carry + b3[i]
            carry = hs_i[slab - 1:slab, :]
            slabs.append(hs_i)
        hs = jnp.concatenate(slabs, axis=0)
        ys.append((hs * _gelu_tanh(gate_raw[t])).astype(BF16))
    h_ref[...] = carry

    for t in ts:
        mix = ALPHA * x_ref[t * sub:(t + 1) * sub, :] + _dot(ys[t], w_o_ref[...])
        pend_ref[t * sub:(t + 1) * sub, :] = _layer_norm(mix, g_ref[...], b_ref[...])


def _rglru_layer(h, w_in, conv_w, conv_b, w_a, b_a, w_x, b_x, lam, w_o, g, b,
                 w1_stack, w2_stack, layer, g2, b2):
    bsz, seq, d = h.shape
    d_ff = w1_stack.shape[2]
    width = conv_b.shape[0]
    tile = LRU_TILE
    row = lambda t: t.reshape(1, -1)
    per_seq = seq // tile
    blocks = bsz * per_seq
    zero = jnp.zeros_like(w_a[0::2])
    w_gate = jnp.concatenate(
        [jnp.concatenate([w_a[0::2], w_x[0::2], zero, zero], axis=-1),
         jnp.concatenate([zero, zero, w_a[1::2], w_x[1::2]], axis=-1)], axis=1)

    def mixer_block(s):
        s = jnp.minimum(s, blocks - 1)
        return s // per_seq, s % per_seq, 0

    def mlp_block(s):
        s = jnp.maximum(s - 1, 0)
        return s // per_seq, s % per_seq, 0

    return pl.pallas_call(
        functools.partial(_lru_kernel, steps_per_seq=per_seq),
        out_shape=jax.ShapeDtypeStruct((bsz, seq, d), F32),
        grid=(blocks + 1,),
        in_specs=[
            pl.BlockSpec((None, tile, d), mixer_block),
            _resident((d, 2 * width)),
            _resident((CONV_WIDTH, width)),
            _resident((1, width)),
            _resident(w_gate.shape),
            _resident((1, width)),
            _resident((1, width)),
            _resident((1, width)),
            _resident((width, d)),
            _resident((1, d)),
            _resident((1, d)),
            _resident((d, d_ff), layer),
            _resident((d_ff, d), layer),
            _resident((1, d)),
            _resident((1, d)),
        ],
        out_specs=pl.BlockSpec((None, tile, d), mlp_block),
        scratch_shapes=[pltpu.VMEM((8, width), F32), pltpu.VMEM((1, width), F32),
                        pltpu.VMEM((tile, d), F32)],
        compiler_params=pltpu.CompilerParams(
            dimension_semantics=("arbitrary",), vmem_limit_bytes=V7X_VMEM_LIMIT),
        name="rglru_layer",
    )(h, w_in.astype(BF16), conv_w, row(conv_b), w_gate.astype(BF16), row(b_a),
      row(b_x), row(lam), w_o.astype(BF16), row(g), row(b),
      w1_stack, w2_stack, row(g2), row(b2))


def _mla_proj_kernel(x_ref, pos_ref, inv_ref, w_down_ref, w_kpe_ref, qn_g_ref, kvn_g_ref,
                     w_uq_ref, w_ukv_ref, q_ref, k_ref, v_ref):
    heads = q_ref.shape[0]
    nope, lanes = MLA_NOPE, 2 * MLA_ROPE
    q_rank = qn_g_ref.shape[1]
    scale = (MLA_NOPE + MLA_ROPE) ** -0.5 * LOG2_E
    sub = MLA_PROJ_SUB
    ts = range(x_ref.shape[0] // sub)
    pairs = heads // 2
    pe_off = heads * nope
    rot_off = pe_off + pairs * lanes
    v_off = heads * nope
    pad_rows = v_ref.shape[1] - MLA_V
    ones_row = jnp.where(lax.broadcasted_iota(jnp.int32, (pad_rows, v_ref.shape[2]), 0) == 0,
                         1.0, 0.0).astype(BF16)

    xb = [x_ref[t * sub:(t + 1) * sub, :].astype(BF16) for t in ts]
    down = [_dot(xb[t], w_down_ref[...]) for t in ts]
    kpe = [_dot(xb[t], w_kpe_ref[...]) for t in ts]
    c_q = [_rms_norm(down[t][:, :q_rank], qn_g_ref[...]).astype(BF16) for t in ts]
    c_kv = [_rms_norm(down[t][:, q_rank:], kvn_g_ref[...]).astype(BF16) for t in ts]
    uq = [_dot(c_q[t], w_uq_ref[...]) for t in ts]
    ukv = [_dot(c_kv[t], w_ukv_ref[...]) for t in ts]

    for t in ts:
        rows = slice(t * sub, (t + 1) * sub)
        ang = pos_ref[rows, :] * inv_ref[...]
        cos = jnp.cos(ang)
        sin = jnp.sin(ang)
        k_pe = [(kpe[t][:, par * lanes:(par + 1) * lanes] * cos
                 + kpe[t][:, (2 + par) * lanes:(3 + par) * lanes] * sin).astype(BF16)
                for par in (0, 1)]
        q_pe_t = [((uq[t][:, pe_off + g * lanes:pe_off + (g + 1) * lanes] * cos
                    + uq[t][:, rot_off + g * lanes:rot_off + (g + 1) * lanes] * sin) * scale
                   ).T.astype(BF16) for g in range(pairs)]
        for h in range(heads):
            q_ref[h, :nope, rows] = (uq[t][:, h * nope:(h + 1) * nope] * scale).T.astype(BF16)
            q_ref[h, nope:, rows] = q_pe_t[h // 2]
            k_ref[h, rows, :nope] = ukv[t][:, h * nope:(h + 1) * nope].astype(BF16)
            k_ref[h, rows, nope:] = k_pe[h % 2]
            v_ref[h, :MLA_V, rows] = (
                ukv[t][:, v_off + h * MLA_V:v_off + (h + 1) * MLA_V].T.astype(BF16))
    for h in range(heads):
        v_ref[h, MLA_V:, :] = ones_row


def _attn_kernel(q_ref, k_ref, vt_ref, o_ref, s_ref):
    i = pl.program_id(2)
    tq, dv = o_ref.shape
    kb_keys = vt_ref.shape[2]
    sub = tk = ATTN_SUB
    nsub = tq // sub
    kpq = tq // tk
    per_kb = kb_keys // tk
    qs = [q_ref[:, s * sub:(s + 1) * sub] for s in range(nsub)]

    def diag_vt(c):
        return vt_ref[i * (kpq // per_kb) + c // per_kb][:, (c % per_kb) * tk:
                                                         (c % per_kb + 1) * tk]

    def scores(j, slot):
        kb = k_ref[pl.ds(pl.multiple_of(j * kb_keys, kb_keys), kb_keys), :]
        maxima = []
        for s in range(nsub):
            st = _dot(kb, qs[s])
            s_ref[slot, s] = st
            maxima.append(jnp.max(st, axis=0, keepdims=True))
        return tuple(maxima)

    def absorb(j, slot, stats, maxima):
        vt = vt_ref[j]
        ps, scaled = [], []
        for s, ((m_prev, acc), m_blk) in enumerate(zip(stats, maxima)):
            m_new = jnp.maximum(m_prev, m_blk)
            ps.append(jnp.exp2(s_ref[slot, s] - m_new).astype(BF16))
            scaled.append((m_new, jnp.exp2(m_prev - m_new) * acc))
        return tuple((m, a + _dot(vt, p)) for (m, a), p in zip(scaled, ps))

    tri = (lax.broadcasted_iota(jnp.int32, (tk, sub), 0)
           <= lax.broadcasted_iota(jnp.int32, (tk, sub), 1))
    diag_scores = {}
    for c in range(kpq):
        kb = k_ref[pl.ds(pl.multiple_of(i * tq + c * tk, tk), tk), :]
        for s in range(c, nsub):
            st = _dot(kb, qs[s])
            diag_scores[c, s] = jnp.where(tri, st, -jnp.inf) if c == s else st
    first_maxima = scores(0, 0)
    diag_p = {}
    diag_m = []
    for s in range(nsub):
        m0 = jnp.max(diag_scores[0, s], axis=0, keepdims=True)
        for c in range(1, s + 1):
            m0 = jnp.maximum(m0, jnp.max(diag_scores[c, s], axis=0, keepdims=True))
        for c in range(s + 1):
            diag_p[c, s] = jnp.exp2(diag_scores[c, s] - m0).astype(BF16)
        diag_m.append(m0)
    stats = []
    for s in range(nsub):
        acc = _dot(diag_vt(0), diag_p[0, s])
        for c in range(1, s + 1):
            acc = acc + _dot(diag_vt(c), diag_p[c, s])
        stats.append((diag_m[s], acc))

    nslot = s_ref.shape[0]
    bpq = tq // kb_keys

    def body(t, carry):
        stats, maxima = carry
        for c in range(nslot):
            cur = t * nslot + c
            nxt = jnp.minimum(cur + 1, i * bpq - 1)
            kb = k_ref[pl.ds(pl.multiple_of(nxt * kb_keys, kb_keys), kb_keys), :]
            vt = vt_ref[cur]
            new_stats, maxima_next = [], []
            for s in range(nsub):
                st = _dot(kb, qs[s])
                s_ref[(c + 1) % nslot, s] = st
                maxima_next.append(jnp.max(st, axis=0, keepdims=True))
                m_prev, acc = stats[s]
                m_new = jnp.maximum(m_prev, maxima[s])
                p = jnp.exp2(s_ref[c, s] - m_new).astype(BF16)
                new_stats.append((m_new, jnp.exp2(m_prev - m_new) * acc + _dot(vt, p)))
            stats, maxima = tuple(new_stats), tuple(maxima_next)
        return stats, maxima

    stats, _ = lax.fori_loop(0, i * (bpq // nslot), body, (tuple(stats), first_maxima))
    for s in range(nsub):
        _, acc = stats[s]
        o_ref[s * sub:(s + 1) * sub, :] = (acc[:dv] / acc[dv:dv + 1]).T.astype(o_ref.dtype)


def _rot_half_cols(w):
    half = w.shape[-1] // 2
    return jnp.concatenate([-w[..., half:], w[..., :half]], axis=-1)


def _mla_layer(h, pos, w_in, q_norm_g, kv_norm_g, w_uq, w_ukv):
    bsz, seq, d = h.shape
    heads, nope, rope_d, vd = MLA_HEADS, MLA_NOPE, MLA_ROPE, MLA_V
    dqk = nope + 2 * rope_d
    half = rope_d // 2
    inv = ROPE_BASE ** (-jnp.arange(half, dtype=F32) / half)
    inv = jnp.tile(inv, 4).reshape(1, 2 * rope_d)

    w_down = w_in[:, :MLA_Q_RANK + MLA_KV_RANK]
    w_pe = w_in[:, MLA_Q_RANK + MLA_KV_RANK:]
    zero = jnp.zeros_like(w_pe)
    w_pe_rot = _rot_half_cols(w_pe)
    w_kpe = jnp.concatenate([w_pe, zero, zero, w_pe, w_pe_rot, zero, zero, w_pe_rot], axis=1)
    w_uq3 = w_uq.reshape(MLA_Q_RANK, heads, nope + rope_d)
    w_qp = w_uq3[:, :, nope:]
    w_uq_all = jnp.concatenate(
        [w_uq3[:, :, :nope].reshape(MLA_Q_RANK, heads * nope),
         w_qp.reshape(MLA_Q_RANK, heads * rope_d),
         _rot_half_cols(w_qp).reshape(MLA_Q_RANK, heads * rope_d)], axis=1)
    w_ukv3 = w_ukv.reshape(MLA_KV_RANK, heads, nope + vd)
    w_ukv_all = jnp.concatenate(
        [w_ukv3[:, :, :nope].reshape(MLA_KV_RANK, heads * nope),
         w_ukv3[:, :, nope:].reshape(MLA_KV_RANK, heads * vd)], axis=1)

    tile = MLA_PROJ_TILE
    per_tk = ATTN_TK // tile
    bf = lambda t: t.astype(BF16)
    q, k, v = pl.pallas_call(
        _mla_proj_kernel,
        out_shape=(jax.ShapeDtypeStruct((bsz, heads, dqk, seq), BF16),
                   jax.ShapeDtypeStruct((bsz, heads, seq, dqk), BF16),
                   jax.ShapeDtypeStruct((bsz, heads, seq // ATTN_TK, vd + ATTN_VT_PAD, ATTN_TK), BF16)),
        grid=(bsz, seq // tile),
        in_specs=[
            pl.BlockSpec((None, tile, d), lambda bi, ti: (bi, ti, 0)),
            pl.BlockSpec((None, tile, 1), lambda bi, ti: (bi, ti, 0)),
            _resident((1, 2 * rope_d)),
            _resident(w_down.shape), _resident(w_kpe.shape),
            _resident((1, MLA_Q_RANK)), _resident((1, MLA_KV_RANK)),
            _resident(w_uq_all.shape), _resident(w_ukv_all.shape),
        ],
        out_specs=(pl.BlockSpec((None, heads, dqk, tile), lambda bi, ti: (bi, 0, 0, ti)),
                   pl.BlockSpec((None, heads, tile, dqk), lambda bi, ti: (bi, 0, ti, 0)),
                   pl.BlockSpec((None, heads, None, vd + ATTN_VT_PAD, tile),
                                lambda bi, ti: (bi, 0, ti // per_tk, 0, ti % per_tk))),
        compiler_params=pltpu.CompilerParams(
            dimension_semantics=("parallel", "parallel"), vmem_limit_bytes=V7X_VMEM_LIMIT),
        name="mla_proj",
    )(h, pos, inv, bf(w_down), bf(w_kpe), q_norm_g.reshape(1, -1), kv_norm_g.reshape(1, -1),
      bf(w_uq_all), bf(w_ukv_all))

    attn = pl.pallas_call(
        _attn_kernel,
        out_shape=jax.ShapeDtypeStruct((bsz, seq, heads * vd), BF16),
        grid=(bsz, heads, seq // ATTN_TQ),
        in_specs=[
            pl.BlockSpec((None, None, dqk, ATTN_TQ), lambda bi, hi, qi: (bi, hi, 0, qi)),
            pl.BlockSpec((None, None, seq, dqk), lambda bi, hi, qi: (bi, hi, 0, 0)),
            pl.BlockSpec((None, None, seq // ATTN_TK, vd + ATTN_VT_PAD, ATTN_TK),
                         lambda bi, hi, qi: (bi, hi, 0, 0, 0)),
        ],
        out_specs=pl.BlockSpec((None, ATTN_TQ, vd), lambda bi, hi, qi: (bi, qi, hi)),
        scratch_shapes=[pltpu.VMEM((ATTN_SLOTS, ATTN_TQ // ATTN_SUB, ATTN_TK, ATTN_SUB),
                                   F32)],
        compiler_params=pltpu.CompilerParams(
            dimension_semantics=("parallel", "parallel", "arbitrary"),
            vmem_limit_bytes=V7X_VMEM_LIMIT),
        name="mla_attention",
    )(q, k, v)

    return attn.reshape(bsz * seq, heads * vd)


def kernel(x, positions, ret_w_in, ret_gn_g, ret_w_o, lru_w_in, lru_conv_w, lru_conv_b, lru_w_a,
           lru_b_a, lru_w_x, lru_b_x, lru_lam, lru_w_o, mla_w_in, mla_q_norm, mla_kv_norm,
           mla_w_uq, mla_w_ukv, mla_w_o, ln_g, ln_b, mlp_w1, mlp_w2):
    bsz, seq, d = x.shape
    pos = positions.astype(F32).reshape(bsz, seq, 1)
    ret_w_in_b, ret_w_o_b = ret_w_in.astype(BF16), ret_w_o.astype(BF16)
    mlp_w1_b, mlp_w2_b = mlp_w1.astype(BF16), mlp_w2.astype(BF16)
    h = x
    for i in range(DEPTH):
        kind, j = i % N_MIXERS, i // N_MIXERS
        pre = None
        if kind == 0:
            h = _retention_layer(h, pos, ret_w_in_b, ret_gn_g[j], ret_w_o_b, j,
                                 ln_g[i, 0], ln_b[i, 0])
        elif kind == 1:
            h = _rglru_layer(h, lru_w_in[j], lru_conv_w[j], lru_conv_b[j], lru_w_a[j],
                             lru_b_a[j], lru_w_x[j], lru_b_x[j], lru_lam[j], lru_w_o[j],
                             ln_g[i, 0], ln_b[i, 0], mlp_w1_b, mlp_w2_b, i,
                             ln_g[i, 1], ln_b[i, 1])
            continue
        else:
            y = _mla_layer(h, pos, mla_w_in[j], mla_q_norm[j], mla_kv_norm[j], mla_w_uq[j],
                           mla_w_ukv[j])
            pre = (y, mla_w_o[j].astype(BF16), ln_g[i, 0], ln_b[i, 0])
        h = _mlp_layer(h.reshape(bsz * seq, d), mlp_w1_b, mlp_w2_b, i,
                       ln_g[i, 1], ln_b[i, 1], pre=pre).reshape(bsz, seq, d)
    return h
```

```python
import functools
import math

import jax
import jax.numpy as jnp
from jax import lax
from jax.experimental import pallas as pl
from jax.experimental.pallas import tpu as pltpu

F32 = jnp.float32
BF16 = jnp.bfloat16

DEPTH = 4
N_MIXERS = 3
RET_HEADS = 4
GN_EPS = 1e-5
CONV_WIDTH = 4
LRU_C = 8.0
MLA_HEADS = 8
MLA_NOPE = 128
MLA_ROPE = 64
MLA_V = 128
MLA_Q_RANK = 384
MLA_KV_RANK = 256
ROPE_BASE = 10000.0
LN_EPS = 1e-5
RMS_EPS = 1e-6
ALPHA = (2.0 * DEPTH) ** 0.25
LOG2_E = math.log2(math.e)

RET_CHUNK = 256
RET_TILE = 512
LRU_TILE = 512
LRU_SUB = 256
MLP_TILE = 1024
MLP_SUB = 256
MLP_PRE_TILE = 512
MLA_PROJ_TILE = 512
MLA_PROJ_SUB = 256
ATTN_TQ = 1024
ATTN_SUB = 256
ATTN_TK = 512
ATTN_SLOTS = 2
ATTN_VT_PAD = 16
FF_CHUNK = 1024

V7X_VMEM_LIMIT = 56 * 1024 * 1024


def _resident(shape, layer=None):
    nd = len(shape)
    if layer is None:
        return pl.BlockSpec(shape, lambda *_: (0,) * nd, pipeline_mode=pl.Buffered(1))
    return pl.BlockSpec((None,) + tuple(shape), lambda *_: (layer,) + (0,) * nd,
                        pipeline_mode=pl.Buffered(1))


def _dot(a, b):
    return jnp.dot(a, b, preferred_element_type=F32)


def _layer_norm(z, g, b):
    mu = jnp.mean(z, axis=-1, keepdims=True)
    zc = z - mu
    var = jnp.mean(zc * zc, axis=-1, keepdims=True)
    return zc * lax.rsqrt(var + LN_EPS) * g + b


def _rms_norm(z, g):
    return z * lax.rsqrt(jnp.mean(z * z, axis=-1, keepdims=True) + RMS_EPS) * g


def _sigmoid(z):
    return jax.nn.sigmoid(z)


def _mlp_residual(x, w1_ref, w2_ref):
    xb = x.astype(BF16)
    acc = ALPHA * x
    for c in range(w1_ref.shape[1] // FF_CHUNK):
        a = _dot(xb, w1_ref[:, c * FF_CHUNK:(c + 1) * FF_CHUNK])
        a = jnp.square(jnp.maximum(a, 0.0)).astype(BF16)
        acc = acc + _dot(a, w2_ref[c * FF_CHUNK:(c + 1) * FF_CHUNK, :])
    return acc


def _mlp_kernel(*refs, pre_proj, sub):
    if pre_proj:
        h_ref, y_ref, w_o_ref, g0_ref, b0_ref, w1_ref, w2_ref, g_ref, b_ref, o_ref = refs
    else:
        h_ref, w1_ref, w2_ref, g_ref, b_ref, o_ref = refs
    parts = range(h_ref.shape[0] // sub)
    accs = []
    for p in parts:
        x = h_ref[p * sub:(p + 1) * sub, :]
        if pre_proj:
            mix = ALPHA * x + _dot(y_ref[p * sub:(p + 1) * sub, :], w_o_ref[...])
            x = _layer_norm(mix, g0_ref[...], b0_ref[...])
        accs.append(_mlp_residual(x, w1_ref, w2_ref))
    for p in parts:
        o_ref[p * sub:(p + 1) * sub, :] = _layer_norm(accs[p], g_ref[...], b_ref[...])


def _mlp_layer(h, w1_stack, w2_stack, layer, g, b, pre=None):
    m, d = h.shape
    d_ff = w1_stack.shape[2]
    tile, sub = (MLP_TILE, MLP_SUB) if pre is None else (MLP_PRE_TILE, MLP_PRE_TILE)
    rows = lambda width: pl.BlockSpec((tile, width), lambda i: (i, 0))
    operands = [h]
    in_specs = [rows(d)]
    if pre is not None:
        y, w_o, g0, b0 = pre
        operands += [y, w_o, g0.reshape(1, d), b0.reshape(1, d)]
        in_specs += [rows(y.shape[1]), _resident(w_o.shape), _resident((1, d)), _resident((1, d))]
    operands += [w1_stack, w2_stack, g.reshape(1, d), b.reshape(1, d)]
    in_specs += [_resident((d, d_ff), layer), _resident((d_ff, d), layer),
                 _resident((1, d)), _resident((1, d))]
    return pl.pallas_call(
        functools.partial(_mlp_kernel, pre_proj=pre is not None, sub=sub),
        out_shape=jax.ShapeDtypeStruct((m, d), F32),
        grid=(m // tile,),
        in_specs=in_specs,
        out_specs=rows(d),
        compiler_params=pltpu.CompilerParams(
            dimension_semantics=("parallel",), vmem_limit_bytes=V7X_VMEM_LIMIT),
        name="mlp_ln",
    )(*operands)


def _ret_kernel(lg_ref, x_ref, pos_ref, inv_ref, w_in_ref, gn_ref, w_o_ref, g_ref, b_ref,
                o_ref, state_ref, gate_ref):
    heads, dk, dv = state_ref.shape
    half = dk // 2
    chunk = RET_CHUNK
    cs = range(x_ref.shape[0] // chunk)
    hs = range(heads)

    @pl.when(pl.program_id(1) == 0)
    def _():
        state_ref[...] = jnp.zeros_like(state_ref)

    def rows(ref, c):
        return ref[c * chunk:(c + 1) * chunk, :]

    k_off = heads * dk
    v_off = 2 * heads * dk
    g_off = v_off + heads * dv
    q_raw, k_raw, vb = [], [], []
    for c in cs:
        xb = rows(x_ref, c).astype(BF16)
        q_raw.append([_dot(xb, w_in_ref[:, h * dk:(h + 1) * dk]) for h in hs])
        k_raw.append([_dot(xb, w_in_ref[:, k_off + h * dk:k_off + (h + 1) * dk]) for h in hs])
        vb.append([_dot(xb, w_in_ref[:, v_off + h * dv:v_off + (h + 1) * dv]).astype(BF16)
                   for h in hs])
        gate_ref[c * chunk:(c + 1) * chunk, :] = _dot(xb, w_in_ref[:, g_off:g_off + heads * dv])

    lgs = [lg_ref[h] for h in hs]
    row = lax.broadcasted_iota(jnp.int32, (chunk, chunk), 0)
    col = lax.broadcasted_iota(jnp.int32, (chunk, chunk), 1)
    diff = jnp.maximum(row - col, 0).astype(F32)
    idx = lax.broadcasted_iota(jnp.int32, (chunk, 1), 0).astype(F32)
    intra = [jnp.where(row >= col, jnp.exp(lgs[h] * diff), 0.0) for h in hs]
    q_dec = [jnp.exp(lgs[h] * (idx + 1.0)) for h in hs]
    idx_t = lax.broadcasted_iota(jnp.int32, (1, chunk), 1).astype(F32)
    k_dec_t = [jnp.exp(lgs[h] * (chunk - 1.0 - idx_t)) for h in hs]
    chunk_dec = [jnp.exp(lgs[h] * chunk) for h in hs]

    state = [state_ref[h] for h in hs]
    outs = []
    for c in cs:
        ang = rows(pos_ref, c) * inv_ref[...]
        cos = jnp.cos(ang)
        sin = jnp.sin(ang)

        def rope(t):
            t1, t2 = t[:, :half], t[:, half:]
            return jnp.concatenate([t1 * cos - t2 * sin, t1 * sin + t2 * cos], axis=-1)

        q = [rope(t) for t in q_raw[c]]
        k_t = [(rope(t) * (dk ** -0.5)).T for t in k_raw[c]]
        old = [state[h].astype(BF16) for h in hs]

        def chunk_out(h, scores_h):
            return _dot(jnp.concatenate([(scores_h * intra[h]).astype(BF16),
                                         (q[h] * q_dec[h]).astype(BF16)], axis=1),
                        jnp.concatenate([vb[c][h], old[h]], axis=0))

        out_c, prev_scores = [], None
        for h in hs:
            scores_h = _dot(q[h].astype(BF16), k_t[h].astype(BF16))
            if prev_scores is not None:
                out_c.append(chunk_out(h - 1, prev_scores))
            state[h] = state[h] * chunk_dec[h] + _dot((k_t[h] * k_dec_t[h]).astype(BF16),
                                                      vb[c][h])
            prev_scores = scores_h
        out_c.append(chunk_out(heads - 1, prev_scores))
        outs.append(out_c)
    for h in hs:
        state_ref[h] = state[h]

    for c in cs:
        mix = ALPHA * rows(x_ref, c)
        for h in hs:
            o = outs[c][h]
            mu = jnp.mean(o, axis=-1, keepdims=True)
            oc = o - mu
            var = jnp.mean(oc * oc, axis=-1, keepdims=True)
            y = oc * lax.rsqrt(var + GN_EPS) * gn_ref[:, h * dv:(h + 1) * dv]
            gate = gate_ref[c * chunk:(c + 1) * chunk, h * dv:(h + 1) * dv]
            y = gate * _sigmoid(gate) * y
            mix = mix + _dot(y.astype(BF16), w_o_ref[h * dv:(h + 1) * dv, :])
        o_ref[c * chunk:(c + 1) * chunk, :] = _layer_norm(mix, g_ref[...], b_ref[...])


def _retention_layer(h, pos, w_in_stack, gn_g, w_o_stack, layer, g, b):
    bsz, seq, d = h.shape
    heads = RET_HEADS
    dk = d // heads
    dv = 2 * dk
    n_in = w_in_stack.shape[2]
    log_gamma = jnp.log1p(-jnp.exp2(-5.0 - jnp.arange(heads, dtype=F32)))
    half = dk // 2
    inv = (ROPE_BASE ** (-jnp.arange(half, dtype=F32) / half)).reshape(1, half)
    tile = RET_TILE
    grid_spec = pltpu.PrefetchScalarGridSpec(
        num_scalar_prefetch=1,
        grid=(bsz, seq // tile),
        in_specs=[
            pl.BlockSpec((None, tile, d), lambda bi, ci, lg: (bi, ci, 0)),
            pl.BlockSpec((None, tile, 1), lambda bi, ci, lg: (bi, ci, 0)),
            _resident((1, half)),
            _resident((d, n_in), layer),
            _resident((1, heads * dv)),
            _resident((heads * dv, d), layer),
            _resident((1, d)),
            _resident((1, d)),
        ],
        out_specs=pl.BlockSpec((None, tile, d), lambda bi, ci, lg: (bi, ci, 0)),
        scratch_shapes=[pltpu.VMEM((heads, dk, dv), F32), pltpu.VMEM((tile, heads * dv), F32)],
    )
    return pl.pallas_call(
        _ret_kernel,
        out_shape=jax.ShapeDtypeStruct((bsz, seq, d), F32),
        grid_spec=grid_spec,
        compiler_params=pltpu.CompilerParams(
            dimension_semantics=("parallel", "arbitrary"), vmem_limit_bytes=V7X_VMEM_LIMIT),
        name="retention_layer",
    )(log_gamma, h, pos, inv, w_in_stack, gn_g.reshape(1, -1), w_o_stack,
      g.reshape(1, d), b.reshape(1, d))


def _gelu_tanh(z):
    c = math.sqrt(2.0 / math.pi)
    return z * _sigmoid((2.0 * c) * (z + 0.044715 * (z * z * z)))


def _softplus(z):
    return jnp.maximum(z, 0.0) + jnp.log1p(jnp.exp(-jnp.abs(z)))


def _lru_kernel(x_ref, w_in_ref, conv_w_ref, conv_b_ref, w_gate_ref, b_a_ref, b_x_ref,
                lam_ref, w_o_ref, g_ref, b_ref, w1_ref, w2_ref, g2_ref, b2_ref,
                o_ref, tail_ref, h_ref, pend_ref, *, steps_per_seq):
    sub = LRU_SUB
    ts = range(x_ref.shape[0] // sub)
    width = conv_b_ref.shape[1]
    npair, pair_dim, _ = w_gate_ref.shape
    bdim = pair_dim // 2
    slab = 8
    blk = pl.program_id(0)

    @pl.when(blk % steps_per_seq == 0)
    def _():
        tail_ref[...] = jnp.zeros_like(tail_ref)
        h_ref[...] = jnp.zeros_like(h_ref)

    @pl.when(blk == 0)
    def _():
        pend_ref[...] = jnp.zeros_like(pend_ref)

    pending = [pend_ref[t * sub:(t + 1) * sub, :] for t in ts]

    xb = [x_ref[t * sub:(t + 1) * sub, :].astype(BF16) for t in ts]
    rec = [_dot(xb[t], w_in_ref[:, width:]) for t in ts]
    gate_raw = [_dot(xb[t], w_in_ref[:, :width]) for t in ts]

    row8 = lax.broadcasted_iota(jnp.int32, (slab, width), 0)
    tail = tail_ref[...]
    us = []
    for t in ts:
        u = rec[t] * conv_w_ref[CONV_WIDTH - 1:CONV_WIDTH, :] + conv_b_ref[...]
        for s in range(1, CONV_WIDTH):
            shifted = pltpu.roll(rec[t], s, 0)
            head = jnp.where(row8 < s, pltpu.roll(tail, s, 0), shifted[:slab])
            shifted = jnp.concatenate([head, shifted[slab:]], axis=0)
            u = u + shifted * conv_w_ref[CONV_WIDTH - 1 - s:CONV_WIDTH - s, :]
        tail = rec[t][sub - slab:, :]
        us.append(u)
    tail_ref[...] = tail

    def pending_mlp(t):
        o_ref[t * sub:(t + 1) * sub, :] = _layer_norm(
            _mlp_residual(pending[t], w1_ref, w2_ref), g2_ref[...], b2_ref[...])

    pre_a, pre_x = [], []
    for t in ts:
        ub = us[t].astype(BF16)
        both = [_dot(ub[:, p * pair_dim:(p + 1) * pair_dim], w_gate_ref[p]) for p in range(npair)]
        pre_a.append(jnp.concatenate(
            [o[:, k * bdim:(k + 1) * bdim] for o in both for k in (0, 2)], axis=-1))
        pre_x.append(jnp.concatenate(
            [o[:, k * bdim:(k + 1) * bdim] for o in both for k in (1, 3)], axis=-1))
    for t in ts:
        pending_mlp(t)

    neg_softplus_lam = -_softplus(-lam_ref[...])
    carry = h_ref[...]
    ys = []
    for t in ts:
        r = _sigmoid(pre_a[t] + b_a_ref[...])
        i_gate = _sigmoid(pre_x[t] + b_x_ref[...])
        log_a = LRU_C * r * neg_softplus_lam
        a = jnp.exp(log_a)
        b_in = jnp.sqrt(1.0 - jnp.exp(2.0 * log_a)) * (i_gate * us[t])
        a3 = a.reshape(sub // slab, slab, width)
        b3 = b_in.reshape(sub // slab, slab, width)
        rows = lax.broadcasted_iota(jnp.int32, a3.shape, 1)
        step = 1
        while step < slab:
            a_sh = pltpu.roll(a3, step, 1)
            b_sh = pltpu.roll(b3, step, 1)
            live = rows >= step
            b3 = jnp.where(live, a3 * b_sh + b3, b3)
            a3 = jnp.where(live, a3 * a_sh, a3)
            step *= 2
        slabs = []
        for i in range(sub // slab):
            hs_i = a3[i] * carry + b3[i]
            carry = hs_i[slab - 1:slab, :]
            slabs.append(hs_i)
        hs = jnp.concatenate(slabs, axis=0)
        ys.append((hs * _gelu_tanh(gate_raw[t])).astype(BF16))
    h_ref[...] = carry

    for t in ts:
        mix = ALPHA * x_ref[t * sub:(t + 1) * sub, :] + _dot(ys[t], w_o_ref[...])
        pend_ref[t * sub:(t + 1) * sub, :] = _layer_norm(mix, g_ref[...], b_ref[...])


def _rglru_layer(h, w_in, conv_w, conv_b, w_a, b_a, w_x, b_x, lam, w_o, g, b,
                 w1_stack, w2_stack, layer, g2, b2):
    bsz, seq, d = h.shape
    d_ff = w1_stack.shape[2]
    width = conv_b.shape[0]
    tile = LRU_TILE
    row = lambda t: t.reshape(1, -1)
    per_seq = seq // tile
    blocks = bsz * per_seq
    zero = jnp.zeros_like(w_a[0::2])
    w_gate = jnp.concatenate(
        [jnp.concatenate([w_a[0::2], w_x[0::2], zero, zero], axis=-1),
         jnp.concatenate([zero, zero, w_a[1::2], w_x[1::2]], axis=-1)], axis=1)

    def mixer_block(s):
        s = jnp.minimum(s, blocks - 1)
        return s // per_seq, s % per_seq, 0

    def mlp_block(s):
        s = jnp.maximum(s - 1, 0)
        return s // per_seq, s % per_seq, 0

    return pl.pallas_call(
        functools.partial(_lru_kernel, steps_per_seq=per_seq),
        out_shape=jax.ShapeDtypeStruct((bsz, seq, d), F32),
        grid=(blocks + 1,),
        in_specs=[
            pl.BlockSpec((None, tile, d), mixer_block),
            _resident((d, 2 * width)),
            _resident((CONV_WIDTH, width)),
            _resident((1, width)),
            _resident(w_gate.shape),
            _resident((1, width)),
            _resident((1, width)),
            _resident((1, width)),
            _resident((width, d)),
            _resident((1, d)),
            _resident((1, d)),
            _resident((d, d_ff), layer),
            _resident((d_ff, d), layer),
            _resident((1, d)),
            _resident((1, d)),
        ],
        out_specs=pl.BlockSpec((None, tile, d), mlp_block),
        scratch_shapes=[pltpu.VMEM((8, width), F32), pltpu.VMEM((1, width), F32),
                        pltpu.VMEM((tile, d), F32)],
        compiler_params=pltpu.CompilerParams(
            dimension_semantics=("arbitrary",), vmem_limit_bytes=V7X_VMEM_LIMIT),
        name="rglru_layer",
    )(h, w_in.astype(BF16), conv_w, row(conv_b), w_gate.astype(BF16), row(b_a),
      row(b_x), row(lam), w_o.astype(BF16), row(g), row(b),
      w1_stack, w2_stack, row(g2), row(b2))


def _mla_proj_kernel(x_ref, pos_ref, inv_ref, w_down_ref, w_kpe_ref, qn_g_ref, kvn_g_ref,
                     w_uq_ref, w_ukv_ref, q_ref, k_ref, v_ref):
    heads = q_ref.shape[0]
    nope, lanes = MLA_NOPE, 2 * MLA_ROPE
    q_rank = qn_g_ref.shape[1]
    scale = (MLA_NOPE + MLA_ROPE) ** -0.5 * LOG2_E
    sub = MLA_PROJ_SUB
    ts = range(x_ref.shape[0] // sub)
    pairs = heads // 2
    pe_off = heads * nope
    rot_off = pe_off + pairs * lanes
    v_off = heads * nope
    pad_rows = v_ref.shape[1] - MLA_V
    ones_row = jnp.where(lax.broadcasted_iota(jnp.int32, (pad_rows, v_ref.shape[2]), 0) == 0,
                         1.0, 0.0).astype(BF16)

    xb = [x_ref[t * sub:(t + 1) * sub, :].astype(BF16) for t in ts]
    down = [_dot(xb[t], w_down_ref[...]) for t in ts]
    kpe = [_dot(xb[t], w_kpe_ref[...]) for t in ts]
    c_q = [_rms_norm(down[t][:, :q_rank], qn_g_ref[...]).astype(BF16) for t in ts]
    c_kv = [_rms_norm(down[t][:, q_rank:], kvn_g_ref[...]).astype(BF16) for t in ts]
    uq = [_dot(c_q[t], w_uq_ref[...]) for t in ts]
    ukv = [_dot(c_kv[t], w_ukv_ref[...]) for t in ts]

    for t in ts:
        rows = slice(t * sub, (t + 1) * sub)
        ang = pos_ref[rows, :] * inv_ref[...]
        cos = jnp.cos(ang)
        sin = jnp.sin(ang)
        k_pe = [(kpe[t][:, par * lanes:(par + 1) * lanes] * cos
                 + kpe[t][:, (2 + par) * lanes:(3 + par) * lanes] * sin).astype(BF16)
                for par in (0, 1)]
        q_pe_t = [((uq[t][:, pe_off + g * lanes:pe_off + (g + 1) * lanes] * cos
                    + uq[t][:, rot_off + g * lanes:rot_off + (g + 1) * lanes] * sin) * scale
                   ).T.astype(BF16) for g in range(pairs)]
        for h in range(heads):
            q_ref[h, :nope, rows] = (uq[t][:, h * nope:(h + 1) * nope] * scale).T.astype(BF16)
            q_ref[h, nope:, rows] = q_pe_t[h // 2]
            k_ref[h, rows, :nope] = ukv[t][:, h * nope:(h + 1) * nope].astype(BF16)
            k_ref[h, rows, nope:] = k_pe[h % 2]
            v_ref[h, :MLA_V, rows] = (
                ukv[t][:, v_off + h * MLA_V:v_off + (h + 1) * MLA_V].T.astype(BF16))
    for h in range(heads):
        v_ref[h, MLA_V:, :] = ones_row


def _attn_kernel(q_ref, k_ref, vt_ref, o_ref, s_ref):
    i = pl.program_id(2)
    tq, dv = o_ref.shape
    kb_keys = vt_ref.shape[2]
    sub = tk = ATTN_SUB
    nsub = tq // sub
    kpq = tq // tk
    per_kb = kb_keys // tk
    qs = [q_ref[:, s * sub:(s + 1) * sub] for s in range(nsub)]

    def diag_vt(c):
        return vt_ref[i * (kpq // per_kb) + c // per_kb][:, (c % per_kb) * tk:
                                                         (c % per_kb + 1) * tk]

    def scores(j, slot):
        kb = k_ref[pl.ds(pl.multiple_of(j * kb_keys, kb_keys), kb_keys), :]
        maxima = []
        for s in range(nsub):
            st = _dot(kb, qs[s])
            s_ref[slot, s] = st
            maxima.append(jnp.max(st, axis=0, keepdims=True))
        return tuple(maxima)

    def absorb(j, slot, stats, maxima):
        vt = vt_ref[j]
        ps, scaled = [], []
        for s, ((m_prev, acc), m_blk) in enumerate(zip(stats, maxima)):
            m_new = jnp.maximum(m_prev, m_blk)
            ps.append(jnp.exp2(s_ref[slot, s] - m_new).astype(BF16))
            scaled.append((m_new, jnp.exp2(m_prev - m_new) * acc))
        return tuple((m, a + _dot(vt, p)) for (m, a), p in zip(scaled, ps))

    tri = (lax.broadcasted_iota(jnp.int32, (tk, sub), 0)
           <= lax.broadcasted_iota(jnp.int32, (tk, sub), 1))
    diag_scores = {}
    for c in range(kpq):
        kb = k_ref[pl.ds(pl.multiple_of(i * tq + c * tk, tk), tk), :]
        for s in range(c, nsub):
            st = _dot(kb, qs[s])
            diag_scores[c, s] = jnp.where(tri, st, -jnp.inf) if c == s else st
    first_maxima = scores(0, 0)
    diag_p = {}
    diag_m = []
    for s in range(nsub):
        m0 = jnp.max(diag_scores[0, s], axis=0, keepdims=True)
        for c in range(1, s + 1):
            m0 = jnp.maximum(m0, jnp.max(diag_scores[c, s], axis=0, keepdims=True))
        for c in range(s + 1):
            diag_p[c, s] = jnp.exp2(diag_scores[c, s] - m0).astype(BF16)
        diag_m.append(m0)
    stats = []
    for s in range(nsub):
        acc = _dot(diag_vt(0), diag_p[0, s])
        for c in range(1, s + 1):
            acc = acc + _dot(diag_vt(c), diag_p[c, s])
        stats.append((diag_m[s], acc))

    nslot = s_ref.shape[0]
    bpq = tq // kb_keys

    def body(t, carry):
        stats, maxima = carry
        for c in range(nslot):
            cur = t * nslot + c
            nxt = jnp.minimum(cur + 1, i * bpq - 1)
            kb = k_ref[pl.ds(pl.multiple_of(nxt * kb_keys, kb_keys), kb_keys), :]
            vt = vt_ref[cur]
            new_stats, maxima_next = [], []
            for s in range(nsub):
                st = _dot(kb, qs[s])
                s_ref[(c + 1) % nslot, s] = st
                maxima_next.append(jnp.max(st, axis=0, keepdims=True))
                m_prev, acc = stats[s]
                m_new = jnp.maximum(m_prev, maxima[s])
                p = jnp.exp2(s_ref[c, s] - m_new).astype(BF16)
                new_stats.append((m_new, jnp.exp2(m_prev - m_new) * acc + _dot(vt, p)))
            stats, maxima = tuple(new_stats), tuple(maxima_next)
        return stats, maxima

    stats, _ = lax.fori_loop(0, i * (bpq // nslot), body, (tuple(stats), first_maxima))
    for s in range(nsub):
        _, acc = stats[s]
        o_ref[s * sub:(s + 1) * sub, :] = (acc[:dv] / acc[dv:dv + 1]).T.astype(o_ref.dtype)


def _rot_half_cols(w):
    half = w.shape[-1] // 2
    return jnp.concatenate([-w[..., half:], w[..., :half]], axis=-1)


def _mla_layer(h, pos, w_in, q_norm_g, kv_norm_g, w_uq, w_ukv):
    bsz, seq, d = h.shape
    heads, nope, rope_d, vd = MLA_HEADS, MLA_NOPE, MLA_ROPE, MLA_V
    dqk = nope + 2 * rope_d
    half = rope_d // 2
    inv = ROPE_BASE ** (-jnp.arange(half, dtype=F32) / half)
    inv = jnp.tile(inv, 4).reshape(1, 2 * rope_d)

    w_down = w_in[:, :MLA_Q_RANK + MLA_KV_RANK]
    w_pe = w_in[:, MLA_Q_RANK + MLA_KV_RANK:]
    zero = jnp.zeros_like(w_pe)
    w_pe_rot = _rot_half_cols(w_pe)
    w_kpe = jnp.concatenate([w_pe, zero, zero, w_pe, w_pe_rot, zero, zero, w_pe_rot], axis=1)
    w_uq3 = w_uq.reshape(MLA_Q_RANK, heads, nope + rope_d)
    w_qp = w_uq3[:, :, nope:]
    w_uq_all = jnp.concatenate(
        [w_uq3[:, :, :nope].reshape(MLA_Q_RANK, heads * nope),
         w_qp.reshape(MLA_Q_RANK, heads * rope_d),
         _rot_half_cols(w_qp).reshape(MLA_Q_RANK, heads * rope_d)], axis=1)
    w_ukv3 = w_ukv.reshape(MLA_KV_RANK, heads, nope + vd)
    w_ukv_all = jnp.concatenate(
        [w_ukv3[:, :, :nope].reshape(MLA_KV_RANK, heads * nope),
         w_ukv3[:, :, nope:].reshape(MLA_KV_RANK, heads * vd)], axis=1)

    tile = MLA_PROJ_TILE
    per_tk = ATTN_TK // tile
    bf = lambda t: t.astype(BF16)
    q, k, v = pl.pallas_call(
        _mla_proj_kernel,
        out_shape=(jax.ShapeDtypeStruct((bsz, heads, dqk, seq), BF16),
                   jax.ShapeDtypeStruct((bsz, heads, seq, dqk), BF16),
                   jax.ShapeDtypeStruct((bsz, heads, seq // ATTN_TK, vd + ATTN_VT_PAD, ATTN_TK), BF16)),
        grid=(bsz, seq // tile),
        in_specs=[
            pl.BlockSpec((None, tile, d), lambda bi, ti: (bi, ti, 0)),
            pl.BlockSpec((None, tile, 1), lambda bi, ti: (bi, ti, 0)),
            _resident((1, 2 * rope_d)),
            _resident(w_down.shape), _resident(w_kpe.shape),
            _resident((1, MLA_Q_RANK)), _resident((1, MLA_KV_RANK)),
            _resident(w_uq_all.shape), _resident(w_ukv_all.shape),
        ],
        out_specs=(pl.BlockSpec((None, heads, dqk, tile), lambda bi, ti: (bi, 0, 0, ti)),
                   pl.BlockSpec((None, heads, tile, dqk), lambda bi, ti: (bi, 0, ti, 0)),
                   pl.BlockSpec((None, heads, None, vd + ATTN_VT_PAD, tile),
                                lambda bi, ti: (bi, 0, ti // per_tk, 0, ti % per_tk))),
        compiler_params=pltpu.CompilerParams(
            dimension_semantics=("parallel", "parallel"), vmem_limit_bytes=V7X_VMEM_LIMIT),
        name="mla_proj",
    )(h, pos, inv, bf(w_down), bf(w_kpe), q_norm_g.reshape(1, -1), kv_norm_g.reshape(1, -1),
      bf(w_uq_all), bf(w_ukv_all))

    attn = pl.pallas_call(
        _attn_kernel,
        out_shape=jax.ShapeDtypeStruct((bsz, seq, heads * vd), BF16),
        grid=(bsz, heads, seq // ATTN_TQ),
        in_specs=[
            pl.BlockSpec((None, None, dqk, ATTN_TQ), lambda bi, hi, qi: (bi, hi, 0, qi)),
            pl.BlockSpec((None, None, seq, dqk), lambda bi, hi, qi: (bi, hi, 0, 0)),
            pl.BlockSpec((None, None, seq // ATTN_TK, vd + ATTN_VT_PAD, ATTN_TK),
                         lambda bi, hi, qi: (bi, hi, 0, 0, 0)),
        ],
        out_specs=pl.BlockSpec((None, ATTN_TQ, vd), lambda bi, hi, qi: (bi, qi, hi)),
        scratch_shapes=[pltpu.VMEM((ATTN_SLOTS, ATTN_TQ // ATTN_SUB, ATTN_TK, ATTN_SUB),
                                   F32)],
        compiler_params=pltpu.CompilerParams(
            dimension_semantics=("parallel", "parallel", "arbitrary"),
            vmem_limit_bytes=V7X_VMEM_LIMIT),
        name="mla_attention",
    )(q, k, v)

    return attn.reshape(bsz * seq, heads * vd)


def kernel(x, positions, ret_w_in, ret_gn_g, ret_w_o, lru_w_in, lru_conv_w, lru_conv_b, lru_w_a,
           lru_b_a, lru_w_x, lru_b_x, lru_lam, lru_w_o, mla_w_in, mla_q_norm, mla_kv_norm,
           mla_w_uq, mla_w_ukv, mla_w_o, ln_g, ln_b, mlp_w1, mlp_w2):
    bsz, seq, d = x.shape
    pos = positions.astype(F32).reshape(bsz, seq, 1)
    ret_w_in_b, ret_w_o_b = ret_w_in.astype(BF16), ret_w_o.astype(BF16)
    mlp_w1_b, mlp_w2_b = mlp_w1.astype(BF16), mlp_w2.astype(BF16)
    h = x
    for i in range(DEPTH):
        kind, j = i % N_MIXERS, i // N_MIXERS
        pre = None
        if kind == 0:
            h = _retention_layer(h, pos, ret_w_in_b, ret_gn_g[j], ret_w_o_b, j,
                                 ln_g[i, 0], ln_b[i, 0])
        elif kind == 1:
            h = _rglru_layer(h, lru_w_in[j], lru_conv_w[j], lru_conv_b[j], lru_w_a[j],
                             lru_b_a[j], lru_w_x[j], lru_b_x[j], lru_lam[j], lru_w_o[j],
                             ln_g[i, 0], ln_b[i, 0], mlp_w1_b, mlp_w2_b, i,
                             ln_g[i, 1], ln_b[i, 1])
            continue
        else:
            y = _mla_layer(h, pos, mla_w_in[j], mla_q_norm[j], mla_kv_norm[j], mla_w_uq[j],
                           mla_w_ukv[j])
            pre = (y, mla_w_o[j].astype(BF16), ln_g[i, 0], ln_b[i, 0])
        h = _mlp_layer(h.reshape(bsz * seq, d), mlp_w1_b, mlp_w2_b, i,
                       ln_g[i, 1], ln_b[i, 1], pre=pre).reshape(bsz, seq, d)
    return h
```

```python
import functools
import math

import jax
import jax.numpy as jnp
from jax import lax
from jax.experimental import pallas as pl
from jax.experimental.pallas import tpu as pltpu

F32 = jnp.float32
BF16 = jnp.bfloat16

DEPTH = 4
N_MIXERS = 3
RET_HEADS = 4
GN_EPS = 1e-5
CONV_WIDTH = 4
LRU_C = 8.0
MLA_HEADS = 8
MLA_NOPE = 128
MLA_ROPE = 64
MLA_V = 128
MLA_Q_RANK = 384
MLA_KV_RANK = 256
ROPE_BASE = 10000.0
LN_EPS = 1e-5
RMS_EPS = 1e-6
ALPHA = (2.0 * DEPTH) ** 0.25
LOG2_E = math.log2(math.e)

RET_CHUNK = 256
RET_TILE = 512
LRU_TILE = 256
LRU_SUB = 256
MLP_TILE = 1024
MLP_SUB = 256
MLP_PRE_TILE = 512
MLA_PROJ_TILE = 512
MLA_PROJ_SUB = 256
ATTN_TQ = 1024
ATTN_SUB = 256
ATTN_TK = 512
ATTN_SLOTS = 2
ATTN_VT_PAD = 16
FF_CHUNK = 1024

V7X_VMEM_LIMIT = 56 * 1024 * 1024


def _resident(shape, layer=None):
    nd = len(shape)
    if layer is None:
        return pl.BlockSpec(shape, lambda *_: (0,) * nd, pipeline_mode=pl.Buffered(1))
    return pl.BlockSpec((None,) + tuple(shape), lambda *_: (layer,) + (0,) * nd,
                        pipeline_mode=pl.Buffered(1))


def _dot(a, b):
    return jnp.dot(a, b, preferred_element_type=F32)


def _layer_norm(z, g, b):
    mu = jnp.mean(z, axis=-1, keepdims=True)
    zc = z - mu
    var = jnp.mean(zc * zc, axis=-1, keepdims=True)
    return zc * lax.rsqrt(var + LN_EPS) * g + b


def _rms_norm(z, g):
    return z * lax.rsqrt(jnp.mean(z * z, axis=-1, keepdims=True) + RMS_EPS) * g


def _sigmoid(z):
    return jax.nn.sigmoid(z)


def _mlp_residual(x, w1_ref, w2_ref):
    xb = x.astype(BF16)
    acc = ALPHA * x
    for c in range(w1_ref.shape[1] // FF_CHUNK):
        a = _dot(xb, w1_ref[:, c * FF_CHUNK:(c + 1) * FF_CHUNK])
        a = jnp.square(jnp.maximum(a, 0.0)).astype(BF16)
        acc = acc + _dot(a, w2_ref[c * FF_CHUNK:(c + 1) * FF_CHUNK, :])
    return acc


def _mlp_kernel(*refs, pre_proj, sub):
    if pre_proj:
        h_ref, y_ref, w_o_ref, g0_ref, b0_ref, w1_ref, w2_ref, g_ref, b_ref, o_ref = refs
    else:
        h_ref, w1_ref, w2_ref, g_ref, b_ref, o_ref = refs
    parts = range(h_ref.shape[0] // sub)
    accs = []
    for p in parts:
        x = h_ref[p * sub:(p + 1) * sub, :]
        if pre_proj:
            mix = ALPHA * x + _dot(y_ref[p * sub:(p + 1) * sub, :], w_o_ref[...])
            x = _layer_norm(mix, g0_ref[...], b0_ref[...])
        accs.append(_mlp_residual(x, w1_ref, w2_ref))
    for p in parts:
        o_ref[p * sub:(p + 1) * sub, :] = _layer_norm(accs[p], g_ref[...], b_ref[...])


def _mlp_layer(h, w1_stack, w2_stack, layer, g, b, pre=None):
    m, d = h.shape
    d_ff = w1_stack.shape[2]
    tile, sub = (MLP_TILE, MLP_SUB) if pre is None else (MLP_PRE_TILE, MLP_PRE_TILE)
    rows = lambda width: pl.BlockSpec((tile, width), lambda i: (i, 0))
    operands = [h]
    in_specs = [rows(d)]
    if pre is not None:
        y, w_o, g0, b0 = pre
        operands += [y, w_o, g0.reshape(1, d), b0.reshape(1, d)]
        in_specs += [rows(y.shape[1]), _resident(w_o.shape), _resident((1, d)), _resident((1, d))]
    operands += [w1_stack, w2_stack, g.reshape(1, d), b.reshape(1, d)]
    in_specs += [_resident((d, d_ff), layer), _resident((d_ff, d), layer),
                 _resident((1, d)), _resident((1, d))]
    return pl.pallas_call(
        functools.partial(_mlp_kernel, pre_proj=pre is not None, sub=sub),
        out_shape=jax.ShapeDtypeStruct((m, d), F32),
        grid=(m // tile,),
        in_specs=in_specs,
        out_specs=rows(d),
        compiler_params=pltpu.CompilerParams(
            dimension_semantics=("parallel",), vmem_limit_bytes=V7X_VMEM_LIMIT),
        name="mlp_ln",
    )(*operands)


def _ret_kernel(lg_ref, x_ref, pos_ref, inv_ref, w_in_ref, gn_ref, w_o_ref, g_ref, b_ref,
                o_ref, state_ref):
    heads, dk, dv = state_ref.shape
    half = dk // 2
    chunk = RET_CHUNK
    cs = range(x_ref.shape[0] // chunk)
    hs = range(heads)

    @pl.when(pl.program_id(1) == 0)
    def _():
        state_ref[...] = jnp.zeros_like(state_ref)

    def rows(ref, c):
        return ref[c * chunk:(c + 1) * chunk, :]

    k_off = heads * dk
    v_off = 2 * heads * dk
    g_off = v_off + heads * dv
    q_raw, k_raw, vb, gates = [], [], [], []
    for c in cs:
        xb = rows(x_ref, c).astype(BF16)
        q_raw.append([_dot(xb, w_in_ref[:, h * dk:(h + 1) * dk]) for h in hs])
        k_raw.append([_dot(xb, w_in_ref[:, k_off + h * dk:k_off + (h + 1) * dk]) for h in hs])
        vb.append([_dot(xb, w_in_ref[:, v_off + h * dv:v_off + (h + 1) * dv]).astype(BF16)
                   for h in hs])
        gates.append([_dot(xb, w_in_ref[:, g_off + h * dv:g_off + (h + 1) * dv]) for h in hs])

    lgs = [lg_ref[h] for h in hs]
    row = lax.broadcasted_iota(jnp.int32, (chunk, chunk), 0)
    col = lax.broadcasted_iota(jnp.int32, (chunk, chunk), 1)
    diff = jnp.maximum(row - col, 0).astype(F32)
    idx = lax.broadcasted_iota(jnp.int32, (chunk, 1), 0).astype(F32)
    intra = [jnp.where(row >= col, jnp.exp(lgs[h] * diff), 0.0) for h in hs]
    q_dec = [jnp.exp(lgs[h] * (idx + 1.0)) for h in hs]
    idx_t = lax.broadcasted_iota(jnp.int32, (1, chunk), 1).astype(F32)
    k_dec_t = [jnp.exp(lgs[h] * (chunk - 1.0 - idx_t)) for h in hs]
    chunk_dec = [jnp.exp(lgs[h] * chunk) for h in hs]

    state = [state_ref[h] for h in hs]
    outs = []
    for c in cs:
        ang = rows(pos_ref, c) * inv_ref[...]
        cos = jnp.cos(ang)
        sin = jnp.sin(ang)

        def rope(t):
            t1, t2 = t[:, :half], t[:, half:]
            return jnp.concatenate([t1 * cos - t2 * sin, t1 * sin + t2 * cos], axis=-1)

        q = [rope(t) for t in q_raw[c]]
        k_t = [(rope(t) * (dk ** -0.5)).T for t in k_raw[c]]
        old = [state[h].astype(BF16) for h in hs]

        def chunk_out(h, scores_h):
            return _dot(jnp.concatenate([(scores_h * intra[h]).astype(BF16),
                                         (q[h] * q_dec[h]).astype(BF16)], axis=1),
                        jnp.concatenate([vb[c][h], old[h]], axis=0))

        out_c, prev_scores = [], None
        for h in hs:
            scores_h = _dot(q[h].astype(BF16), k_t[h].astype(BF16))
            if prev_scores is not None:
                out_c.append(chunk_out(h - 1, prev_scores))
            state[h] = state[h] * chunk_dec[h] + _dot((k_t[h] * k_dec_t[h]).astype(BF16),
                                                      vb[c][h])
            prev_scores = scores_h
        out_c.append(chunk_out(heads - 1, prev_scores))
        outs.append(out_c)
    for h in hs:
        state_ref[h] = state[h]

    for c in cs:
        mix = ALPHA * rows(x_ref, c)
        for h in hs:
            o = outs[c][h]
            mu = jnp.mean(o, axis=-1, keepdims=True)
            oc = o - mu
            var = jnp.mean(oc * oc, axis=-1, keepdims=True)
            y = oc * lax.rsqrt(var + GN_EPS) * gn_ref[:, h * dv:(h + 1) * dv]
            y = gates[c][h] * _sigmoid(gates[c][h]) * y
            mix = mix + _dot(y.astype(BF16), w_o_ref[h * dv:(h + 1) * dv, :])
        o_ref[c * chunk:(c + 1) * chunk, :] = _layer_norm(mix, g_ref[...], b_ref[...])


def _retention_layer(h, pos, w_in_stack, gn_g, w_o_stack, layer, g, b):
    bsz, seq, d = h.shape
    heads = RET_HEADS
    dk = d // heads
    dv = 2 * dk
    n_in = w_in_stack.shape[2]
    log_gamma = jnp.log1p(-jnp.exp2(-5.0 - jnp.arange(heads, dtype=F32)))
    half = dk // 2
    inv = (ROPE_BASE ** (-jnp.arange(half, dtype=F32) / half)).reshape(1, half)
    tile = RET_TILE
    grid_spec = pltpu.PrefetchScalarGridSpec(
        num_scalar_prefetch=1,
        grid=(bsz, seq // tile),
        in_specs=[
            pl.BlockSpec((None, tile, d), lambda bi, ci, lg: (bi, ci, 0)),
            pl.BlockSpec((None, tile, 1), lambda bi, ci, lg: (bi, ci, 0)),
            _resident((1, half)),
            _resident((d, n_in), layer),
            _resident((1, heads * dv)),
            _resident((heads * dv, d), layer),
            _resident((1, d)),
            _resident((1, d)),
        ],
        out_specs=pl.BlockSpec((None, tile, d), lambda bi, ci, lg: (bi, ci, 0)),
        scratch_shapes=[pltpu.VMEM((heads, dk, dv), F32)],
    )
    return pl.pallas_call(
        _ret_kernel,
        out_shape=jax.ShapeDtypeStruct((bsz, seq, d), F32),
        grid_spec=grid_spec,
        compiler_params=pltpu.CompilerParams(
            dimension_semantics=("parallel", "arbitrary"), vmem_limit_bytes=V7X_VMEM_LIMIT),
        name="retention_layer",
    )(log_gamma, h, pos, inv, w_in_stack, gn_g.reshape(1, -1), w_o_stack,
      g.reshape(1, d), b.reshape(1, d))


def _gelu_tanh(z):
    c = math.sqrt(2.0 / math.pi)
    return z * _sigmoid((2.0 * c) * (z + 0.044715 * (z * z * z)))


def _softplus(z):
    return jnp.maximum(z, 0.0) + jnp.log1p(jnp.exp(-jnp.abs(z)))


def _lru_kernel(x_ref, w_in_ref, conv_w_ref, conv_b_ref, w_gate_ref, b_a_ref, b_x_ref,
                lam_ref, w_o_ref, g_ref, b_ref, w1_ref, w2_ref, g2_ref, b2_ref,
                o_ref, tail_ref, h_ref, pend_ref, *, steps_per_seq):
    sub = LRU_SUB
    ts = range(x_ref.shape[0] // sub)
    width = conv_b_ref.shape[1]
    npair, pair_dim, _ = w_gate_ref.shape
    bdim = pair_dim // 2
    slab = 8
    blk = pl.program_id(0)

    @pl.when(blk % steps_per_seq == 0)
    def _():
        tail_ref[...] = jnp.zeros_like(tail_ref)
        h_ref[...] = jnp.zeros_like(h_ref)

    @pl.when(blk == 0)
    def _():
        pend_ref[...] = jnp.zeros_like(pend_ref)

    pending = [pend_ref[t * sub:(t + 1) * sub, :] for t in ts]

    xb = [x_ref[t * sub:(t + 1) * sub, :].astype(BF16) for t in ts]
    rec = [_dot(xb[t], w_in_ref[:, width:]) for t in ts]
    gate_raw = [_dot(xb[t], w_in_ref[:, :width]) for t in ts]

    row8 = lax.broadcasted_iota(jnp.int32, (slab, width), 0)
    tail = tail_ref[...]
    us = []
    for t in ts:
        u = rec[t] * conv_w_ref[CONV_WIDTH - 1:CONV_WIDTH, :] + conv_b_ref[...]
        for s in range(1, CONV_WIDTH):
            shifted = pltpu.roll(rec[t], s, 0)
            head = jnp.where(row8 < s, pltpu.roll(tail, s, 0), shifted[:slab])
            shifted = jnp.concatenate([head, shifted[slab:]], axis=0)
            u = u + shifted * conv_w_ref[CONV_WIDTH - 1 - s:CONV_WIDTH - s, :]
        tail = rec[t][sub - slab:, :]
        us.append(u)
    tail_ref[...] = tail

    def pending_mlp(t):
        o_ref[t * sub:(t + 1) * sub, :] = _layer_norm(
            _mlp_residual(pending[t], w1_ref, w2_ref), g2_ref[...], b2_ref[...])

    pre_a, pre_x = [], []
    for t in ts:
        ub = us[t].astype(BF16)
        both = [_dot(ub[:, p * pair_dim:(p + 1) * pair_dim], w_gate_ref[p]) for p in range(npair)]
        pre_a.append(jnp.concatenate(
            [o[:, k * bdim:(k + 1) * bdim] for o in both for k in (0, 2)], axis=-1))
        pre_x.append(jnp.concatenate(
            [o[:, k * bdim:(k + 1) * bdim] for o in both for k in (1, 3)], axis=-1))
    for t in ts:
        pending_mlp(t)

    neg_softplus_lam = -_softplus(-lam_ref[...])
    carry = h_ref[...]
    ys = []
    for t in ts:
        r = _sigmoid(pre_a[t] + b_a_ref[...])
        i_gate = _sigmoid(pre_x[t] + b_x_ref[...])
        log_a = LRU_C * r * neg_softplus_lam
        a = jnp.exp(log_a)
        b_in = jnp.sqrt(1.0 - jnp.exp(2.0 * log_a)) * (i_gate * us[t])
        a3 = a.reshape(sub // slab, slab, width)
        b3 = b_in.reshape(sub // slab, slab, width)
        rows = lax.broadcasted_iota(jnp.int32, a3.shape, 1)
        step = 1
        while step < slab:
            a_sh = pltpu.roll(a3, step, 1)
            b_sh = pltpu.roll(b3, step, 1)
            live = rows >= step
            b3 = jnp.where(live, a3 * b_sh + b3, b3)
            a3 = jnp.where(live, a3 * a_sh, a3)
            step *= 2
        slabs = []
        for i in range(sub // slab):
            hs_i = a3[i] * carry + b3[i]
            carry = hs_i[slab - 1:slab, :]
            slabs.append(hs_i)
        hs = jnp.concatenate(slabs, axis=0)
        ys.append((hs * _gelu_tanh(gate_raw[t])).astype(BF16))
    h_ref[...] = carry

    for t in ts:
        mix = ALPHA * x_ref[t * sub:(t + 1) * sub, :] + _dot(ys[t], w_o_ref[...])
        pend_ref[t * sub:(t + 1) * sub, :] = _layer_norm(mix, g_ref[...], b_ref[...])


def _rglru_layer(h, w_in, conv_w, conv_b, w_a, b_a, w_x, b_x, lam, w_o, g, b,
                 w1_stack, w2_stack, layer, g2, b2):
    bsz, seq, d = h.shape
    d_ff = w1_stack.shape[2]
    width = conv_b.shape[0]
    tile = LRU_TILE
    row = lambda t: t.reshape(1, -1)
    per_seq = seq // tile
    blocks = bsz * per_seq
    zero = jnp.zeros_like(w_a[0::2])
    w_gate = jnp.concatenate(
        [jnp.concatenate([w_a[0::2], w_x[0::2], zero, zero], axis=-1),
         jnp.concatenate([zero, zero, w_a[1::2], w_x[1::2]], axis=-1)], axis=1)

    def mixer_block(s):
        s = jnp.minimum(s, blocks - 1)
        return s // per_seq, s % per_seq, 0

    def mlp_block(s):
        s = jnp.maximum(s - 1, 0)
        return s // per_seq, s % per_seq, 0

    return pl.pallas_call(
        functools.partial(_lru_kernel, steps_per_seq=per_seq),
        out_shape=jax.ShapeDtypeStruct((bsz, seq, d), F32),
        grid=(blocks + 1,),
        in_specs=[
            pl.BlockSpec((None, tile, d), mixer_block),
            _resident((d, 2 * width)),
            _resident((CONV_WIDTH, width)),
            _resident((1, width)),
            _resident(w_gate.shape),
            _resident((1, width)),
            _resident((1, width)),
            _resident((1, width)),
            _resident((width, d)),
            _resident((1, d)),
            _resident((1, d)),
            _resident((d, d_ff), layer),
            _resident((d_ff, d), layer),
            _resident((1, d)),
            _resident((1, d)),
        ],
        out_specs=pl.BlockSpec((None, tile, d), mlp_block),
        scratch_shapes=[pltpu.VMEM((8, width), F32), pltpu.VMEM((1, width), F32),
                        pltpu.VMEM((tile, d), F32)],
        compiler_params=pltpu.CompilerParams(
            dimension_semantics=("arbitrary",), vmem_limit_bytes=V7X_VMEM_LIMIT),
        name="rglru_layer",
    )(h, w_in.astype(BF16), conv_w, row(conv_b), w_gate.astype(BF16), row(b_a),
      row(b_x), row(lam), w_o.astype(BF16), row(g), row(b),
      w1_stack, w2_stack, row(g2), row(b2))


def _mla_proj_kernel(x_ref, pos_ref, inv_ref, w_down_ref, w_kpe_ref, qn_g_ref, kvn_g_ref,
                     w_uq_ref, w_ukv_ref, q_ref, k_ref, v_ref):
    heads = q_ref.shape[0]
    nope, lanes = MLA_NOPE, 2 * MLA_ROPE
    q_rank = qn_g_ref.shape[1]
    scale = (MLA_NOPE + MLA_ROPE) ** -0.5 * LOG2_E
    sub = MLA_PROJ_SUB
    ts = range(x_ref.shape[0] // sub)
    pairs = heads // 2
    pe_off = heads * nope
    rot_off = pe_off + pairs * lanes
    v_off = heads * nope
    pad_rows = v_ref.shape[1] - MLA_V
    ones_row = jnp.where(lax.broadcasted_iota(jnp.int32, (pad_rows, v_ref.shape[2]), 0) == 0,
                         1.0, 0.0).astype(BF16)

    xb = [x_ref[t * sub:(t + 1) * sub, :].astype(BF16) for t in ts]
    down = [_dot(xb[t], w_down_ref[...]) for t in ts]
    kpe = [_dot(xb[t], w_kpe_ref[...]) for t in ts]
    c_q = [_rms_norm(down[t][:, :q_rank], qn_g_ref[...]).astype(BF16) for t in ts]
    c_kv = [_rms_norm(down[t][:, q_rank:], kvn_g_ref[...]).astype(BF16) for t in ts]
    uq = [_dot(c_q[t], w_uq_ref[...]) for t in ts]
    ukv = [_dot(c_kv[t], w_ukv_ref[...]) for t in ts]

    for t in ts:
        rows = slice(t * sub, (t + 1) * sub)
        ang = pos_ref[rows, :] * inv_ref[...]
        cos = jnp.cos(ang)
        sin = jnp.sin(ang)
        k_pe = [(kpe[t][:, par * lanes:(par + 1) * lanes] * cos
                 + kpe[t][:, (2 + par) * lanes:(3 + par) * lanes] * sin).astype(BF16)
                for par in (0, 1)]
        q_pe_t = [((uq[t][:, pe_off + g * lanes:pe_off + (g + 1) * lanes] * cos
                    + uq[t][:, rot_off + g * lanes:rot_off + (g + 1) * lanes] * sin) * scale
                   ).T.astype(BF16) for g in range(pairs)]
        for h in range(heads):
            q_ref[h, :nope, rows] = (uq[t][:, h * nope:(h + 1) * nope] * scale).T.astype(BF16)
            q_ref[h, nope:, rows] = q_pe_t[h // 2]
            k_ref[h, rows, :nope] = ukv[t][:, h * nope:(h + 1) * nope].astype(BF16)
            k_ref[h, rows, nope:] = k_pe[h % 2]
            v_ref[h, :MLA_V, rows] = (
                ukv[t][:, v_off + h * MLA_V:v_off + (h + 1) * MLA_V].T.astype(BF16))
    for h in range(heads):
        v_ref[h, MLA_V:, :] = ones_row


def _attn_kernel(q_ref, k_ref, vt_ref, o_ref, s_ref):
    i = pl.program_id(2)
    tq, dv = o_ref.shape
    kb_keys = vt_ref.shape[2]
    sub = tk = ATTN_SUB
    nsub = tq // sub
    kpq = tq // tk
    per_kb = kb_keys // tk
    qs = [q_ref[:, s * sub:(s + 1) * sub] for s in range(nsub)]

    def diag_vt(c):
        return vt_ref[i * (kpq // per_kb) + c // per_kb][:, (c % per_kb) * tk:
                                                         (c % per_kb + 1) * tk]

    def scores(j, slot):
        kb = k_ref[pl.ds(pl.multiple_of(j * kb_keys, kb_keys), kb_keys), :]
        maxima = []
        for s in range(nsub):
            st = _dot(kb, qs[s])
            s_ref[slot, s] = st
            maxima.append(jnp.max(st, axis=0, keepdims=True))
        return tuple(maxima)

    def absorb(j, slot, stats, maxima):
        vt = vt_ref[j]
        ps, scaled = [], []
        for s, ((m_prev, acc), m_blk) in enumerate(zip(stats, maxima)):
            m_new = jnp.maximum(m_prev, m_blk)
            ps.append(jnp.exp2(s_ref[slot, s] - m_new).astype(BF16))
            scaled.append((m_new, jnp.exp2(m_prev - m_new) * acc))
        return tuple((m, a + _dot(vt, p)) for (m, a), p in zip(scaled, ps))

    tri = (lax.broadcasted_iota(jnp.int32, (tk, sub), 0)
           <= lax.broadcasted_iota(jnp.int32, (tk, sub), 1))
    diag_scores = {}
    for c in range(kpq):
        kb = k_ref[pl.ds(pl.multiple_of(i * tq + c * tk, tk), tk), :]
        for s in range(c, nsub):
            st = _dot(kb, qs[s])
            diag_scores[c, s] = jnp.where(tri, st, -jnp.inf) if c == s else st
    first_maxima = scores(0, 0)
    diag_p = {}
    diag_m = []
    for s in range(nsub):
        m0 = jnp.max(diag_scores[0, s], axis=0, keepdims=True)
        for c in range(1, s + 1):
            m0 = jnp.maximum(m0, jnp.max(diag_scores[c, s], axis=0, keepdims=True))
        for c in range(s + 1):
            diag_p[c, s] = jnp.exp2(diag_scores[c, s] - m0).astype(BF16)
        diag_m.append(m0)
    stats = []
    for s in range(nsub):
        acc = _dot(diag_vt(0), diag_p[0, s])
        for c in range(1, s + 1):
            acc = acc + _dot(diag_vt(c), diag_p[c, s])
        stats.append((diag_m[s], acc))

    nslot = s_ref.shape[0]
    bpq = tq // kb_keys

    def body(t, carry):
        stats, maxima = carry
        for c in range(nslot):
            cur = t * nslot + c
            nxt = jnp.minimum(cur + 1, i * bpq - 1)
            kb = k_ref[pl.ds(pl.multiple_of(nxt * kb_keys, kb_keys), kb_keys), :]
            vt = vt_ref[cur]
            new_stats, maxima_next = [], []
            for s in range(nsub):
                st = _dot(kb, qs[s])
                s_ref[(c + 1) % nslot, s] = st
                maxima_next.append(jnp.max(st, axis=0, keepdims=True))
                m_prev, acc = stats[s]
                m_new = jnp.maximum(m_prev, maxima[s])
                p = jnp.exp2(s_ref[c, s] - m_new).astype(BF16)
                new_stats.append((m_new, jnp.exp2(m_prev - m_new) * acc + _dot(vt, p)))
            stats, maxima = tuple(new_stats), tuple(maxima_next)
        return stats, maxima

    stats, _ = lax.fori_loop(0, i * (bpq // nslot), body, (tuple(stats), first_maxima))
    for s in range(nsub):
        _, acc = stats[s]
        o_ref[s * sub:(s + 1) * sub, :] = (acc[:dv] / acc[dv:dv + 1]).T.astype(o_ref.dtype)


def _rot_half_cols(w):
    half = w.shape[-1] // 2
    return jnp.concatenate([-w[..., half:], w[..., :half]], axis=-1)


def _mla_layer(h, pos, w_in, q_norm_g, kv_norm_g, w_uq, w_ukv):
    bsz, seq, d = h.shape
    heads, nope, rope_d, vd = MLA_HEADS, MLA_NOPE, MLA_ROPE, MLA_V
    dqk = nope + 2 * rope_d
    half = rope_d // 2
    inv = ROPE_BASE ** (-jnp.arange(half, dtype=F32) / half)
    inv = jnp.tile(inv, 4).reshape(1, 2 * rope_d)

    w_down = w_in[:, :MLA_Q_RANK + MLA_KV_RANK]
    w_pe = w_in[:, MLA_Q_RANK + MLA_KV_RANK:]
    zero = jnp.zeros_like(w_pe)
    w_pe_rot = _rot_half_cols(w_pe)
    w_kpe = jnp.concatenate([w_pe, zero, zero, w_pe, w_pe_rot, zero, zero, w_pe_rot], axis=1)
    w_uq3 = w_uq.reshape(MLA_Q_RANK, heads, nope + rope_d)
    w_qp = w_uq3[:, :, nope:]
    w_uq_all = jnp.concatenate(
        [w_uq3[:, :, :nope].reshape(MLA_Q_RANK, heads * nope),
         w_qp.reshape(MLA_Q_RANK, heads * rope_d),
         _rot_half_cols(w_qp).reshape(MLA_Q_RANK, heads * rope_d)], axis=1)
    w_ukv3 = w_ukv.reshape(MLA_KV_RANK, heads, nope + vd)
    w_ukv_all = jnp.concatenate(
        [w_ukv3[:, :, :nope].reshape(MLA_KV_RANK, heads * nope),
         w_ukv3[:, :, nope:].reshape(MLA_KV_RANK, heads * vd)], axis=1)

    tile = MLA_PROJ_TILE
    per_tk = ATTN_TK // tile
    bf = lambda t: t.astype(BF16)
    q, k, v = pl.pallas_call(
        _mla_proj_kernel,
        out_shape=(jax.ShapeDtypeStruct((bsz, heads, dqk, seq), BF16),
                   jax.ShapeDtypeStruct((bsz, heads, seq, dqk), BF16),
                   jax.ShapeDtypeStruct((bsz, heads, seq // ATTN_TK, vd + ATTN_VT_PAD, ATTN_TK), BF16)),
        grid=(bsz, seq // tile),
        in_specs=[
            pl.BlockSpec((None, tile, d), lambda bi, ti: (bi, ti, 0)),
            pl.BlockSpec((None, tile, 1), lambda bi, ti: (bi, ti, 0)),
            _resident((1, 2 * rope_d)),
            _resident(w_down.shape), _resident(w_kpe.shape),
            _resident((1, MLA_Q_RANK)), _resident((1, MLA_KV_RANK)),
            _resident(w_uq_all.shape), _resident(w_ukv_all.shape),
        ],
        out_specs=(pl.BlockSpec((None, heads, dqk, tile), lambda bi, ti: (bi, 0, 0, ti)),
                   pl.BlockSpec((None, heads, tile, dqk), lambda bi, ti: (bi, 0, ti, 0)),
                   pl.BlockSpec((None, heads, None, vd + ATTN_VT_PAD, tile),
                                lambda bi, ti: (bi, 0, ti // per_tk, 0, ti % per_tk))),
        compiler_params=pltpu.CompilerParams(
            dimension_semantics=("parallel", "parallel"), vmem_limit_bytes=V7X_VMEM_LIMIT),
        name="mla_proj",
    )(h, pos, inv, bf(w_down), bf(w_kpe), q_norm_g.reshape(1, -1), kv_norm_g.reshape(1, -1),
      bf(w_uq_all), bf(w_ukv_all))

    attn = pl.pallas_call(
        _attn_kernel,
        out_shape=jax.ShapeDtypeStruct((bsz, seq, heads * vd), BF16),
        grid=(bsz, heads, seq // ATTN_TQ),
        in_specs=[
            pl.BlockSpec((None, None, dqk, ATTN_TQ), lambda bi, hi, qi: (bi, hi, 0, qi)),
            pl.BlockSpec((None, None, seq, dqk), lambda bi, hi, qi: (bi, hi, 0, 0)),
            pl.BlockSpec((None, None, seq // ATTN_TK, vd + ATTN_VT_PAD, ATTN_TK),
                         lambda bi, hi, qi: (bi, hi, 0, 0, 0)),
        ],
        out_specs=pl.BlockSpec((None, ATTN_TQ, vd), lambda bi, hi, qi: (bi, qi, hi)),
        scratch_shapes=[pltpu.VMEM((ATTN_SLOTS, ATTN_TQ // ATTN_SUB, ATTN_TK, ATTN_SUB),
                                   F32)],
        compiler_params=pltpu.CompilerParams(
            dimension_semantics=("parallel", "parallel", "arbitrary"),
            vmem_limit_bytes=V7X_VMEM_LIMIT),
        name="mla_attention",
    )(q, k, v)

    return attn.reshape(bsz * seq, heads * vd)


def kernel(x, positions, ret_w_in, ret_gn_g, ret_w_o, lru_w_in, lru_conv_w, lru_conv_b, lru_w_a,
           lru_b_a, lru_w_x, lru_b_x, lru_lam, lru_w_o, mla_w_in, mla_q_norm, mla_kv_norm,
           mla_w_uq, mla_w_ukv, mla_w_o, ln_g, ln_b, mlp_w1, mlp_w2):
    bsz, seq, d = x.shape
    pos = positions.astype(F32).reshape(bsz, seq, 1)
    ret_w_in_b, ret_w_o_b = ret_w_in.astype(BF16), ret_w_o.astype(BF16)
    mlp_w1_b, mlp_w2_b = mlp_w1.astype(BF16), mlp_w2.astype(BF16)
    h = x
    for i in range(DEPTH):
        kind, j = i % N_MIXERS, i // N_MIXERS
        pre = None
        if kind == 0:
            h = _retention_layer(h, pos, ret_w_in_b, ret_gn_g[j], ret_w_o_b, j,
                                 ln_g[i, 0], ln_b[i, 0])
        elif kind == 1:
            h = _rglru_layer(h, lru_w_in[j], lru_conv_w[j], lru_conv_b[j], lru_w_a[j],
                             lru_b_a[j], lru_w_x[j], lru_b_x[j], lru_lam[j], lru_w_o[j],
                             ln_g[i, 0], ln_b[i, 0], mlp_w1_b, mlp_w2_b, i,
                             ln_g[i, 1], ln_b[i, 1])
            continue
        else:
            y = _mla_layer(h, pos, mla_w_in[j], mla_q_norm[j], mla_kv_norm[j], mla_w_uq[j],
                           mla_w_ukv[j])
            pre = (y, mla_w_o[j].astype(BF16), ln_g[i, 0], ln_b[i, 0])
        h = _mlp_layer(h.reshape(bsz * seq, d), mlp_w1_b, mlp_w2_b, i,
                       ln_g[i, 1], ln_b[i, 1], pre=pre).reshape(bsz, seq, d)
    return h
```
